```python
import jax, jax.numpy as jnp
from jax import lax
import numpy as np

D_MODEL = 1024
BATCH = 16
SEQ = 2048
DEPTH = 1
DEC_BATCH = 32
DEC_SEQ = 4
PAST_LEN = 16384
PAGE_SIZE = 128

ATTN_HEADS = 8
HEAD_DIM = 64
ATTN_WIDTH = ATTN_HEADS * HEAD_DIM
REC_WIDTH = D_MODEL - ATTN_WIDTH
REC_BLOCKS = 8
REC_BLOCK_DIM = REC_WIDTH // REC_BLOCKS
CONV_WIDTH = 4
RG_C = 8.0
FFN_HIDDEN = ((8 * D_MODEL + 3 * 256 - 1) // (3 * 256)) * 256
PLE_DIM = 256
Q_BLOCK = 128
NORM_EPS = 1e-6
N_IN = 3 * ATTN_WIDTH + ATTN_HEADS + 2 * REC_WIDTH
FORGET_BIAS = 6.0
CACHE_FORGET_LOGIT = 8.0

kernel_name = "hymba_fox_rglru_decoder_step"


def rmsnorm(x, g):
    xf = x.astype(jnp.float32)
    y = xf * lax.rsqrt(jnp.mean(xf * xf, axis=-1, keepdims=True) + NORM_EPS)
    return (y * g.astype(jnp.float32)).astype(x.dtype)


def split_projection(h, w_in, b_f, g_q, g_k):
    B, T, _ = h.shape
    proj = h @ w_in
    o = 0
    q = proj[..., o:o + ATTN_WIDTH].reshape(B, T, ATTN_HEADS, HEAD_DIM); o += ATTN_WIDTH
    k = proj[..., o:o + ATTN_WIDTH].reshape(B, T, ATTN_HEADS, HEAD_DIM); o += ATTN_WIDTH
    v = proj[..., o:o + ATTN_WIDTH].reshape(B, T, ATTN_HEADS, HEAD_DIM); o += ATTN_WIDTH
    f = proj[..., o:o + ATTN_HEADS]; o += ATTN_HEADS
    xr = proj[..., o:o + REC_WIDTH]; o += REC_WIDTH
    gr = proj[..., o:o + REC_WIDTH]
    q = rmsnorm(q, g_q)
    k = rmsnorm(k, g_k)
    logf = jax.nn.log_sigmoid(f.astype(jnp.float32) + b_f.astype(jnp.float32))
    return q, k, v, logf, xr, gr


def fox_prompt(q, k, v, logf):
    B, S, H, Dh = q.shape
    scale = HEAD_DIM ** -0.5
    cum = jnp.cumsum(logf, axis=1).transpose(0, 2, 1)
    qpos = jnp.arange(Q_BLOCK)
    outs = []
    for blk in range(S // Q_BLOCK):
        qs = blk * Q_BLOCK
        qe = qs + Q_BLOCK
        s = jnp.einsum('bqhd,bkhd->bhqk', q[:, qs:qe], k[:, :qe]).astype(jnp.float32) * scale
        bias = cum[:, :, qs:qe, None] - cum[:, :, None, :qe]
        causal = (qs + qpos)[:, None] >= jnp.arange(qe)[None, :]
        s = jnp.where(causal, s + bias, -jnp.inf)
        pr = jax.nn.softmax(s, axis=-1).astype(v.dtype)
        outs.append(jnp.einsum('bhqk,bkhd->bqhd', pr, v[:, :qe]))
    return jnp.concatenate(outs, axis=1).reshape(B, S, ATTN_WIDTH)


def fox_sample(q, k, v, logf, k_past, v_past, logf_past):
    B, T, H, Dh = q.shape
    P = k_past.shape[1]
    scale = HEAD_DIM ** -0.5
    lp = logf_past.astype(jnp.float32)
    suffix = lax.cumsum(lp, axis=1, reverse=True) - lp
    cn = jnp.cumsum(logf, axis=1).transpose(0, 2, 1)
    s_past = (jnp.einsum('bthd,bshd->bhts', q, k_past).astype(jnp.float32) * scale
              + suffix.transpose(0, 2, 1)[:, :, None, :] + cn[:, :, :, None])
    s_new = (jnp.einsum('bthd,bshd->bhts', q, k).astype(jnp.float32) * scale
             + cn[:, :, :, None] - cn[:, :, None, :])
    causal = jnp.arange(T)[:, None] >= jnp.arange(T)[None, :]
    s_new = jnp.where(causal, s_new, -jnp.inf)
    pr = jax.nn.softmax(jnp.concatenate([s_past, s_new], axis=-1), axis=-1).astype(v.dtype)
    o = (jnp.einsum('bhts,bshd->bthd', pr[..., :P], v_past)
         + jnp.einsum('bhts,bshd->bthd', pr[..., P:], v))
    return o.reshape(B, T, ATTN_WIDTH)


def rglru_branch(xr, gr, conv_prev, h_prev, w_conv, b_conv, w_a, b_a, w_x, b_x, lam):
    B, T, W = xr.shape
    xpad = jnp.concatenate([conv_prev.astype(xr.dtype), xr], axis=1)
    xc = b_conv.astype(xr.dtype) + xpad[:, 0:T] * w_conv[0]
    for j in range(1, CONV_WIDTH):
        xc = xc + xpad[:, j:j + T] * w_conv[j]
    new_conv = xpad[:, T:]
    xb = xc.reshape(B, T, REC_BLOCKS, REC_BLOCK_DIM)
    r = jax.nn.sigmoid(jnp.einsum('btnd,nde->btne', xb, w_a).reshape(B, T, W).astype(jnp.float32)
                       + b_a.astype(jnp.float32))
    i = jax.nn.sigmoid(jnp.einsum('btnd,nde->btne', xb, w_x).reshape(B, T, W).astype(jnp.float32)
                       + b_x.astype(jnp.float32))
    log_a = -RG_C * r * jax.nn.softplus(-lam.astype(jnp.float32))
    a = jnp.exp(log_a)
    bvec = jnp.sqrt(-jnp.expm1(2.0 * log_a)) * (i * xc.astype(jnp.float32))

    def step(h, ab):
        a_t, b_t = ab
        h = a_t * h + b_t
        return h, h

    h_last, hs = lax.scan(step, h_prev.astype(jnp.float32),
                          (a.transpose(1, 0, 2), bvec.transpose(1, 0, 2)))
    hs = hs.transpose(1, 0, 2)
    y = (hs * jax.nn.gelu(gr.astype(jnp.float32))).astype(xr.dtype)
    return y, new_conv, h_last


def layer_tail(x, att, rec, p_i, g_out_attn, g_out_rec, w_out, g_ffn, w_ffn_in, w_ffn_out,
               w_ple, g_ple, w_ple_gate):
    mix = jnp.concatenate([rmsnorm(att, g_out_attn), rmsnorm(rec, g_out_rec)], axis=-1) @ w_out
    x = x + mix
    gu = rmsnorm(x, g_ffn) @ w_ffn_in
    x = x + (jax.nn.silu(gu[..., :FFN_HIDDEN]) * gu[..., FFN_HIDDEN:]) @ w_ffn_out
    gate = jax.nn.sigmoid((rmsnorm(x, g_ple) @ w_ple_gate).astype(jnp.float32))
    x = x + ((p_i.astype(x.dtype) @ w_ple).astype(jnp.float32) * gate).astype(x.dtype)
    return x


def setup_inputs(seed: int = 0) -> dict:
    key = jax.random.key(seed)
    ks = iter(jax.random.split(key, 40))
    nrm = lambda shape, s=1.0: jax.random.normal(next(ks), shape, jnp.float32) * s
    gain = lambda shape: 1.0 + 0.02 * jax.random.normal(next(ks), shape, jnp.float32)
    n_pages = PAST_LEN // PAGE_SIZE
    n_used = DEC_BATCH * n_pages
    n_phys = (5 * n_used + 3) // 4
    perm = jax.random.permutation(next(ks), n_phys)[:n_used]
    page_table = perm.reshape(DEC_BATCH, n_pages).astype(jnp.int32)
    u = jax.random.uniform(next(ks), (DEPTH, REC_WIDTH), jnp.float32, 0.9, 0.999)
    sig = u ** (1.0 / RG_C)
    lam = jnp.log(sig) - jnp.log1p(-sig)
    return {
        "x_prompt": nrm((BATCH, SEQ, D_MODEL)),
        "x_sample": nrm((DEC_BATCH, DEC_SEQ, D_MODEL)),
        "p_prompt": nrm((DEPTH, BATCH, SEQ, PLE_DIM)),
        "p_sample": nrm((DEPTH, DEC_BATCH, DEC_SEQ, PLE_DIM)),
        "cache_k": nrm((DEPTH, n_phys, PAGE_SIZE, ATTN_HEADS, HEAD_DIM)),
        "cache_v": nrm((DEPTH, n_phys, PAGE_SIZE, ATTN_HEADS, HEAD_DIM)),
        "cache_logf": jax.nn.log_sigmoid(CACHE_FORGET_LOGIT + nrm((DEPTH, n_phys, PAGE_SIZE, ATTN_HEADS), 0.5)),
        "state_conv": nrm((DEPTH, DEC_BATCH, CONV_WIDTH - 1, REC_WIDTH)),
        "state_h": nrm((DEPTH, DEC_BATCH, REC_WIDTH), 0.5),
        "page_table": page_table,
        "g_mix": gain((DEPTH, D_MODEL)),
        "w_in": nrm((DEPTH, D_MODEL, N_IN), D_MODEL ** -0.5),
        "b_f": FORGET_BIAS + nrm((DEPTH, ATTN_HEADS), 0.5),
        "g_q": gain((DEPTH, ATTN_HEADS, HEAD_DIM)),
        "g_k": gain((DEPTH, ATTN_HEADS, HEAD_DIM)),
        "w_conv": nrm((DEPTH, CONV_WIDTH, REC_WIDTH), CONV_WIDTH ** -0.5),
        "b_conv": nrm((DEPTH, REC_WIDTH), 0.02),
        "w_a": nrm((DEPTH, REC_BLOCKS, REC_BLOCK_DIM, REC_BLOCK_DIM), REC_BLOCK_DIM ** -0.5),
        "b_a": nrm((DEPTH, REC_WIDTH), 0.02),
        "w_x": nrm((DEPTH, REC_BLOCKS, REC_BLOCK_DIM, REC_BLOCK_DIM), REC_BLOCK_DIM ** -0.5),
        "b_x": nrm((DEPTH, REC_WIDTH), 0.02),
        "lam": lam,
        "g_out_attn": gain((DEPTH, ATTN_WIDTH)),
        "g_out_rec": gain((DEPTH, REC_WIDTH)),
        "w_out": nrm((DEPTH, D_MODEL, D_MODEL), D_MODEL ** -0.5),
        "g_ffn": gain((DEPTH, D_MODEL)),
        "w_ffn_in": nrm((DEPTH, D_MODEL, 2 * FFN_HIDDEN), D_MODEL ** -0.5),
        "w_ffn_out": nrm((DEPTH, FFN_HIDDEN, D_MODEL), FFN_HIDDEN ** -0.5),
        "w_ple": nrm((DEPTH, PLE_DIM, D_MODEL), PLE_DIM ** -0.5),
        "g_ple": gain((DEPTH, D_MODEL)),
        "w_ple_gate": nrm((DEPTH, D_MODEL, D_MODEL), D_MODEL ** -0.5),
    }


def reference(x_prompt, x_sample, p_prompt, p_sample, cache_k, cache_v, cache_logf, state_conv, state_h,
              page_table, g_mix, w_in, b_f, g_q, g_k, w_conv, b_conv, w_a, b_a, w_x, b_x, lam,
              g_out_attn, g_out_rec, w_out, g_ffn, w_ffn_in, w_ffn_out, w_ple, g_ple, w_ple_gate):
    xp, xs = x_prompt, x_sample
    B, S, _ = xp.shape
    DB, T, _ = xs.shape
    n_pages = page_table.shape[1]
    P = n_pages * PAGE_SIZE
    kp_l, vp_l, lp_l, cp_l, hp_l = [], [], [], [], []
    ks_l, vs_l, ls_l, cs_l, hs_l = [], [], [], [], []
    for l in range(DEPTH):
        tail = (g_out_attn[l], g_out_rec[l], w_out[l], g_ffn[l], w_ffn_in[l], w_ffn_out[l],
                w_ple[l], g_ple[l], w_ple_gate[l])
        rec_w = (w_conv[l], b_conv[l], w_a[l], b_a[l], w_x[l], b_x[l], lam[l])
        q, k, v, lf, xr, gr = split_projection(rmsnorm(xp, g_mix[l]), w_in[l], b_f[l], g_q[l], g_k[l])
        att = fox_prompt(q, k, v, lf)
        conv0 = jnp.zeros((B, CONV_WIDTH - 1, REC_WIDTH), xr.dtype)
        h0 = jnp.zeros((B, REC_WIDTH), jnp.float32)
        rec, c_new, h_new = rglru_branch(xr, gr, conv0, h0, *rec_w)
        xp = layer_tail(xp, att, rec, p_prompt[l], *tail)
        kp_l.append(k); vp_l.append(v); lp_l.append(lf); cp_l.append(c_new); hp_l.append(h_new)
        q, k, v, lf, xr, gr = split_projection(rmsnorm(xs, g_mix[l]), w_in[l], b_f[l], g_q[l], g_k[l])
        k_past = cache_k[l][page_table].reshape(DB, P, ATTN_HEADS, HEAD_DIM)
        v_past = cache_v[l][page_table].reshape(DB, P, ATTN_HEADS, HEAD_DIM)
        lf_past = cache_logf[l][page_table].reshape(DB, P, ATTN_HEADS)
        att = fox_sample(q, k, v, lf, k_past, v_past, lf_past)
        rec, c_new, h_new = rglru_branch(xr, gr, state_conv[l], state_h[l], *rec_w)
        xs = layer_tail(xs, att, rec, p_sample[l], *tail)
        ks_l.append(k); vs_l.append(v); ls_l.append(lf); cs_l.append(c_new); hs_l.append(h_new)
    return (xp, xs,
            jnp.stack(kp_l), jnp.stack(vp_l), jnp.stack(lp_l), jnp.stack(cp_l), jnp.stack(hp_l),
            jnp.stack(ks_l), jnp.stack(vs_l), jnp.stack(ls_l), jnp.stack(cs_l), jnp.stack(hs_l))
```

```python
import functools

import numpy as np
import jax
import jax.numpy as jnp
from jax import lax
from jax.experimental import pallas as pl
from jax.experimental.pallas import tpu as pltpu

RG_C = 8.0
NORM_EPS = 1e-6
CONV_WIDTH = 4

V7X_LANES = 128
V7X_SUBLANES = 8
V7X_MXU_DIM = 256
V7X_VMEM_LIMIT_BYTES = 56 * 1024 * 1024

PROJ_ROWS = 512
ATTN_BLOCK = 512
PAGES_PER_STEP = 8
FFN_CHUNK = 512

_F32 = jnp.float32
_BF16 = jnp.bfloat16
_NT = (((1,), (1,)), ((), ()))


def _dot(a, b):
    return jnp.dot(a, b, preferred_element_type=_F32)


def _dot_nt(a, b):
    return lax.dot_general(a, b, _NT, preferred_element_type=_F32)


def _rms(x, g):
    return x * lax.rsqrt(jnp.mean(x * x, axis=-1, keepdims=True) + NORM_EPS) * g


def _split3(z):
    hi = z.astype(_BF16).astype(_F32)
    r = z - hi
    mid = r.astype(_BF16).astype(_F32)
    lo = r - mid
    return hi, mid, lo


def _stack3(z, rows):
    hi, mid, lo = _split3(z)
    pad = jnp.zeros((rows - 3 * V7X_SUBLANES, z.shape[1]), _F32)
    return jnp.concatenate([hi, mid, lo, pad], axis=0).astype(_BF16)


def _sum3(r):
    return r[0:8] + r[8:16] + r[16:24]


def _head_norm(z, g, gmat):
    zz = (z * z).astype(_BF16)
    half = gmat.shape[0]
    ms = jnp.concatenate([_dot(zz[:, :half], gmat), _dot(zz[:, half:], gmat)], axis=1)
    return z * lax.rsqrt(ms + NORM_EPS) * g


def _const_spec(shape):
    nd = len(shape)
    return pl.BlockSpec(shape, lambda *_: (0,) * nd, pipeline_mode=pl.Buffered(1))


def _params(n_axes):
    return pltpu.CompilerParams(
        dimension_semantics=("arbitrary",) * n_axes,
        vmem_limit_bytes=V7X_VMEM_LIMIT_BYTES,
    )


def _proj_kernel(x_ref, gmix_ref, w_ref, wft_ref, bf_ref, gq_ref, gk_ref, gmat_ref, cum_ref,
                 aq_ref, ak_ref, cq_ref, ck_ref,
                 k_out, v_out, lft_out, qs_out, kb_out, vt_out, augq_out, augk_out, xr_out, gr_out,
                 carry_ref, *, tiles_per_seq, aw, rw):
    i = pl.program_id(0)
    tm = x_ref.shape[0]

    @pl.when(i % tiles_per_seq == 0)
    def _():
        carry_ref[...] = jnp.zeros_like(carry_ref)

    hn = _rms(x_ref[...], gmix_ref[...]).astype(_BF16)
    proj = _dot(hn, w_ref[...])
    q = proj[:, 0:aw]
    k = proj[:, aw:2 * aw]
    v = proj[:, 2 * aw:3 * aw]
    xr_out[...] = proj[:, 3 * aw:3 * aw + rw]
    gr_out[...] = proj[:, 3 * aw + rw:3 * aw + 2 * rw]

    gmat = gmat_ref[...]
    qn = _head_norm(q, gq_ref[...], gmat)
    kn = _head_norm(k, gk_ref[...], gmat)
    k_out[...] = kn
    v_out[...] = v
    qs_out[...] = qn.astype(_BF16)
    kb_out[...] = kn.astype(_BF16)
    vt_out[...] = v.T.astype(_BF16)

    ft = _dot_nt(wft_ref[...], hn)
    lft = jax.nn.log_sigmoid(ft[0:8] + bf_ref[...])
    lft_out[...] = lft

    p3 = _stack3(lft, V7X_LANES)
    blk = cum_ref.shape[0]
    carry = carry_ref[...]
    cums = []
    for c in range(tm // blk):
        r = _dot(p3[:, c * blk:(c + 1) * blk], cum_ref[...])
        cums.append(_sum3(r[:, :blk]) + carry)
        carry = carry + _sum3(r[:, blk:])
    carry_ref[...] = carry
    cum = jnp.concatenate(cums, axis=1)

    c3 = _stack3(cum, V7X_LANES)
    qat = _dot(aq_ref[...], c3) + cq_ref[...]
    kat = _dot(ak_ref[...], c3) + ck_ref[...]
    augq_out[...] = qat.T.astype(_BF16)
    augk_out[...] = kat.T.astype(_BF16)


def _bias_placement(n_heads):
    aq = np.zeros((V7X_LANES, V7X_LANES), np.float32)
    ak = np.zeros((V7X_LANES, V7X_LANES), np.float32)
    cq = np.zeros((V7X_LANES, 1), np.float32)
    ck = np.zeros((V7X_LANES, 1), np.float32)
    for h in range(n_heads):
        for j in range(3):
            aq[h * 8 + j, j * 8 + h] = 1.0
            cq[h * 8 + 3 + j, 0] = 1.0
            ak[h * 8 + 3 + j, j * 8 + h] = -1.0
            ck[h * 8 + j, 0] = 1.0
    return aq, ak, cq, ck


def _group_mean_matrix(head_dim):
    idx = np.arange(V7X_MXU_DIM) // head_dim
    return (idx[:, None] == idx[None, :]).astype(np.float32) / head_dim


def _prefix_matrix():
    idx = np.arange(V7X_MXU_DIM)
    incl = (idx[:, None] <= idx[None, :]).astype(np.float32)
    return np.concatenate([incl, np.ones_like(incl)], axis=1)


def _proj_prompt(x2d, seq, gmix, w_main, wft, bf_col, gq, gk, n_heads, head_dim, rw):
    n, d = x2d.shape
    tm = PROJ_ROWS
    aw = n_heads * head_dim
    aq, ak, cq, ck = _bias_placement(n_heads)
    consts = [
        gmix, w_main, wft, bf_col, gq, gk,
        jnp.asarray(_group_mean_matrix(head_dim), _BF16),
        jnp.asarray(_prefix_matrix(), _BF16),
        jnp.asarray(aq, _BF16), jnp.asarray(ak, _BF16), jnp.asarray(cq), jnp.asarray(ck),
    ]
    row = lambda w: pl.BlockSpec((tm, w), lambda i: (i, 0))
    out_shape = (
        jax.ShapeDtypeStruct((n, aw), _F32),
        jax.ShapeDtypeStruct((n, aw), _F32),
        jax.ShapeDtypeStruct((V7X_SUBLANES, n), _F32),
        jax.ShapeDtypeStruct((n, aw), _BF16),
        jax.ShapeDtypeStruct((n, aw), _BF16),
        jax.ShapeDtypeStruct((aw, n), _BF16),
        jax.ShapeDtypeStruct((n, V7X_LANES), _BF16),
        jax.ShapeDtypeStruct((n, V7X_LANES), _BF16),
        jax.ShapeDtypeStruct((n, rw), _F32),
        jax.ShapeDtypeStruct((n, rw), _F32),
    )
    out_specs = (
        row(aw), row(aw), pl.BlockSpec((V7X_SUBLANES, tm), lambda i: (0, i)),
        row(aw), row(aw), pl.BlockSpec((aw, tm), lambda i: (0, i)),
        row(V7X_LANES), row(V7X_LANES), row(rw), row(rw),
    )
    return pl.pallas_call(
        functools.partial(_proj_kernel, tiles_per_seq=seq // tm, aw=aw, rw=rw),
        grid=(n // tm,),
        in_specs=[row(d)] + [_const_spec(c.shape) for c in consts],
        out_specs=out_specs,
        out_shape=out_shape,
        scratch_shapes=[pltpu.VMEM((V7X_SUBLANES, V7X_MXU_DIM), _F32)],
        compiler_params=_params(1),
        name="proj_prompt",
    )(x2d, *consts)


def _fox_kernel(q_ref, aq_ref, k_ref, ak_ref, vt_ref, o_ref, qf_ref, m_ref, acc_ref, *, head_dim):
    hp = pl.program_id(1)
    qi = pl.program_id(2)
    ki = pl.program_id(3)
    bq = q_ref.shape[0]
    bk = k_ref.shape[0]

    @pl.when(ki == 0)
    def _init():
        lane = lax.broadcasted_iota(jnp.int32, (bq, V7X_LANES), 1)
        q2 = q_ref[...]
        a = aq_ref[...]
        zero = jnp.zeros_like(q2)
        for h in range(2):
            qh = jnp.where((lane >= head_dim * h) & (lane < head_dim * (h + 1)), q2, zero)
            lo = (2 * hp + h) * 8
            ah = jnp.where((lane >= lo) & (lane < lo + 8), a, zero)
            qf_ref[h] = jnp.concatenate([qh, ah], axis=1)
        m_ref[...] = jnp.full_like(m_ref, -jnp.inf)
        acc_ref[...] = jnp.zeros_like(acc_ref)

    @pl.when(ki <= qi)
    def _step():
        kf = jnp.concatenate([k_ref[...], ak_ref[...]], axis=1)
        kpos = ki * bk + lax.broadcasted_iota(jnp.int32, (bk, bq), 0)
        qpos = qi * bq + lax.broadcasted_iota(jnp.int32, (bk, bq), 1)
        causal = kpos <= qpos
        ones_blk = (lax.broadcasted_iota(jnp.int32, (16, bk), 0) == 0).astype(_BF16)
        for h in range(2):
            s = _dot_nt(kf, qf_ref[h])
            s = jnp.where(causal, s, -jnp.inf)
            m_old = m_ref[h]
            m_new = jnp.maximum(m_old, jnp.max(s, axis=0, keepdims=True))
            alpha = jnp.exp(m_old - m_new)
            p = jnp.exp(s - m_new).astype(_BF16)
            lhs = jnp.concatenate([vt_ref[h * head_dim:(h + 1) * head_dim, :], ones_blk], axis=0)
            acc_ref[h] = alpha * acc_ref[h] + _dot(lhs, p)
            m_ref[h] = m_new

    @pl.when(ki == qi)
    def _fin():
        outs = []
        for h in range(2):
            a = acc_ref[h]
            outs.append(a[0:head_dim] / a[head_dim:head_dim + 1])
        o_ref[...] = jnp.concatenate(outs, axis=0).T


def _fox_prompt(qs, augq, kb, augk, vt, batch, seq, n_heads, head_dim):
    aw = n_heads * head_dim
    bq = bk = ATTN_BLOCK
    nq = seq // bq
    pair = 2 * head_dim
    qs3 = qs.reshape(batch, seq, aw)
    kb3 = kb.reshape(batch, seq, aw)
    aq3 = augq.reshape(batch, seq, V7X_LANES)
    ak3 = augk.reshape(batch, seq, V7X_LANES)
    kmap = lambda b, hp, qi, ki: (b, jnp.minimum(ki, qi), hp)
    return pl.pallas_call(
        functools.partial(_fox_kernel, head_dim=head_dim),
        grid=(batch, aw // pair, nq, nq),
        in_specs=[
            pl.BlockSpec((None, bq, pair), lambda b, hp, qi, ki: (b, qi, hp)),
            pl.BlockSpec((None, bq, V7X_LANES), lambda b, hp, qi, ki: (b, qi, 0)),
            pl.BlockSpec((None, bk, pair), kmap),
            pl.BlockSpec((None, bk, V7X_LANES), lambda b, hp, qi, ki: (b, jnp.minimum(ki, qi), 0)),
            pl.BlockSpec((pair, bk), lambda b, hp, qi, ki: (hp, b * nq + jnp.minimum(ki, qi))),
        ],
        out_specs=pl.BlockSpec((None, bq, pair), lambda b, hp, qi, ki: (b, qi, hp)),
        out_shape=jax.ShapeDtypeStruct((batch, seq, aw), _F32),
        scratch_shapes=[
            pltpu.VMEM((2, bq, 2 * V7X_LANES), _BF16),
            pltpu.VMEM((2, 1, bq), _F32),
            pltpu.VMEM((2, head_dim + 16, bq), _F32),
        ],
        compiler_params=_params(4),
        name="fox_prompt",
    )(qs3, aq3, kb3, ak3, vt)


def _rglru_kernel(xr_ref, gr_ref, p1_ref, p2_ref, p3_ref, h0_ref, wconv_ref, bconv_ref,
                  wa_ref, ba_ref, wx_ref, bx_ref, lam_ref, y_ref, h_ref, xlast_ref, hc_ref,
                  *, seg_len, tiles_per_seq):
    i = pl.program_id(0)
    tm, rw = xr_ref.shape
    sub = V7X_SUBLANES

    @pl.when(i % tiles_per_seq == 0)
    def _():
        xlast_ref[...] = jnp.zeros_like(xlast_ref)
        hc_ref[...] = jnp.zeros_like(hc_ref)

    xr = xr_ref[...]
    row = lax.broadcasted_iota(jnp.int32, (tm, rw), 0)
    t = ((i % tiles_per_seq) * tm + row) & (seg_len - 1)
    row8 = lax.broadcasted_iota(jnp.int32, (sub, rw), 0)
    xl = xlast_ref[...]
    prev = (p1_ref, p2_ref, p3_ref)
    shifted = []
    for k in range(1, CONV_WIDTH):
        r = pltpu.roll(xr, k, 0)
        head = jnp.where(row8 < k, pltpu.roll(xl, k, 0), r[0:sub])
        r = jnp.concatenate([head, r[sub:]], axis=0)
        shifted.append(jnp.where(t >= k, r, prev[k - 1][...]))
    xlast_ref[...] = xr[tm - sub:tm]

    w = wconv_ref[...]
    xc = bconv_ref[...] + shifted[2] * w[0:1]
    xc = xc + shifted[1] * w[1:2]
    xc = xc + shifted[0] * w[2:3]
    xc = xc + xr * w[3:4]

    xcb = xc.astype(_BF16)
    half = wa_ref.shape[1]

    def gate(w_ref, b_ref):
        z = jnp.concatenate([_dot(xcb[:, :half], w_ref[0]), _dot(xcb[:, half:], w_ref[1])], axis=1)
        return jax.nn.sigmoid(z + b_ref[...])

    r_gate = gate(wa_ref, ba_ref)
    i_gate = gate(wx_ref, bx_ref)
    log_a = -RG_C * r_gate * jax.nn.softplus(-lam_ref[...])
    a = jnp.exp(log_a)
    b = jnp.sqrt(-jnp.tanh(log_a) * (a * a + 1.0)) * (i_gate * xc)

    d = 1
    while d < min(tm, seg_len):
        valid = (row >= d) & (t >= d)
        a_s = pltpu.roll(a, d, 0)
        b_s = pltpu.roll(b, d, 0)
        b = jnp.where(valid, b + a * b_s, b)
        a = jnp.where(valid, a * a_s, a)
        d *= 2
    h_in = jnp.where(t <= row, h0_ref[...], hc_ref[...])
    h = a * h_in + b
    hc_ref[...] = h[tm - 1:tm]
    y_ref[...] = h * jax.nn.gelu(gr_ref[...])
    h_ref[...] = h[tm - h_ref.shape[0]:tm]


def _rglru(xr, gr, prevs, h0, seg_len, rec_w, all_h):
    n, rw = xr.shape
    tm = min(PROJ_ROWS, n)
    assert seg_len & (seg_len - 1) == 0, "sequence length must be a power of two"
    tiles_per_seq = max(seg_len // tm, 1)
    row = pl.BlockSpec((tm, rw), lambda i: (i, 0))
    if prevs is None:
        zeros = jnp.zeros((tm, rw), _F32)
        prevs, h0 = (zeros, zeros, zeros), zeros
        state_spec = _const_spec((tm, rw))
    else:
        state_spec = row
    h_rows = tm if all_h else V7X_SUBLANES
    return pl.pallas_call(
        functools.partial(_rglru_kernel, seg_len=seg_len, tiles_per_seq=tiles_per_seq),
        grid=(n // tm,),
        in_specs=[row, row] + [state_spec] * 4 + [_const_spec(c.shape) for c in rec_w],
        out_specs=(row, pl.BlockSpec((h_rows, rw), lambda i: (i, 0))),
        out_shape=(jax.ShapeDtypeStruct((n, rw), _F32),
                   jax.ShapeDtypeStruct((n // tm * h_rows, rw), _F32)),
        scratch_shapes=[pltpu.VMEM((V7X_SUBLANES, rw), _F32), pltpu.VMEM((1, rw), _F32)],
        compiler_params=_params(1),
        name="rglru",
    )(xr, gr, *prevs, h0, *rec_w)


def _tail_kernel(x_ref, att_ref, rec_ref, p_ref, goa_ref, gor_ref, wout_ref, gffn_ref, wg_ref, wu_ref,
                 wo_ref, wple_ref, gple_ref, wpg_ref, o_ref):
    aw = att_ref.shape[1]
    an = _rms(att_ref[...], goa_ref[...]).astype(_BF16)
    rn = _rms(rec_ref[...], gor_ref[...]).astype(_BF16)
    x1 = x_ref[...] + _dot(an, wout_ref[0:aw, :]) + _dot(rn, wout_ref[aw:, :])
    hn = _rms(x1, gffn_ref[...]).astype(_BF16)
    hidden = wg_ref.shape[1]
    ffn = None
    for c0 in range(0, hidden, FFN_CHUNK):
        cw = min(FFN_CHUNK, hidden - c0)
        g = _dot(hn, wg_ref[:, c0:c0 + cw])
        u = _dot(hn, wu_ref[:, c0:c0 + cw])
        part = _dot((jax.nn.silu(g) * u).astype(_BF16), wo_ref[c0:c0 + cw, :])
        ffn = part if ffn is None else ffn + part
    x2 = x1 + ffn
    gate = jax.nn.sigmoid(_dot(_rms(x2, gple_ref[...]).astype(_BF16), wpg_ref[...]))
    o_ref[...] = x2 + _dot(p_ref[...].astype(_BF16), wple_ref[...]) * gate


def _tail(x2d, att, rec, p2d, tail_w):
    n, d = x2d.shape
    tm = min(PROJ_ROWS, n)
    row = lambda w: pl.BlockSpec((tm, w), lambda i: (i, 0))
    return pl.pallas_call(
        _tail_kernel,
        grid=(n // tm,),
        in_specs=[row(d), row(att.shape[1]), row(rec.shape[1]), row(p2d.shape[1])]
        + [_const_spec(c.shape) for c in tail_w],
        out_specs=row(d),
        out_shape=jax.ShapeDtypeStruct((n, d), _F32),
        compiler_params=_params(1),
        name="tail",
    )(x2d, att, rec, p2d, *tail_w)


def _sproj_kernel(x_ref, gmix_ref, w_ref, wf_ref, bf_ref, gq_ref, gk_ref, gmat_ref, seg_ref,
                  q_out, k_out, v_out, lf_out, cn_out, xr_out, gr_out, *, aw, rw):
    hn = _rms(x_ref[...], gmix_ref[...]).astype(_BF16)
    proj = _dot(hn, w_ref[...])
    gmat = gmat_ref[...]
    q_out[...] = _head_norm(proj[:, 0:aw], gq_ref[...], gmat)
    k_out[...] = _head_norm(proj[:, aw:2 * aw], gk_ref[...], gmat)
    v_out[...] = proj[:, 2 * aw:3 * aw]
    xr_out[...] = proj[:, 3 * aw:3 * aw + rw]
    gr_out[...] = proj[:, 3 * aw + rw:3 * aw + 2 * rw]
    lf = jax.nn.log_sigmoid(_dot(hn, wf_ref[...]) + bf_ref[...])
    lf_out[...] = lf
    hi, mid, lo = _split3(lf)
    seg = seg_ref[...]
    cn_out[...] = _dot(seg, hi.astype(_BF16)) + _dot(seg, mid.astype(_BF16)) + _dot(seg, lo.astype(_BF16))


def _proj_sample(x2d, t_new, gmix, w_main, wf_pad, bf_row, gq, gk, n_heads, head_dim, rw):
    n, d = x2d.shape
    aw = n_heads * head_dim
    idx = np.arange(n)
    seg = ((idx[:, None] // t_new == idx[None, :] // t_new) & (idx[None, :] <= idx[:, None])).astype(np.float32)
    ins = [x2d, gmix, w_main, wf_pad, bf_row, gq, gk,
           jnp.asarray(_group_mean_matrix(head_dim), _BF16), jnp.asarray(seg, _BF16)]
    full = lambda s: pl.BlockSpec(s, lambda i: (0,) * len(s))
    widths = (aw, aw, aw, V7X_LANES, V7X_LANES, rw, rw)
    return pl.pallas_call(
        functools.partial(_sproj_kernel, aw=aw, rw=rw),
        grid=(1,),
        in_specs=[full(a.shape) for a in ins],
        out_specs=tuple(full((n, w)) for w in widths),
        out_shape=tuple(jax.ShapeDtypeStruct((n, w), _F32) for w in widths),
        compiler_params=_params(1),
        name="proj_sample",
    )(*ins)


def _sattn_kernel(pt_ref, q_ref, kn_ref, vn_ref, cnq_ref, cnk_ref, *refs, pps, t_new, head_dim):
    del pt_ref
    k_refs = refs[:pps]
    v_refs = refs[pps:2 * pps]
    lf_refs = refs[2 * pps:3 * pps]
    uo_ref, o_ref, qbd_ref, m_ref, l_ref, acc_ref, carry_ref = refs[3 * pps:]
    j = pl.program_id(1)
    nh, aw = V7X_SUBLANES, q_ref.shape[1]
    rows = t_new * nh
    page = k_refs[0].shape[0]
    head_mask = (lax.broadcasted_iota(jnp.int32, (nh, aw), 1) // head_dim
                 == lax.broadcasted_iota(jnp.int32, (nh, aw), 0))

    @pl.when(j == 0)
    def _init():
        q = q_ref[...]
        qbd = jnp.concatenate(
            [jnp.where(head_mask, jnp.broadcast_to(q[t:t + 1], (nh, aw)), 0.0) for t in range(t_new)],
            axis=0).astype(_BF16)
        qbd_ref[...] = qbd
        pad = jnp.zeros((page - kn_ref.shape[0], aw), _F32)
        kn = jnp.concatenate([kn_ref[...], pad], axis=0).astype(_BF16)
        vn = jnp.concatenate([vn_ref[...], pad], axis=0).astype(_BF16)
        s = _dot_nt(qbd, kn) + cnq_ref[...] - cnk_ref[...]
        t_row = lax.broadcasted_iota(jnp.int32, (rows, page), 0) // nh
        t_col = lax.broadcasted_iota(jnp.int32, (rows, page), 1)
        s = jnp.where(t_col <= t_row, s, -jnp.inf)
        m = jnp.max(s, axis=1, keepdims=True)
        p = jnp.exp(s - m)
        m_ref[...] = m
        l_ref[...] = jnp.sum(p, axis=1, keepdims=True)
        acc_ref[...] = _dot(p.astype(_BF16), vn)
        carry_ref[...] = jnp.zeros_like(carry_ref)

    qbd = qbd_ref[...]
    carry = carry_ref[...]
    scores = []
    for i in range(pps):
        s = _dot_nt(qbd, k_refs[i][...].astype(_BF16))
        r = _dot(_stack3(lf_refs[i][...], 4 * V7X_SUBLANES), uo_ref[...])
        bias = _sum3(r[:, :page]) + carry
        carry = carry + _sum3(r[:, page:])
        scores.append(s + jnp.concatenate([bias] * t_new, axis=0))
    carry_ref[...] = carry
    s_all = jnp.concatenate(scores, axis=1) + cnq_ref[...]
    m_old = m_ref[...]
    m_new = jnp.maximum(m_old, jnp.max(s_all, axis=1, keepdims=True))
    alpha = jnp.exp(m_old - m_new)
    p = jnp.exp(s_all - m_new)
    l_ref[...] = alpha * l_ref[...] + jnp.sum(p, axis=1, keepdims=True)
    v_all = jnp.concatenate([v_refs[i][...].astype(_BF16) for i in range(pps)], axis=0)
    acc_ref[...] = alpha * acc_ref[...] + _dot(p.astype(_BF16), v_all)
    m_ref[...] = m_new

    @pl.when(j == pl.num_programs(1) - 1)
    def _fin():
        a = acc_ref[...] / l_ref[...]
        for t in range(t_new):
            o_ref[t:t + 1, :] = jnp.sum(jnp.where(head_mask, a[t * nh:(t + 1) * nh], 0.0), axis=0, keepdims=True)


def _suffix_matrix(page):
    idx = np.arange(page)
    strict = (idx[:, None] > idx[None, :]).astype(np.float32)
    return np.concatenate([strict, np.ones_like(strict)], axis=1)


def _attn_sample(q, k_new, v_new, cn, cache_k, cache_v, cache_lft, page_table, t_new, n_heads, head_dim):
    db, n_pages = page_table.shape
    page = cache_k.shape[1]
    aw = n_heads * head_dim
    pps = PAGES_PER_STEP
    rows = t_new * n_heads
    pad_t = V7X_SUBLANES - t_new
    q3 = q.reshape(db, t_new, aw)
    kn = jnp.pad(k_new.reshape(db, t_new, aw), ((0, 0), (0, pad_t), (0, 0)))
    vn = jnp.pad(v_new.reshape(db, t_new, aw), ((0, 0), (0, pad_t), (0, 0)))
    cn3 = cn.reshape(db, t_new, n_heads)
    cnq = cn3.reshape(db, rows, 1)
    cnk = jnp.tile(jnp.swapaxes(cn3, 1, 2), (1, t_new, 1))
    cnk = jnp.pad(cnk, ((0, 0), (0, 0), (0, page - t_new)))

    def page_spec(width_shape, i):
        return pl.BlockSpec((None,) + width_shape,
                            lambda b, j, pt: (pt[b, n_pages - 1 - (j * pps + i)], 0, 0))

    per_b = lambda s: pl.BlockSpec((None,) + s, lambda b, j, pt: (b, 0, 0))
    uo = jnp.asarray(_suffix_matrix(page), _BF16)
    grid_spec = pltpu.PrefetchScalarGridSpec(
        num_scalar_prefetch=1,
        grid=(db, n_pages // pps),
        in_specs=[per_b((t_new, aw)), per_b((V7X_SUBLANES, aw)), per_b((V7X_SUBLANES, aw)),
                  per_b((rows, 1)), per_b((rows, page))]
        + [page_spec((page, aw), i) for i in range(pps)]
        + [page_spec((page, aw), i) for i in range(pps)]
        + [page_spec((V7X_SUBLANES, page), i) for i in range(pps)]
        + [pl.BlockSpec(uo.shape, lambda b, j, pt: (0, 0))],
        out_specs=per_b((t_new, aw)),
        scratch_shapes=[
            pltpu.VMEM((rows, aw), _BF16),
            pltpu.VMEM((rows, 1), _F32),
            pltpu.VMEM((rows, 1), _F32),
            pltpu.VMEM((rows, aw), _F32),
            pltpu.VMEM((V7X_SUBLANES, page), _F32),
        ],
    )
    out = pl.pallas_call(
        functools.partial(_sattn_kernel, pps=pps, t_new=t_new, head_dim=head_dim),
        grid_spec=grid_spec,
        out_shape=jax.ShapeDtypeStruct((db, t_new, aw), _F32),
        compiler_params=_params(2),
        name="attn_sample",
    )(page_table, q3, kn, vn, cnq, cnk, *([cache_k] * pps), *([cache_v] * pps), *([cache_lft] * pps), uo)
    return out.reshape(db * t_new, aw)


def _block_diag_pair(w):
    nb, dd, _ = w.shape
    per = V7X_MXU_DIM // dd
    tiles = []
    for half in range(nb // per):
        tile = jnp.zeros((V7X_MXU_DIM, V7X_MXU_DIM), w.dtype)
        for j in range(per):
            tile = lax.dynamic_update_slice(tile, w[half * per + j], (j * dd, j * dd))
        tiles.append(tile)
    return jnp.stack(tiles).astype(_BF16)


def kernel(x_prompt, x_sample, p_prompt, p_sample, cache_k, cache_v, cache_logf, state_conv, state_h, page_table, g_mix, w_in, b_f, g_q, g_k, w_conv, b_conv, w_a, b_a, w_x, b_x, lam, g_out_attn, g_out_rec, w_out, g_ffn, w_ffn_in, w_ffn_out, w_ple, g_ple, w_ple_gate):
    batch, seq, d_model = x_prompt.shape
    db, t_new, _ = x_sample.shape
    depth, n_heads, head_dim = g_q.shape
    aw = n_heads * head_dim
    rw = lam.shape[1]
    hidden = w_ffn_out.shape[1]
    n_phys, page = cache_k.shape[1], cache_k.shape[2]
    assert n_heads == V7X_SUBLANES and aw == 2 * V7X_MXU_DIM and rw == 2 * V7X_MXU_DIM
    assert seq % ATTN_BLOCK == 0 and seq % PROJ_ROWS == 0 and page_table.shape[1] % PAGES_PER_STEP == 0

    xp = x_prompt.reshape(batch * seq, d_model)
    xs = x_sample.reshape(db * t_new, d_model)
    outs = [[] for _ in range(10)]
    row = lambda a: a.reshape(1, -1)
    for l in range(depth):
        w_l = w_in[l]
        w_main = jnp.concatenate([w_l[:, :3 * aw], w_l[:, 3 * aw + n_heads:]], axis=1).astype(_BF16)
        w_f = w_l[:, 3 * aw:3 * aw + n_heads]
        wft = jnp.pad(w_f.T, ((0, 16 - n_heads), (0, 0))).astype(_BF16)
        wf_pad = jnp.pad(w_f, ((0, 0), (0, V7X_LANES - n_heads))).astype(_BF16)
        bf_col = b_f[l].reshape(n_heads, 1)
        bf_row = jnp.pad(b_f[l], (0, V7X_LANES - n_heads)).reshape(1, V7X_LANES)
        gmix = row(g_mix[l])
        gq = row(g_q[l]) * (head_dim ** -0.5)
        gk = row(g_k[l])
        rec_w = (w_conv[l], row(b_conv[l]), _block_diag_pair(w_a[l]), row(b_a[l]),
                 _block_diag_pair(w_x[l]), row(b_x[l]), row(lam[l]))
        tail_w = (row(g_out_attn[l]), row(g_out_rec[l]), w_out[l].astype(_BF16), row(g_ffn[l]),
                  w_ffn_in[l][:, :hidden].astype(_BF16), w_ffn_in[l][:, hidden:].astype(_BF16),
                  w_ffn_out[l].astype(_BF16), w_ple[l].astype(_BF16), row(g_ple[l]),
                  w_ple_gate[l].astype(_BF16))

        k, v, lft, qs, kb, vt, augq, augk, xr, gr = _proj_prompt(
            xp, seq, gmix, w_main, wft, bf_col, gq, gk, n_heads, head_dim, rw)
        att = _fox_prompt(qs, augq, kb, augk, vt, batch, seq, n_heads, head_dim).reshape(batch * seq, aw)
        rec, h_tiles = _rglru(xr, gr, None, None, seq, rec_w, all_h=False)
        xp = _tail(xp, att, rec, p_prompt[l].reshape(batch * seq, -1), tail_w)
        outs[0].append(k.reshape(batch, seq, n_heads, head_dim))
        outs[1].append(v.reshape(batch, seq, n_heads, head_dim))
        outs[2].append(lft.reshape(n_heads, batch, seq).transpose(1, 2, 0))
        outs[3].append(xr.reshape(batch, seq, rw)[:, seq - (CONV_WIDTH - 1):])
        outs[4].append(h_tiles.reshape(batch, -1, V7X_SUBLANES, rw)[:, -1, -1])

        q_s, k_s, v_s, lf_s, cn_s, xr_s, gr_s = _proj_sample(
            xs, t_new, gmix, w_main, wf_pad, bf_row, gq, gk, n_heads, head_dim, rw)
        lf_s = lf_s[:, :n_heads]
        cache_lft = jnp.swapaxes(cache_logf[l], 1, 2)
        att_s = _attn_sample(q_s, k_s, v_s, cn_s[:, :n_heads],
                             cache_k[l].reshape(n_phys, page, aw), cache_v[l].reshape(n_phys, page, aw),
                             cache_lft, page_table, t_new, n_heads, head_dim)
        hist = jnp.concatenate([state_conv[l], jnp.zeros((db, t_new, rw), _F32)], axis=1)
        prevs = tuple(hist[:, CONV_WIDTH - 1 - s:CONV_WIDTH - 1 - s + t_new].reshape(db * t_new, rw)
                      for s in range(1, CONV_WIDTH))
        h0 = jnp.repeat(state_h[l], t_new, axis=0)
        rec_s, h_s = _rglru(xr_s, gr_s, prevs, h0, t_new, rec_w, all_h=True)
        xs = _tail(xs, att_s, rec_s, p_sample[l].reshape(db * t_new, -1), tail_w)
        outs[5].append(k_s.reshape(db, t_new, n_heads, head_dim))
        outs[6].append(v_s.reshape(db, t_new, n_heads, head_dim))
        outs[7].append(lf_s.reshape(db, t_new, n_heads))
        outs[8].append(xr_s.reshape(db, t_new, rw)[:, t_new - (CONV_WIDTH - 1):])
        outs[9].append(h_s.reshape(db, t_new, rw)[:, -1])
    return (xp.reshape(batch, seq, d_model), xs.reshape(db, t_new, d_model),
            *(jnp.stack(o) for o in outs))
```

```python
import functools

import numpy as np
import jax
import jax.numpy as jnp
from jax import lax
from jax.experimental import pallas as pl
from jax.experimental.pallas import tpu as pltpu

RG_C = 8.0
NORM_EPS = 1e-6
LOG2_E = 1.4426950408889634
CONV_WIDTH = 4

V7X_LANES = 128
V7X_SUBLANES = 8
V7X_MXU_DIM = 256
V7X_VMEM_LIMIT_BYTES = 56 * 1024 * 1024

PROJ_ROWS = 512
ATTN_BLOCK = 512
PAGES_PER_STEP = 16
FFN_CHUNK = 512

_F32 = jnp.float32
_BF16 = jnp.bfloat16
_NT = (((1,), (1,)), ((), ()))


def _dot(a, b):
    return jnp.dot(a, b, preferred_element_type=_F32)


def _dot_nt(a, b):
    return lax.dot_general(a, b, _NT, preferred_element_type=_F32)


def _rms(x, g):
    return x * lax.rsqrt(jnp.mean(x * x, axis=-1, keepdims=True) + NORM_EPS) * g


def _split3(z):
    hi = z.astype(_BF16).astype(_F32)
    r = z - hi
    mid = r.astype(_BF16).astype(_F32)
    lo = r - mid
    return hi, mid, lo


def _stack3(z, rows):
    hi, mid, lo = _split3(z)
    pad = jnp.zeros((rows - 3 * V7X_SUBLANES, z.shape[1]), _F32)
    return jnp.concatenate([hi, mid, lo, pad], axis=0).astype(_BF16)


def _sum3(r):
    return r[0:8] + r[8:16] + r[16:24]


def _head_norm(z, g, gmat):
    zz = (z * z).astype(_BF16)
    half = gmat.shape[0]
    ms = jnp.concatenate([_dot(zz[:, :half], gmat), _dot(zz[:, half:], gmat)], axis=1)
    return z * lax.rsqrt(ms + NORM_EPS) * g


def _const_spec(shape):
    nd = len(shape)
    return pl.BlockSpec(shape, lambda *_: (0,) * nd, pipeline_mode=pl.Buffered(1))


def _params(n_axes):
    return pltpu.CompilerParams(
        dimension_semantics=("arbitrary",) * n_axes,
        vmem_limit_bytes=V7X_VMEM_LIMIT_BYTES,
    )


def _proj_kernel(x_ref, gmix_ref, w_ref, wft_ref, bf_ref, gq_ref, gk_ref, gmat_ref, cum_ref,
                 aq_ref, ak_ref, cq_ref, ck_ref,
                 k_out, v_out, lft_out, qs_out, kb_out, vt_out, augq_out, augk_out, xr_out, gr_out,
                 carry_ref, *, tiles_per_seq, aw, rw):
    i = pl.program_id(0)
    tm = x_ref.shape[0]

    @pl.when(i % tiles_per_seq == 0)
    def _():
        carry_ref[...] = jnp.zeros_like(carry_ref)

    hn = _rms(x_ref[...], gmix_ref[...]).astype(_BF16)
    proj = _dot(hn, w_ref[...])
    q = proj[:, 0:aw]
    k = proj[:, aw:2 * aw]
    v = proj[:, 2 * aw:3 * aw]
    xr_out[...] = proj[:, 3 * aw:3 * aw + rw]
    gr_out[...] = proj[:, 3 * aw + rw:3 * aw + 2 * rw]

    gmat = gmat_ref[...]
    qn = _head_norm(q, gq_ref[...], gmat)
    kn = _head_norm(k, gk_ref[...], gmat)
    vt = v.T
    k_out[...] = kn.T
    v_out[...] = vt
    qs_out[...] = qn.astype(_BF16)
    kb_out[...] = kn.astype(_BF16)
    vt_out[...] = vt.astype(_BF16)

    ft = _dot_nt(wft_ref[...], hn)
    lft = jax.nn.log_sigmoid(ft[0:8] + bf_ref[...])
    lft_out[...] = lft

    p3 = _stack3(lft, V7X_LANES)
    blk = cum_ref.shape[0]
    carry = carry_ref[...]
    cums = []
    for c in range(tm // blk):
        r = _dot(p3[:, c * blk:(c + 1) * blk], cum_ref[...])
        cums.append(_sum3(r[:, :blk]) + carry)
        carry = carry + _sum3(r[:, blk:])
    carry_ref[...] = carry
    cum = jnp.concatenate(cums, axis=1)

    c3 = _stack3(cum * LOG2_E, V7X_LANES)
    qat = _dot(aq_ref[...], c3) + cq_ref[...]
    kat = _dot(ak_ref[...], c3) + ck_ref[...]
    augq_out[...] = qat.T.astype(_BF16)
    augk_out[...] = kat.T.astype(_BF16)


def _bias_placement(n_heads):
    aq = np.zeros((V7X_LANES, V7X_LANES), np.float32)
    ak = np.zeros((V7X_LANES, V7X_LANES), np.float32)
    cq = np.zeros((V7X_LANES, 1), np.float32)
    ck = np.zeros((V7X_LANES, 1), np.float32)
    for h in range(n_heads):
        for j in range(3):
            aq[h * 8 + j, j * 8 + h] = 1.0
            cq[h * 8 + 3 + j, 0] = 1.0
            ak[h * 8 + 3 + j, j * 8 + h] = -1.0
            ck[h * 8 + j, 0] = 1.0
    return aq, ak, cq, ck


def _group_mean_matrix(head_dim):
    idx = np.arange(V7X_MXU_DIM) // head_dim
    return (idx[:, None] == idx[None, :]).astype(np.float32) / head_dim


def _prefix_matrix():
    idx = np.arange(V7X_MXU_DIM)
    incl = (idx[:, None] <= idx[None, :]).astype(np.float32)
    return np.concatenate([incl, np.ones_like(incl)], axis=1)


def _proj_prompt(x2d, seq, gmix, w_main, wft, bf_col, gq, gk, n_heads, head_dim, rw):
    n, d = x2d.shape
    tm = PROJ_ROWS
    aw = n_heads * head_dim
    aq, ak, cq, ck = _bias_placement(n_heads)
    consts = [
        gmix, w_main, wft, bf_col, gq, gk,
        jnp.asarray(_group_mean_matrix(head_dim), _BF16),
        jnp.asarray(_prefix_matrix(), _BF16),
        jnp.asarray(aq, _BF16), jnp.asarray(ak, _BF16), jnp.asarray(cq), jnp.asarray(ck),
    ]
    tps = seq // tm
    batch = n // seq
    row = lambda w: pl.BlockSpec((tm, w), lambda i: (i, 0))
    seq_t = lambda r: pl.BlockSpec((None, r, tm), lambda i: (i // tps, 0, i % tps))
    out_shape = (
        jax.ShapeDtypeStruct((batch, aw, seq), _F32),
        jax.ShapeDtypeStruct((batch, aw, seq), _F32),
        jax.ShapeDtypeStruct((batch, V7X_SUBLANES, seq), _F32),
        jax.ShapeDtypeStruct((n, aw), _BF16),
        jax.ShapeDtypeStruct((n, aw), _BF16),
        jax.ShapeDtypeStruct((batch, aw, seq), _BF16),
        jax.ShapeDtypeStruct((n, V7X_LANES), _BF16),
        jax.ShapeDtypeStruct((n, V7X_LANES), _BF16),
        jax.ShapeDtypeStruct((n, rw), _F32),
        jax.ShapeDtypeStruct((n, rw), _F32),
    )
    out_specs = (
        seq_t(aw), seq_t(aw), seq_t(V7X_SUBLANES), row(aw), row(aw), seq_t(aw),
        row(V7X_LANES), row(V7X_LANES), row(rw), row(rw),
    )
    return pl.pallas_call(
        functools.partial(_proj_kernel, tiles_per_seq=tps, aw=aw, rw=rw),
        grid=(n // tm,),
        in_specs=[row(d)] + [_const_spec(c.shape) for c in consts],
        out_specs=out_specs,
        out_shape=out_shape,
        scratch_shapes=[pltpu.VMEM((V7X_SUBLANES, V7X_MXU_DIM), _F32)],
        compiler_params=_params(1),
        name="proj_prompt",
    )(x2d, *consts)


def _fox_kernel(qi_tab, ki_tab, q_ref, aq_ref, k_ref, ak_ref, vt_ref, o_ref, qf_ref, m_ref, acc_ref,
                *, n_heads, head_dim):
    step = pl.program_id(1)
    qi = qi_tab[step]
    ki = ki_tab[step]
    bq = q_ref.shape[0]
    bk = k_ref.shape[0]
    pair = 2 * head_dim

    @pl.when(ki == 0)
    def _init():
        lane = lax.broadcasted_iota(jnp.int32, (bq, V7X_LANES), 1)
        a = aq_ref[...]
        zero = jnp.zeros_like(a)
        for h in range(n_heads):
            q2 = q_ref[:, (h // 2) * pair:(h // 2 + 1) * pair]
            lo = head_dim * (h % 2)
            qh = jnp.where((lane >= lo) & (lane < lo + head_dim), q2, zero)
            ah = jnp.where((lane >= h * 8) & (lane < h * 8 + 8), a, zero)
            qf_ref[h] = jnp.concatenate([qh, ah], axis=1)
        m_ref[...] = jnp.full_like(m_ref, -jnp.inf)
        acc_ref[...] = jnp.zeros_like(acc_ref)

    def step_body(masked):
        ak = ak_ref[...]
        ones_blk = (lax.broadcasted_iota(jnp.int32, (16, bk), 0) == 0).astype(_BF16)
        if masked:
            causal = (lax.broadcasted_iota(jnp.int32, (bk, bq), 0)
                      <= lax.broadcasted_iota(jnp.int32, (bk, bq), 1))
        for h in range(n_heads):
            hp = h // 2
            kf = jnp.concatenate([k_ref[:, hp * pair:(hp + 1) * pair], ak], axis=1)
            s = _dot_nt(kf, qf_ref[h])
            if masked:
                s = jnp.where(causal, s, -jnp.inf)
            m_old = m_ref[h]
            m_new = jnp.maximum(m_old, jnp.max(s, axis=0, keepdims=True))
            alpha = jnp.exp2(m_old - m_new)
            p = jnp.exp2(s - m_new).astype(_BF16)
            lhs = jnp.concatenate([vt_ref[h * head_dim:(h + 1) * head_dim, :], ones_blk], axis=0)
            acc_ref[h] = alpha * acc_ref[h] + _dot(lhs, p)
            m_ref[h] = m_new

    @pl.when(ki < qi)
    def _off_diagonal():
        step_body(False)

    @pl.when(ki == qi)
    def _diagonal():
        step_body(True)
        for hp in range(n_heads // 2):
            outs = []
            for h in (2 * hp, 2 * hp + 1):
                a = acc_ref[h]
                outs.append(a[0:head_dim] / a[head_dim:head_dim + 1])
            o_ref[:, hp * pair:(hp + 1) * pair] = jnp.concatenate(outs, axis=0).T


def _fox_prompt(qs, augq, kb, augk, vt, batch, seq, n_heads, head_dim):
    aw = n_heads * head_dim
    bq = bk = ATTN_BLOCK
    nq = seq // bq
    qs3 = qs.reshape(batch, seq, aw)
    kb3 = kb.reshape(batch, seq, aw)
    aq3 = augq.reshape(batch, seq, V7X_LANES)
    ak3 = augk.reshape(batch, seq, V7X_LANES)
    pairs = [(qi, ki) for qi in range(nq) for ki in range(qi + 1)]
    qi_tab = jnp.asarray([p[0] for p in pairs], jnp.int32)
    ki_tab = jnp.asarray([p[1] for p in pairs], jnp.int32)
    qmap = lambda b, s, qt, kt: (b, qt[s], 0)
    kmap = lambda b, s, qt, kt: (b, kt[s], 0)
    grid_spec = pltpu.PrefetchScalarGridSpec(
        num_scalar_prefetch=2,
        grid=(batch, len(pairs)),
        in_specs=[
            pl.BlockSpec((None, bq, aw), qmap),
            pl.BlockSpec((None, bq, V7X_LANES), qmap),
            pl.BlockSpec((None, bk, aw), kmap),
            pl.BlockSpec((None, bk, V7X_LANES), kmap),
            pl.BlockSpec((None, aw, bk), lambda b, s, qt, kt: (b, 0, kt[s])),
        ],
        out_specs=pl.BlockSpec((None, bq, aw), qmap),
        scratch_shapes=[
            pltpu.VMEM((n_heads, bq, 2 * V7X_LANES), _BF16),
            pltpu.VMEM((n_heads, 1, bq), _F32),
            pltpu.VMEM((n_heads, head_dim + 16, bq), _F32),
        ],
    )
    return pl.pallas_call(
        functools.partial(_fox_kernel, n_heads=n_heads, head_dim=head_dim),
        grid_spec=grid_spec,
        out_shape=jax.ShapeDtypeStruct((batch, seq, aw), _F32),
        compiler_params=_params(2),
        name="fox_prompt",
    )(qi_tab, ki_tab, qs3, aq3, kb3, ak3, vt)


def _rglru_kernel(xr_ref, gr_ref, p1_ref, p2_ref, p3_ref, h0_ref, wconv_ref, bconv_ref,
                  wa_ref, ba_ref, wx_ref, bx_ref, lam_ref, y_ref, h_ref, xlast_ref, hc_ref,
                  *, seg_len, tiles_per_seq):
    i = pl.program_id(0)
    tm, rw = xr_ref.shape
    sub = V7X_SUBLANES

    @pl.when(i % tiles_per_seq == 0)
    def _():
        xlast_ref[...] = jnp.zeros_like(xlast_ref)
        hc_ref[...] = jnp.zeros_like(hc_ref)

    xr = xr_ref[...]
    row = lax.broadcasted_iota(jnp.int32, (tm, rw), 0)
    t = ((i % tiles_per_seq) * tm + row) & (seg_len - 1)
    row8 = lax.broadcasted_iota(jnp.int32, (sub, rw), 0)
    xl = xlast_ref[...]
    prev = (p1_ref, p2_ref, p3_ref)
    shifted = []
    for k in range(1, CONV_WIDTH):
        r = pltpu.roll(xr, k, 0)
        head = jnp.where(row8 < k, pltpu.roll(xl, k, 0), r[0:sub])
        r = jnp.concatenate([head, r[sub:]], axis=0)
        shifted.append(jnp.where(t >= k, r, prev[k - 1][...]))
    xlast_ref[...] = xr[tm - sub:tm]

    w = wconv_ref[...]
    xc = bconv_ref[...] + shifted[2] * w[0:1]
    xc = xc + shifted[1] * w[1:2]
    xc = xc + shifted[0] * w[2:3]
    xc = xc + xr * w[3:4]

    xcb = xc.astype(_BF16)
    half = wa_ref.shape[1]

    def gate(w_ref, b_ref):
        z = jnp.concatenate([_dot(xcb[:, :half], w_ref[0]), _dot(xcb[:, half:], w_ref[1])], axis=1)
        return jax.nn.sigmoid(z + b_ref[...])

    r_gate = gate(wa_ref, ba_ref)
    i_gate = gate(wx_ref, bx_ref)
    log_a = -RG_C * r_gate * jax.nn.softplus(-lam_ref[...])
    a = jnp.exp(log_a)
    b = jnp.sqrt(-jnp.tanh(log_a) * (a * a + 1.0)) * (i_gate * xc)

    d = 1
    while d < min(tm, seg_len):
        valid = (row >= d) & (t >= d)
        a_s = pltpu.roll(a, d, 0)
        b_s = pltpu.roll(b, d, 0)
        b = jnp.where(valid, b + a * b_s, b)
        a = jnp.where(valid, a * a_s, a)
        d *= 2
    h_in = jnp.where(t <= row, h0_ref[...], hc_ref[...])
    h = a * h_in + b
    hc_ref[...] = h[tm - 1:tm]
    y_ref[...] = h * jax.nn.gelu(gr_ref[...])
    h_ref[...] = h[tm - h_ref.shape[0]:tm]


def _rglru(xr, gr, prevs, h0, seg_len, rec_w, all_h):
    n, rw = xr.shape
    tm = min(PROJ_ROWS, n)
    assert seg_len & (seg_len - 1) == 0, "sequence length must be a power of two"
    tiles_per_seq = max(seg_len // tm, 1)
    row = pl.BlockSpec((tm, rw), lambda i: (i, 0))
    if prevs is None:
        zeros = jnp.zeros((tm, rw), _F32)
        prevs, h0 = (zeros, zeros, zeros), zeros
        state_spec = _const_spec((tm, rw))
    else:
        state_spec = row
    h_rows = tm if all_h else V7X_SUBLANES
    return pl.pallas_call(
        functools.partial(_rglru_kernel, seg_len=seg_len, tiles_per_seq=tiles_per_seq),
        grid=(n // tm,),
        in_specs=[row, row] + [state_spec] * 4 + [_const_spec(c.shape) for c in rec_w],
        out_specs=(row, pl.BlockSpec((h_rows, rw), lambda i: (i, 0))),
        out_shape=(jax.ShapeDtypeStruct((n, rw), _F32),
                   jax.ShapeDtypeStruct((n // tm * h_rows, rw), _F32)),
        scratch_shapes=[pltpu.VMEM((V7X_SUBLANES, rw), _F32), pltpu.VMEM((1, rw), _F32)],
        compiler_params=_params(1),
        name="rglru",
    )(xr, gr, *prevs, h0, *rec_w)


def _tail_kernel(x_ref, att_ref, rec_ref, p_ref, goa_ref, gor_ref, wout_ref, gffn_ref, wg_ref, wu_ref,
                 wo_ref, wple_ref, gple_ref, wpg_ref, o_ref):
    aw = att_ref.shape[1]
    an = _rms(att_ref[...], goa_ref[...]).astype(_BF16)
    rn = _rms(rec_ref[...], gor_ref[...]).astype(_BF16)
    x1 = x_ref[...] + _dot(an, wout_ref[0:aw, :]) + _dot(rn, wout_ref[aw:, :])
    hn = _rms(x1, gffn_ref[...]).astype(_BF16)
    hidden = wg_ref.shape[1]
    ffn = None
    for c0 in range(0, hidden, FFN_CHUNK):
        cw = min(FFN_CHUNK, hidden - c0)
        g = _dot(hn, wg_ref[:, c0:c0 + cw])
        u = _dot(hn, wu_ref[:, c0:c0 + cw])
        part = _dot((jax.nn.silu(g) * u).astype(_BF16), wo_ref[c0:c0 + cw, :])
        ffn = part if ffn is None else ffn + part
    x2 = x1 + ffn
    gate = jax.nn.sigmoid(_dot(_rms(x2, gple_ref[...]).astype(_BF16), wpg_ref[...]))
    o_ref[...] = x2 + _dot(p_ref[...].astype(_BF16), wple_ref[...]) * gate


def _tail(x2d, att, rec, p2d, tail_w):
    n, d = x2d.shape
    tm = min(PROJ_ROWS, n)
    row = lambda w: pl.BlockSpec((tm, w), lambda i: (i, 0))
    return pl.pallas_call(
        _tail_kernel,
        grid=(n // tm,),
        in_specs=[row(d), row(att.shape[1]), row(rec.shape[1]), row(p2d.shape[1])]
        + [_const_spec(c.shape) for c in tail_w],
        out_specs=row(d),
        out_shape=jax.ShapeDtypeStruct((n, d), _F32),
        compiler_params=_params(1),
        name="tail",
    )(x2d, att, rec, p2d, *tail_w)


def _sproj_kernel(x_ref, gmix_ref, w_ref, wf_ref, bf_ref, gq_ref, gk_ref, gmat_ref, seg_ref,
                  q_out, k_out, v_out, lf_out, cn_out, xr_out, gr_out, *, aw, rw):
    hn = _rms(x_ref[...], gmix_ref[...]).astype(_BF16)
    proj = _dot(hn, w_ref[...])
    gmat = gmat_ref[...]
    q_out[...] = _head_norm(proj[:, 0:aw], gq_ref[...], gmat)
    k_out[...] = _head_norm(proj[:, aw:2 * aw], gk_ref[...], gmat)
    v_out[...] = proj[:, 2 * aw:3 * aw]
    xr_out[...] = proj[:, 3 * aw:3 * aw + rw]
    gr_out[...] = proj[:, 3 * aw + rw:3 * aw + 2 * rw]
    lf = jax.nn.log_sigmoid(_dot(hn, wf_ref[...]) + bf_ref[...])
    lf_out[...] = lf
    hi, mid, lo = _split3(lf)
    seg = seg_ref[...]
    cn_out[...] = _dot(seg, hi.astype(_BF16)) + _dot(seg, mid.astype(_BF16)) + _dot(seg, lo.astype(_BF16))


def _proj_sample(x2d, t_new, gmix, w_main, wf_pad, bf_row, gq, gk, n_heads, head_dim, rw):
    n, d = x2d.shape
    aw = n_heads * head_dim
    idx = np.arange(n)
    seg = ((idx[:, None] // t_new == idx[None, :] // t_new) & (idx[None, :] <= idx[:, None])).astype(np.float32)
    ins = [x2d, gmix, w_main, wf_pad, bf_row, gq, gk,
           jnp.asarray(_group_mean_matrix(head_dim), _BF16), jnp.asarray(seg, _BF16)]
    full = lambda s: pl.BlockSpec(s, lambda i: (0,) * len(s))
    widths = (aw, aw, aw, V7X_LANES, V7X_LANES, rw, rw)
    return pl.pallas_call(
        functools.partial(_sproj_kernel, aw=aw, rw=rw),
        grid=(1,),
        in_specs=[full(a.shape) for a in ins],
        out_specs=tuple(full((n, w)) for w in widths),
        out_shape=tuple(jax.ShapeDtypeStruct((n, w), _F32) for w in widths),
        compiler_params=_params(1),
        name="proj_sample",
    )(*ins)


def _sattn_kernel(pt_ref, q_ref, kn_ref, vn_ref, cnq_ref, cnk_ref, *refs, pps, t_new, head_dim):
    del pt_ref
    k_refs = refs[:pps]
    v_refs = refs[pps:2 * pps]
    lf_refs = refs[2 * pps:3 * pps]
    uo_ref, o_ref, qbd_ref, m_ref, l_ref, acc_ref, carry_ref = refs[3 * pps:]
    j = pl.program_id(1)
    nh, aw = V7X_SUBLANES, q_ref.shape[1]
    rows = t_new * nh
    page = k_refs[0].shape[1]
    feat = lax.broadcasted_iota(jnp.int32, (nh, aw), 1)
    head = lax.broadcasted_iota(jnp.int32, (nh, aw), 0)
    head_mask = (feat >= head * head_dim) & (feat < (head + 1) * head_dim)

    @pl.when(j == 0)
    def _init():
        q = q_ref[...]
        qbd = jnp.concatenate(
            [jnp.where(head_mask, jnp.broadcast_to(q[t:t + 1], (nh, aw)), 0.0) for t in range(t_new)],
            axis=0).astype(_BF16)
        qbd_ref[...] = qbd
        pad = jnp.zeros((page - kn_ref.shape[0], aw), _F32)
        kn = jnp.concatenate([kn_ref[...], pad], axis=0).astype(_BF16)
        vn = jnp.concatenate([vn_ref[...], pad], axis=0).astype(_BF16)
        s = _dot_nt(qbd, kn) + cnq_ref[...] - cnk_ref[...]
        row = lax.broadcasted_iota(jnp.int32, (rows, page), 0)
        t_col = lax.broadcasted_iota(jnp.int32, (rows, page), 1)
        s = jnp.where(t_col * nh <= row, s, -jnp.inf)
        m = jnp.max(s, axis=1, keepdims=True)
        p = jnp.exp(s - m)
        m_ref[...] = m
        l_ref[...] = jnp.sum(p, axis=1, keepdims=True)
        acc_ref[...] = _dot(p.astype(_BF16), vn)
        carry_ref[...] = jnp.zeros_like(carry_ref)

    qbd = qbd_ref[...]
    carry = carry_ref[...]
    scores = []
    for i in range(pps):
        s = _dot(qbd, k_refs[i][...].astype(_BF16))
        r = _dot(_stack3(lf_refs[i][...], 4 * V7X_SUBLANES), uo_ref[...])
        bias = _sum3(r[:, :page]) + carry
        carry = carry + _sum3(r[:, page:])
        scores.append(s + jnp.concatenate([bias] * t_new, axis=0))
    carry_ref[...] = carry
    s_all = jnp.concatenate(scores, axis=1) + cnq_ref[...]
    m_old = m_ref[...]
    m_new = jnp.maximum(m_old, jnp.max(s_all, axis=1, keepdims=True))
    alpha = jnp.exp(m_old - m_new)
    p = jnp.exp(s_all - m_new)
    l_ref[...] = alpha * l_ref[...] + jnp.sum(p, axis=1, keepdims=True)
    vt_all = jnp.concatenate([v_refs[i][...].astype(_BF16) for i in range(pps)], axis=1)
    acc_ref[...] = alpha * acc_ref[...] + _dot_nt(p.astype(_BF16), vt_all)
    m_ref[...] = m_new

    @pl.when(j == pl.num_programs(1) - 1)
    def _fin():
        a = acc_ref[...] / l_ref[...]
        for t in range(t_new):
            o_ref[t:t + 1, :] = jnp.sum(jnp.where(head_mask, a[t * nh:(t + 1) * nh], 0.0), axis=0, keepdims=True)


def _suffix_matrix(page):
    idx = np.arange(page)
    strict = (idx[:, None] > idx[None, :]).astype(np.float32)
    return np.concatenate([strict, np.ones_like(strict)], axis=1)


def _attn_sample(q, k_new, v_new, cn, cache_kt, cache_vt, cache_lft, page_table, t_new, n_heads, head_dim):
    db, n_pages = page_table.shape
    page = cache_kt.shape[2]
    aw = n_heads * head_dim
    pps = PAGES_PER_STEP
    rows = t_new * n_heads
    pad_t = V7X_SUBLANES - t_new
    q3 = q.reshape(db, t_new, aw)
    kn = jnp.pad(k_new.reshape(db, t_new, aw), ((0, 0), (0, pad_t), (0, 0)))
    vn = jnp.pad(v_new.reshape(db, t_new, aw), ((0, 0), (0, pad_t), (0, 0)))
    cn3 = cn.reshape(db, t_new, n_heads)
    cnq = cn3.reshape(db, rows, 1)
    cnk = jnp.tile(jnp.swapaxes(cn3, 1, 2), (1, t_new, 1))
    cnk = jnp.pad(cnk, ((0, 0), (0, 0), (0, page - t_new)))

    def page_spec(width_shape, i):
        return pl.BlockSpec((None,) + width_shape,
                            lambda b, j, pt: (pt[b, n_pages - 1 - (j * pps + i)], 0, 0))

    per_b = lambda s: pl.BlockSpec((None,) + s, lambda b, j, pt: (b, 0, 0))
    uo = jnp.asarray(_suffix_matrix(page), _BF16)
    grid_spec = pltpu.PrefetchScalarGridSpec(
        num_scalar_prefetch=1,
        grid=(db, n_pages // pps),
        in_specs=[per_b((t_new, aw)), per_b((V7X_SUBLANES, aw)), per_b((V7X_SUBLANES, aw)),
                  per_b((rows, 1)), per_b((rows, page))]
        + [page_spec((aw, page), i) for i in range(pps)]
        + [page_spec((aw, page), i) for i in range(pps)]
        + [page_spec((V7X_SUBLANES, page), i) for i in range(pps)]
        + [pl.BlockSpec(uo.shape, lambda b, j, pt: (0, 0))],
        out_specs=per_b((t_new, aw)),
        scratch_shapes=[
            pltpu.VMEM((rows, aw), _BF16),
            pltpu.VMEM((rows, 1), _F32),
            pltpu.VMEM((rows, 1), _F32),
            pltpu.VMEM((rows, aw), _F32),
            pltpu.VMEM((V7X_SUBLANES, page), _F32),
        ],
    )
    out = pl.pallas_call(
        functools.partial(_sattn_kernel, pps=pps, t_new=t_new, head_dim=head_dim),
        grid_spec=grid_spec,
        out_shape=jax.ShapeDtypeStruct((db, t_new, aw), _F32),
        compiler_params=_params(2),
        name="attn_sample",
    )(page_table, q3, kn, vn, cnq, cnk, *([cache_kt] * pps), *([cache_vt] * pps), *([cache_lft] * pps), uo)
    return out.reshape(db * t_new, aw)


def _block_diag_pair(w):
    nb, dd, _ = w.shape
    per = V7X_MXU_DIM // dd
    tiles = []
    for half in range(nb // per):
        tile = jnp.zeros((V7X_MXU_DIM, V7X_MXU_DIM), w.dtype)
        for j in range(per):
            tile = lax.dynamic_update_slice(tile, w[half * per + j], (j * dd, j * dd))
        tiles.append(tile)
    return jnp.stack(tiles).astype(_BF16)


def kernel(x_prompt, x_sample, p_prompt, p_sample, cache_k, cache_v, cache_logf, state_conv, state_h, page_table, g_mix, w_in, b_f, g_q, g_k, w_conv, b_conv, w_a, b_a, w_x, b_x, lam, g_out_attn, g_out_rec, w_out, g_ffn, w_ffn_in, w_ffn_out, w_ple, g_ple, w_ple_gate):
    batch, seq, d_model = x_prompt.shape
    db, t_new, _ = x_sample.shape
    depth, n_heads, head_dim = g_q.shape
    aw = n_heads * head_dim
    rw = lam.shape[1]
    hidden = w_ffn_out.shape[1]
    n_phys, page = cache_k.shape[1], cache_k.shape[2]
    assert n_heads == V7X_SUBLANES and aw == 2 * V7X_MXU_DIM and rw == 2 * V7X_MXU_DIM
    assert seq % ATTN_BLOCK == 0 and seq % PROJ_ROWS == 0 and page_table.shape[1] % PAGES_PER_STEP == 0

    xp = x_prompt.reshape(batch * seq, d_model)
    xs = x_sample.reshape(db * t_new, d_model)
    outs = [[] for _ in range(10)]
    row = lambda a: a.reshape(1, -1)
    for l in range(depth):
        w_l = w_in[l]
        w_main = jnp.concatenate([w_l[:, :3 * aw], w_l[:, 3 * aw + n_heads:]], axis=1).astype(_BF16)
        w_f = w_l[:, 3 * aw:3 * aw + n_heads]
        wft = jnp.pad(w_f.T, ((0, 16 - n_heads), (0, 0))).astype(_BF16)
        wf_pad = jnp.pad(w_f, ((0, 0), (0, V7X_LANES - n_heads))).astype(_BF16)
        bf_col = b_f[l].reshape(n_heads, 1)
        bf_row = jnp.pad(b_f[l], (0, V7X_LANES - n_heads)).reshape(1, V7X_LANES)
        gmix = row(g_mix[l])
        gq = row(g_q[l]) * (head_dim ** -0.5)
        gk = row(g_k[l])
        rec_w = (w_conv[l], row(b_conv[l]), _block_diag_pair(w_a[l]), row(b_a[l]),
                 _block_diag_pair(w_x[l]), row(b_x[l]), row(lam[l]))
        tail_w = (row(g_out_attn[l]), row(g_out_rec[l]), w_out[l].astype(_BF16), row(g_ffn[l]),
                  w_ffn_in[l][:, :hidden].astype(_BF16), w_ffn_in[l][:, hidden:].astype(_BF16),
                  w_ffn_out[l].astype(_BF16), w_ple[l].astype(_BF16), row(g_ple[l]),
                  w_ple_gate[l].astype(_BF16))

        kt, vt32, lft, qs, kb, vt, augq, augk, xr, gr = _proj_prompt(
            xp, seq, gmix, w_main, wft, bf_col, gq * LOG2_E, gk, n_heads, head_dim, rw)
        att = _fox_prompt(qs, augq, kb, augk, vt, batch, seq, n_heads, head_dim).reshape(batch * seq, aw)
        rec, h_tiles = _rglru(xr, gr, None, None, seq, rec_w, all_h=False)
        xp = _tail(xp, att, rec, p_prompt[l].reshape(batch * seq, -1), tail_w)
        outs[0].append(kt.reshape(batch, n_heads, head_dim, seq).transpose(0, 3, 1, 2))
        outs[1].append(vt32.reshape(batch, n_heads, head_dim, seq).transpose(0, 3, 1, 2))
        outs[2].append(lft.transpose(0, 2, 1))
        outs[3].append(xr.reshape(batch, seq, rw)[:, seq - (CONV_WIDTH - 1):])
        outs[4].append(h_tiles.reshape(batch, -1, V7X_SUBLANES, rw)[:, -1, -1])

        q_s, k_s, v_s, lf_s, cn_s, xr_s, gr_s = _proj_sample(
            xs, t_new, gmix, w_main, wf_pad, bf_row, gq, gk, n_heads, head_dim, rw)
        lf_s = lf_s[:, :n_heads]
        cache_lft = jnp.swapaxes(cache_logf[l], 1, 2)
        cache_kt = cache_k[l].transpose(0, 2, 3, 1).reshape(n_phys, aw, page)
        cache_vt = cache_v[l].transpose(0, 2, 3, 1).reshape(n_phys, aw, page)
        att_s = _attn_sample(q_s, k_s, v_s, cn_s[:, :n_heads], cache_kt, cache_vt, cache_lft,
                             page_table, t_new, n_heads, head_dim)
        hist = jnp.concatenate([state_conv[l], jnp.zeros((db, t_new, rw), _F32)], axis=1)
        prevs = tuple(hist[:, CONV_WIDTH - 1 - s:CONV_WIDTH - 1 - s + t_new].reshape(db * t_new, rw)
                      for s in range(1, CONV_WIDTH))
        h0 = jnp.repeat(state_h[l], t_new, axis=0)
        rec_s, h_s = _rglru(xr_s, gr_s, prevs, h0, t_new, rec_w, all_h=True)
        xs = _tail(xs, att_s, rec_s, p_sample[l].reshape(db * t_new, -1), tail_w)
        outs[5].append(k_s.reshape(db, t_new, n_heads, head_dim))
        outs[6].append(v_s.reshape(db, t_new, n_heads, head_dim))
        outs[7].append(lf_s.reshape(db, t_new, n_heads))
        outs[8].append(xr_s.reshape(db, t_new, rw)[:, t_new - (CONV_WIDTH - 1):])
        outs[9].append(h_s.reshape(db, t_new, rw)[:, -1])
    return (xp.reshape(batch, seq, d_model), xs.reshape(db, t_new, d_model),
            *(jnp.stack(o) for o in outs))
```

```python
import functools

import numpy as np
import jax
import jax.numpy as jnp
from jax import lax
from jax.experimental import pallas as pl
from jax.experimental.pallas import tpu as pltpu

RG_C = 8.0
NORM_EPS = 1e-6
LOG2_E = 1.4426950408889634
CONV_WIDTH = 4

V7X_LANES = 128
V7X_SUBLANES = 8
V7X_MXU_DIM = 256
V7X_VMEM_LIMIT_BYTES = 56 * 1024 * 1024

PROJ_ROWS = 512
ATTN_BLOCK = 512
FOX_LOOKAHEAD = 2
PAGES_PER_STEP = 16
FFN_CHUNK = 512

_F32 = jnp.float32
_BF16 = jnp.bfloat16
_NT = (((1,), (1,)), ((), ()))


def _dot(a, b):
    return jnp.dot(a, b, preferred_element_type=_F32)


def _dot_nt(a, b):
    return lax.dot_general(a, b, _NT, preferred_element_type=_F32)


def _rms(x, g):
    return x * lax.rsqrt(jnp.mean(x * x, axis=-1, keepdims=True) + NORM_EPS) * g


def _split3(z):
    hi = z.astype(_BF16).astype(_F32)
    r = z - hi
    mid = r.astype(_BF16).astype(_F32)
    lo = r - mid
    return hi, mid, lo


def _stack3(z, rows):
    hi, mid, lo = _split3(z)
    pad = jnp.zeros((rows - 3 * V7X_SUBLANES, z.shape[1]), _F32)
    return jnp.concatenate([hi, mid, lo, pad], axis=0).astype(_BF16)


def _sum3(r):
    return r[0:8] + r[8:16] + r[16:24]


def _head_norm(z, g, gmat):
    zz = (z * z).astype(_BF16)
    half = gmat.shape[0]
    ms = jnp.concatenate([_dot(zz[:, :half], gmat), _dot(zz[:, half:], gmat)], axis=1)
    return z * lax.rsqrt(ms + NORM_EPS) * g


def _const_spec(shape):
    nd = len(shape)
    return pl.BlockSpec(shape, lambda *_: (0,) * nd, pipeline_mode=pl.Buffered(1))


def _params(n_axes):
    return pltpu.CompilerParams(
        dimension_semantics=("arbitrary",) * n_axes,
        vmem_limit_bytes=V7X_VMEM_LIMIT_BYTES,
    )


def _sigmoid(z):
    return 0.5 * jnp.tanh(0.5 * z) + 0.5


def _rglru_coeffs(xr, shifted, rec_refs):
    wconv_ref, bconv_ref, wa_ref, ba_ref, wx_ref, bx_ref, lam_ref = rec_refs
    w = wconv_ref[...]
    xc = bconv_ref[...] + shifted[2] * w[0:1]
    xc = xc + shifted[1] * w[1:2]
    xc = xc + shifted[0] * w[2:3]
    xc = xc + xr * w[3:4]

    xcb = xc.astype(_BF16)
    half = wa_ref.shape[1]

    def gate(w_ref, b_ref):
        z = jnp.concatenate([_dot(xcb[:, :half], w_ref[0]), _dot(xcb[:, half:], w_ref[1])], axis=1)
        return _sigmoid(z + b_ref[...])

    r_gate = gate(wa_ref, ba_ref)
    i_gate = gate(wx_ref, bx_ref)
    log_a = -RG_C * r_gate * jax.nn.softplus(-lam_ref[...])
    a = jnp.exp(log_a)
    om = -jnp.tanh(log_a) * (a * a + 1.0)
    root = jnp.where(om > 0.0, om * lax.rsqrt(om), 0.0)
    return a, root * (i_gate * xc)


def _scan_step(a, b, a_s, b_s):
    return a * a_s, b + a * b_s


def _rglru_prompt_tile(xr, gr, hc, rec_refs, xbuf_ref, abuf_ref, bbuf_ref):
    tm, rw = xr.shape
    sub = V7X_SUBLANES
    xbuf_ref[sub:sub + tm] = xr
    shifted = [xbuf_ref[sub - k:sub - k + tm] for k in range(1, CONV_WIDTH)]
    a, b = _rglru_coeffs(xr, shifted, rec_refs)
    abuf_ref[0:sub] = jnp.ones((sub, rw), _F32)
    bbuf_ref[0:sub] = jnp.zeros((sub, rw), _F32)
    d = 1
    while d < tm:
        if d % sub == 0:
            a_new, b_new = _scan_step(a[d:], b[d:], a[:tm - d], b[:tm - d])
            a = jnp.concatenate([a[:d], a_new], axis=0)
            b = jnp.concatenate([b[:d], b_new], axis=0)
        else:
            abuf_ref[sub:sub + tm] = a
            bbuf_ref[sub:sub + tm] = b
            a, b = _scan_step(a, b, abuf_ref[sub - d:sub - d + tm], bbuf_ref[sub - d:sub - d + tm])
        d *= 2
    h = a * hc + b
    return h * jax.nn.gelu(gr), h


def _proj_kernel(x_ref, gmix_ref, w_ref, wft_ref, bf_ref, gq_ref, gk_ref, gmat_ref, cum_ref,
                 aq_ref, ak_ref, cq_ref, ck_ref, *refs, tiles_per_seq, aw, rw):
    rec_refs = refs[:7]
    (k_out, v_out, lft_out, qs_out, kb_out, vt_out, augq_out, augk_out, rec_out, h_out, xtail_out,
     carry_ref, hc_ref, xbuf_ref, abuf_ref, bbuf_ref) = refs[7:]
    i = pl.program_id(0)
    tm = x_ref.shape[0]
    sub = V7X_SUBLANES
    seq_start = i % tiles_per_seq == 0

    @pl.when(seq_start)
    def _():
        carry_ref[...] = jnp.zeros_like(carry_ref)
        hc_ref[...] = jnp.zeros_like(hc_ref)
        xbuf_ref[0:sub] = jnp.zeros((sub, rw), _F32)

    @pl.when(jnp.logical_not(seq_start))
    def _():
        xbuf_ref[0:sub] = xbuf_ref[tm:tm + sub]

    hn = _rms(x_ref[...], gmix_ref[...]).astype(_BF16)
    w_rec = 3 * aw
    proj_rec = _dot(hn, w_ref[:, w_rec:w_rec + 2 * rw])
    xr = proj_rec[:, 0:rw]
    y, h = _rglru_prompt_tile(xr, proj_rec[:, rw:2 * rw], hc_ref[...], rec_refs,
                              xbuf_ref, abuf_ref, bbuf_ref)
    hc_ref[...] = h[tm - 1:tm]
    rec_out[...] = y
    h_out[...] = h[tm - sub:tm]
    xtail_out[...] = xr[tm - sub:tm]

    proj = _dot(hn, w_ref[:, 0:w_rec])
    q = proj[:, 0:aw]
    k = proj[:, aw:2 * aw]
    v = proj[:, 2 * aw:3 * aw]

    gmat = gmat_ref[...]
    qn = _head_norm(q, gq_ref[...], gmat)
    kn = _head_norm(k, gk_ref[...], gmat)
    vt = v.T
    k_out[...] = kn.T
    v_out[...] = vt
    qs_out[...] = qn.astype(_BF16)
    kb_out[...] = kn.astype(_BF16)
    vt_out[...] = vt.astype(_BF16)

    ft = _dot_nt(wft_ref[...], hn)
    lft = jax.nn.log_sigmoid(ft[0:8] + bf_ref[...])
    lft_out[...] = lft

    p3 = _stack3(lft, V7X_LANES)
    blk = cum_ref.shape[0]
    carry = carry_ref[...]
    cums = []
    for c in range(tm // blk):
        r = _dot(p3[:, c * blk:(c + 1) * blk], cum_ref[...])
        cums.append(_sum3(r[:, :blk]) + carry)
        carry = carry + _sum3(r[:, blk:])
    carry_ref[...] = carry
    cum = jnp.concatenate(cums, axis=1)

    c3 = _stack3(cum * LOG2_E, V7X_LANES)
    qat = _dot(aq_ref[...], c3) + cq_ref[...]
    kat = _dot(ak_ref[...], c3) + ck_ref[...]
    augq_out[...] = qat.T.astype(_BF16)
    augk_out[...] = kat.T.astype(_BF16)


def _bias_placement(n_heads):
    aq = np.zeros((V7X_LANES, V7X_LANES), np.float32)
    ak = np.zeros((V7X_LANES, V7X_LANES), np.float32)
    cq = np.zeros((V7X_LANES, 1), np.float32)
    ck = np.zeros((V7X_LANES, 1), np.float32)
    for h in range(n_heads):
        for j in range(3):
            aq[h * 8 + j, j * 8 + h] = 1.0
            cq[h * 8 + 3 + j, 0] = 1.0
            ak[h * 8 + 3 + j, j * 8 + h] = -1.0
            ck[h * 8 + j, 0] = 1.0
    return aq, ak, cq, ck


def _group_mean_matrix(head_dim):
    idx = np.arange(V7X_MXU_DIM) // head_dim
    return (idx[:, None] == idx[None, :]).astype(np.float32) / head_dim


def _prefix_matrix():
    idx = np.arange(V7X_MXU_DIM)
    incl = (idx[:, None] <= idx[None, :]).astype(np.float32)
    return np.concatenate([incl, np.ones_like(incl)], axis=1)


def _proj_prompt(x2d, seq, gmix, w_main, wft, bf_col, gq, gk, rec_w, n_heads, head_dim, rw):
    n, d = x2d.shape
    tm = PROJ_ROWS
    aw = n_heads * head_dim
    assert seq & (seq - 1) == 0, "sequence length must be a power of two"
    aq, ak, cq, ck = _bias_placement(n_heads)
    consts = [
        gmix, w_main, wft, bf_col, gq, gk,
        jnp.asarray(_group_mean_matrix(head_dim), _BF16),
        jnp.asarray(_prefix_matrix(), _BF16),
        jnp.asarray(aq, _BF16), jnp.asarray(ak, _BF16), jnp.asarray(cq), jnp.asarray(ck),
        *rec_w,
    ]
    tps = seq // tm
    batch = n // seq
    row = lambda w: pl.BlockSpec((tm, w), lambda i: (i, 0))
    seq_t = lambda r: pl.BlockSpec((None, r, tm), lambda i: (i // tps, 0, i % tps))
    out_shape = (
        jax.ShapeDtypeStruct((batch, aw, seq), _F32),
        jax.ShapeDtypeStruct((batch, aw, seq), _F32),
        jax.ShapeDtypeStruct((batch, V7X_SUBLANES, seq), _F32),
        jax.ShapeDtypeStruct((n, aw), _BF16),
        jax.ShapeDtypeStruct((n, aw), _BF16),
        jax.ShapeDtypeStruct((batch, aw, seq), _BF16),
        jax.ShapeDtypeStruct((n, V7X_LANES), _BF16),
        jax.ShapeDtypeStruct((n, V7X_LANES), _BF16),
        jax.ShapeDtypeStruct((n, rw), _F32),
        jax.ShapeDtypeStruct((n // tm * V7X_SUBLANES, rw), _F32),
        jax.ShapeDtypeStruct((n // tm * V7X_SUBLANES, rw), _F32),
    )
    tail8 = pl.BlockSpec((V7X_SUBLANES, rw), lambda i: (i, 0))
    out_specs = (
        seq_t(aw), seq_t(aw), seq_t(V7X_SUBLANES), row(aw), row(aw), seq_t(aw),
        row(V7X_LANES), row(V7X_LANES), row(rw), tail8, tail8,
    )
    return pl.pallas_call(
        functools.partial(_proj_kernel, tiles_per_seq=tps, aw=aw, rw=rw),
        grid=(n // tm,),
        in_specs=[row(d)] + [_const_spec(c.shape) for c in consts],
        out_specs=out_specs,
        out_shape=out_shape,
        scratch_shapes=[pltpu.VMEM((V7X_SUBLANES, V7X_MXU_DIM), _F32), pltpu.VMEM((1, rw), _F32)]
        + [pltpu.VMEM((V7X_SUBLANES + tm, rw), _F32)] * 3,
        compiler_params=_params(1),
        name="proj_prompt",
    )(x2d, *consts)


def _fox_kernel(qi_tab, ki_tab, q_ref, aq_ref, k_ref, ak_ref, vt_ref, o_ref, qf_ref, m_ref, acc_ref,
                *, n_heads, head_dim):
    step = pl.program_id(1)
    qi = qi_tab[step]
    ki = ki_tab[step]
    bq = q_ref.shape[0]
    bk = k_ref.shape[0]
    pair = 2 * head_dim

    @pl.when(ki == 0)
    def _init():
        lane = lax.broadcasted_iota(jnp.int32, (bq, V7X_LANES), 1)
        a = aq_ref[...]
        zero = jnp.zeros_like(a)
        for h in range(n_heads):
            q2 = q_ref[:, (h // 2) * pair:(h // 2 + 1) * pair]
            lo = head_dim * (h % 2)
            qh = jnp.where((lane >= lo) & (lane < lo + head_dim), q2, zero)
            ah = jnp.where((lane >= h * 8) & (lane < h * 8 + 8), a, zero)
            qf_ref[h] = jnp.concatenate([qh, ah], axis=1)
        m_ref[...] = jnp.full_like(m_ref, -jnp.inf)
        acc_ref[...] = jnp.zeros_like(acc_ref)

    def step_body(masked):
        ak = ak_ref[...]
        ones_blk = (lax.broadcasted_iota(jnp.int32, (16, bk), 0) == 0).astype(_BF16)
        if masked:
            causal = (lax.broadcasted_iota(jnp.int32, (bk, bq), 0)
                      <= lax.broadcasted_iota(jnp.int32, (bk, bq), 1))
        def scores(h):
            hp = h // 2
            kf = jnp.concatenate([k_ref[:, hp * pair:(hp + 1) * pair], ak], axis=1)
            return _dot_nt(kf, qf_ref[h])

        ahead = [scores(h) for h in range(FOX_LOOKAHEAD)]
        for h in range(n_heads):
            s = ahead.pop(0)
            if h + FOX_LOOKAHEAD < n_heads:
                ahead.append(scores(h + FOX_LOOKAHEAD))
            if masked:
                s = jnp.where(causal, s, -jnp.inf)
            m_old = m_ref[h]
            m_new = jnp.maximum(m_old, jnp.max(s, axis=0, keepdims=True))
            alpha = jnp.exp2(m_old - m_new)
            p = jnp.exp2(s - m_new).astype(_BF16)
            lhs = jnp.concatenate([vt_ref[h * head_dim:(h + 1) * head_dim, :], ones_blk], axis=0)
            acc_ref[h] = alpha * acc_ref[h] + _dot(lhs, p)
            m_ref[h] = m_new

    @pl.when(ki < qi)
    def _off_diagonal():
        step_body(False)

    @pl.when(ki == qi)
    def _diagonal():
        step_body(True)
        for hp in range(n_heads // 2):
            outs = []
            for h in (2 * hp, 2 * hp + 1):
                a = acc_ref[h]
                outs.append(a[0:head_dim] / a[head_dim:head_dim + 1])
            o_ref[:, hp * pair:(hp + 1) * pair] = jnp.concatenate(outs, axis=0).T


def _fox_prompt(qs, augq, kb, augk, vt, batch, seq, n_heads, head_dim):
    aw = n_heads * head_dim
    bq = bk = ATTN_BLOCK
    nq = seq // bq
    qs3 = qs.reshape(batch, seq, aw)
    kb3 = kb.reshape(batch, seq, aw)
    aq3 = augq.reshape(batch, seq, V7X_LANES)
    ak3 = augk.reshape(batch, seq, V7X_LANES)
    pairs = [(qi, ki) for qi in range(nq) for ki in range(qi + 1)]
    qi_tab = jnp.asarray([p[0] for p in pairs], jnp.int32)
    ki_tab = jnp.asarray([p[1] for p in pairs], jnp.int32)
    qmap = lambda b, s, qt, kt: (b, qt[s], 0)
    kmap = lambda b, s, qt, kt: (b, kt[s], 0)
    grid_spec = pltpu.PrefetchScalarGridSpec(
        num_scalar_prefetch=2,
        grid=(batch, len(pairs)),
        in_specs=[
            pl.BlockSpec((None, bq, aw), qmap),
            pl.BlockSpec((None, bq, V7X_LANES), qmap),
            pl.BlockSpec((None, bk, aw), kmap),
            pl.BlockSpec((None, bk, V7X_LANES), kmap),
            pl.BlockSpec((None, aw, bk), lambda b, s, qt, kt: (b, 0, kt[s])),
        ],
        out_specs=pl.BlockSpec((None, bq, aw), qmap),
        scratch_shapes=[
            pltpu.VMEM((n_heads, bq, 2 * V7X_LANES), _BF16),
            pltpu.VMEM((n_heads, 1, bq), _F32),
            pltpu.VMEM((n_heads, head_dim + 16, bq), _F32),
        ],
    )
    return pl.pallas_call(
        functools.partial(_fox_kernel, n_heads=n_heads, head_dim=head_dim),
        grid_spec=grid_spec,
        out_shape=jax.ShapeDtypeStruct((batch, seq, aw), _F32),
        compiler_params=_params(2),
        name="fox_prompt",
    )(qi_tab, ki_tab, qs3, aq3, kb3, ak3, vt)


def _rglru_sample_kernel(xr_ref, gr_ref, p1_ref, p2_ref, p3_ref, h0_ref, *refs, seg_len):
    rec_refs = refs[:7]
    y_ref, h_ref = refs[7:]
    tm, rw = xr_ref.shape
    xr = xr_ref[...]
    t = lax.broadcasted_iota(jnp.int32, (tm, rw), 0) & (seg_len - 1)
    prev = (p1_ref, p2_ref, p3_ref)
    shifted = [jnp.where(t >= k, pltpu.roll(xr, k, 0), prev[k - 1][...]) for k in range(1, CONV_WIDTH)]
    a, b = _rglru_coeffs(xr, shifted, rec_refs)
    d = 1
    while d < seg_len:
        a_new, b_new = _scan_step(a, b, pltpu.roll(a, d, 0), pltpu.roll(b, d, 0))
        a = jnp.where(t >= d, a_new, a)
        b = jnp.where(t >= d, b_new, b)
        d *= 2
    h = a * h0_ref[...] + b
    y_ref[...] = h * jax.nn.gelu(gr_ref[...])
    h_ref[...] = h


def _rglru_sample(xr, gr, prevs, h0, seg_len, rec_w):
    n, rw = xr.shape
    assert seg_len & (seg_len - 1) == 0 and n % seg_len == 0
    full = pl.BlockSpec((n, rw), lambda i: (0, 0))
    return pl.pallas_call(
        functools.partial(_rglru_sample_kernel, seg_len=seg_len),
        grid=(1,),
        in_specs=[full] * 6 + [_const_spec(c.shape) for c in rec_w],
        out_specs=(full, full),
        out_shape=(jax.ShapeDtypeStruct((n, rw), _F32), jax.ShapeDtypeStruct((n, rw), _F32)),
        compiler_params=_params(1),
        name="rglru_sample",
    )(xr, gr, *prevs, h0, *rec_w)


def _tail_kernel(x_ref, att_ref, rec_ref, p_ref, goa_ref, gor_ref, wout_ref, gffn_ref, wg_ref, wu_ref,
                 wo_ref, wple_ref, gple_ref, wpg_ref, o_ref):
    aw = att_ref.shape[1]
    an = _rms(att_ref[...], goa_ref[...]).astype(_BF16)
    rn = _rms(rec_ref[...], gor_ref[...]).astype(_BF16)
    x1 = x_ref[...] + _dot(an, wout_ref[0:aw, :]) + _dot(rn, wout_ref[aw:, :])
    hn = _rms(x1, gffn_ref[...]).astype(_BF16)
    hidden = wg_ref.shape[1]
    ffn = None
    for c0 in range(0, hidden, FFN_CHUNK):
        cw = min(FFN_CHUNK, hidden - c0)
        g = _dot(hn, wg_ref[:, c0:c0 + cw])
        u = _dot(hn, wu_ref[:, c0:c0 + cw])
        part = _dot((jax.nn.silu(g) * u).astype(_BF16), wo_ref[c0:c0 + cw, :])
        ffn = part if ffn is None else ffn + part
    x2 = x1 + ffn
    gate = jax.nn.sigmoid(_dot(_rms(x2, gple_ref[...]).astype(_BF16), wpg_ref[...]))
    o_ref[...] = x2 + _dot(p_ref[...].astype(_BF16), wple_ref[...]) * gate


def _tail(x2d, att, rec, p2d, tail_w):
    n, d = x2d.shape
    tm = min(PROJ_ROWS, n)
    row = lambda w: pl.BlockSpec((tm, w), lambda i: (i, 0))
    return pl.pallas_call(
        _tail_kernel,
        grid=(n // tm,),
        in_specs=[row(d), row(att.shape[1]), row(rec.shape[1]), row(p2d.shape[1])]
        + [_const_spec(c.shape) for c in tail_w],
        out_specs=row(d),
        out_shape=jax.ShapeDtypeStruct((n, d), _F32),
        compiler_params=_params(1),
        name="tail",
    )(x2d, att, rec, p2d, *tail_w)


def _sproj_kernel(x_ref, gmix_ref, w_ref, wf_ref, bf_ref, gq_ref, gk_ref, gmat_ref, seg_ref,
                  q_out, k_out, v_out, lf_out, cn_out, xr_out, gr_out, *, aw, rw):
    hn = _rms(x_ref[...], gmix_ref[...]).astype(_BF16)
    proj = _dot(hn, w_ref[...])
    gmat = gmat_ref[...]
    q_out[...] = _head_norm(proj[:, 0:aw], gq_ref[...], gmat)
    k_out[...] = _head_norm(proj[:, aw:2 * aw], gk_ref[...], gmat)
    v_out[...] = proj[:, 2 * aw:3 * aw]
    xr_out[...] = proj[:, 3 * aw:3 * aw + rw]
    gr_out[...] = proj[:, 3 * aw + rw:3 * aw + 2 * rw]
    lf = jax.nn.log_sigmoid(_dot(hn, wf_ref[...]) + bf_ref[...])
    lf_out[...] = lf
    hi, mid, lo = _split3(lf)
    seg = seg_ref[...]
    cn_out[...] = _dot(seg, hi.astype(_BF16)) + _dot(seg, mid.astype(_BF16)) + _dot(seg, lo.astype(_BF16))


def _proj_sample(x2d, t_new, gmix, w_main, wf_pad, bf_row, gq, gk, n_heads, head_dim, rw):
    n, d = x2d.shape
    aw = n_heads * head_dim
    idx = np.arange(n)
    seg = ((idx[:, None] // t_new == idx[None, :] // t_new) & (idx[None, :] <= idx[:, None])).astype(np.float32)
    ins = [x2d, gmix, w_main, wf_pad, bf_row, gq, gk,
           jnp.asarray(_group_mean_matrix(head_dim), _BF16), jnp.asarray(seg, _BF16)]
    full = lambda s: pl.BlockSpec(s, lambda i: (0,) * len(s))
    widths = (aw, aw, aw, V7X_LANES, V7X_LANES, rw, rw)
    return pl.pallas_call(
        functools.partial(_sproj_kernel, aw=aw, rw=rw),
        grid=(1,),
        in_specs=[full(a.shape) for a in ins],
        out_specs=tuple(full((n, w)) for w in widths),
        out_shape=tuple(jax.ShapeDtypeStruct((n, w), _F32) for w in widths),
        compiler_params=_params(1),
        name="proj_sample",
    )(*ins)


def _sattn_kernel(pt_ref, q_ref, kn_ref, vn_ref, cnq_ref, cnk_ref, *refs, pps, t_new, head_dim):
    del pt_ref
    k_refs = refs[:pps]
    v_refs = refs[pps:2 * pps]
    lf_refs = refs[2 * pps:3 * pps]
    uo_ref, o_ref, qbd_ref, m_ref, l_ref, acc_ref, carry_ref = refs[3 * pps:]
    j = pl.program_id(1)
    nh, aw = V7X_SUBLANES, q_ref.shape[1]
    rows = t_new * nh
    page = k_refs[0].shape[1]
    feat = lax.broadcasted_iota(jnp.int32, (nh, aw), 1)
    head = lax.broadcasted_iota(jnp.int32, (nh, aw), 0)
    head_mask = (feat >= head * head_dim) & (feat < (head + 1) * head_dim)

    @pl.when(j == 0)
    def _init():
        q = q_ref[...]
        qbd = jnp.concatenate(
            [jnp.where(head_mask, jnp.broadcast_to(q[t:t + 1], (nh, aw)), 0.0) for t in range(t_new)],
            axis=0).astype(_BF16)
        qbd_ref[...] = qbd
        pad = jnp.zeros((page - kn_ref.shape[0], aw), _F32)
        kn = jnp.concatenate([kn_ref[...], pad], axis=0).astype(_BF16)
        vn = jnp.concatenate([vn_ref[...], pad], axis=0).astype(_BF16)
        s = _dot_nt(qbd, kn) + cnq_ref[...] - cnk_ref[...]
        row = lax.broadcasted_iota(jnp.int32, (rows, page), 0)
        t_col = lax.broadcasted_iota(jnp.int32, (rows, page), 1)
        s = jnp.where(t_col * nh <= row, s, -jnp.inf)
        m = jnp.max(s, axis=1, keepdims=True)
        p = jnp.exp(s - m)
        m_ref[...] = m
        l_ref[...] = jnp.sum(p, axis=1, keepdims=True)
        acc_ref[...] = _dot(p.astype(_BF16), vn)
        carry_ref[...] = jnp.zeros_like(carry_ref)

    qbd = qbd_ref[...]
    carry = carry_ref[...]
    scores = []
    for i in range(pps):
        s = _dot(qbd, k_refs[i][...].astype(_BF16))
        r = _dot(_stack3(lf_refs[i][...], 4 * V7X_SUBLANES), uo_ref[...])
        bias = _sum3(r[:, :page]) + carry
        carry = carry + _sum3(r[:, page:])
        scores.append(s + jnp.concatenate([bias] * t_new, axis=0))
    carry_ref[...] = carry
    s_all = jnp.concatenate(scores, axis=1) + cnq_ref[...]
    m_old = m_ref[...]
    m_new = jnp.maximum(m_old, jnp.max(s_all, axis=1, keepdims=True))
    alpha = jnp.exp(m_old - m_new)
    p = jnp.exp(s_all - m_new)
    l_ref[...] = alpha * l_ref[...] + jnp.sum(p, axis=1, keepdims=True)
    vt_all = jnp.concatenate([v_refs[i][...].astype(_BF16) for i in range(pps)], axis=1)
    acc_ref[...] = alpha * acc_ref[...] + _dot_nt(p.astype(_BF16), vt_all)
    m_ref[...] = m_new

    @pl.when(j == pl.num_programs(1) - 1)
    def _fin():
        a = acc_ref[...] / l_ref[...]
        for t in range(t_new):
            o_ref[t:t + 1, :] = jnp.sum(jnp.where(head_mask, a[t * nh:(t + 1) * nh], 0.0), axis=0, keepdims=True)


def _suffix_matrix(page):
    idx = np.arange(page)
    strict = (idx[:, None] > idx[None, :]).astype(np.float32)
    return np.concatenate([strict, np.ones_like(strict)], axis=1)


def _attn_sample(q, k_new, v_new, cn, cache_kt, cache_vt, cache_lft, page_table, t_new, n_heads, head_dim):
    db, n_pages = page_table.shape
    page = cache_kt.shape[2]
    aw = n_heads * head_dim
    pps = PAGES_PER_STEP
    rows = t_new * n_heads
    pad_t = V7X_SUBLANES - t_new
    q3 = q.reshape(db, t_new, aw)
    kn = jnp.pad(k_new.reshape(db, t_new, aw), ((0, 0), (0, pad_t), (0, 0)))
    vn = jnp.pad(v_new.reshape(db, t_new, aw), ((0, 0), (0, pad_t), (0, 0)))
    cn3 = cn.reshape(db, t_new, n_heads)
    cnq = cn3.reshape(db, rows, 1)
    cnk = jnp.tile(jnp.swapaxes(cn3, 1, 2), (1, t_new, 1))
    cnk = jnp.pad(cnk, ((0, 0), (0, 0), (0, page - t_new)))

    def page_spec(width_shape, i):
        return pl.BlockSpec((None,) + width_shape,
                            lambda b, j, pt: (pt[b, n_pages - 1 - (j * pps + i)], 0, 0))

    per_b = lambda s: pl.BlockSpec((None,) + s, lambda b, j, pt: (b, 0, 0))
    uo = jnp.asarray(_suffix_matrix(page), _BF16)
    grid_spec = pltpu.PrefetchScalarGridSpec(
        num_scalar_prefetch=1,
        grid=(db, n_pages // pps),
        in_specs=[per_b((t_new, aw)), per_b((V7X_SUBLANES, aw)), per_b((V7X_SUBLANES, aw)),
                  per_b((rows, 1)), per_b((rows, page))]
        + [page_spec((aw, page), i) for i in range(pps)]
        + [page_spec((aw, page), i) for i in range(pps)]
        + [page_spec((V7X_SUBLANES, page), i) for i in range(pps)]
        + [pl.BlockSpec(uo.shape, lambda b, j, pt: (0, 0))],
        out_specs=per_b((t_new, aw)),
        scratch_shapes=[
            pltpu.VMEM((rows, aw), _BF16),
            pltpu.VMEM((rows, 1), _F32),
            pltpu.VMEM((rows, 1), _F32),
            pltpu.VMEM((rows, aw), _F32),
            pltpu.VMEM((V7X_SUBLANES, page), _F32),
        ],
    )
    out = pl.pallas_call(
        functools.partial(_sattn_kernel, pps=pps, t_new=t_new, head_dim=head_dim),
        grid_spec=grid_spec,
        out_shape=jax.ShapeDtypeStruct((db, t_new, aw), _F32),
        compiler_params=_params(2),
        name="attn_sample",
    )(page_table, q3, kn, vn, cnq, cnk, *([cache_kt] * pps), *([cache_vt] * pps), *([cache_lft] * pps), uo)
    return out.reshape(db * t_new, aw)


def _block_diag_pair(w):
    nb, dd, _ = w.shape
    per = V7X_MXU_DIM // dd
    tiles = []
    for half in range(nb // per):
        tile = jnp.zeros((V7X_MXU_DIM, V7X_MXU_DIM), w.dtype)
        for j in range(per):
            tile = lax.dynamic_update_slice(tile, w[half * per + j], (j * dd, j * dd))
        tiles.append(tile)
    return jnp.stack(tiles).astype(_BF16)


def kernel(x_prompt, x_sample, p_prompt, p_sample, cache_k, cache_v, cache_logf, state_conv, state_h, page_table, g_mix, w_in, b_f, g_q, g_k, w_conv, b_conv, w_a, b_a, w_x, b_x, lam, g_out_attn, g_out_rec, w_out, g_ffn, w_ffn_in, w_ffn_out, w_ple, g_ple, w_ple_gate):
    batch, seq, d_model = x_prompt.shape
    db, t_new, _ = x_sample.shape
    depth, n_heads, head_dim = g_q.shape
    aw = n_heads * head_dim
    rw = lam.shape[1]
    hidden = w_ffn_out.shape[1]
    n_phys, page = cache_k.shape[1], cache_k.shape[2]
    assert n_heads == V7X_SUBLANES and aw == 2 * V7X_MXU_DIM and rw == 2 * V7X_MXU_DIM
    assert seq % ATTN_BLOCK == 0 and seq % PROJ_ROWS == 0 and page_table.shape[1] % PAGES_PER_STEP == 0

    xp = x_prompt.reshape(batch * seq, d_model)
    xs = x_sample.reshape(db * t_new, d_model)
    outs = [[] for _ in range(10)]
    row = lambda a: a.reshape(1, -1)
    for l in range(depth):
        w_l = w_in[l]
        w_main = jnp.concatenate([w_l[:, :3 * aw], w_l[:, 3 * aw + n_heads:]], axis=1).astype(_BF16)
        w_f = w_l[:, 3 * aw:3 * aw + n_heads]
        wft = jnp.pad(w_f.T, ((0, 16 - n_heads), (0, 0))).astype(_BF16)
        wf_pad = jnp.pad(w_f, ((0, 0), (0, V7X_LANES - n_heads))).astype(_BF16)
        bf_col = b_f[l].reshape(n_heads, 1)
        bf_row = jnp.pad(b_f[l], (0, V7X_LANES - n_heads)).reshape(1, V7X_LANES)
        gmix = row(g_mix[l])
        gq = row(g_q[l]) * (head_dim ** -0.5)
        gk = row(g_k[l])
        rec_w = (w_conv[l], row(b_conv[l]), _block_diag_pair(w_a[l]), row(b_a[l]),
                 _block_diag_pair(w_x[l]), row(b_x[l]), row(lam[l]))
        tail_w = (row(g_out_attn[l]), row(g_out_rec[l]), w_out[l].astype(_BF16), row(g_ffn[l]),
                  w_ffn_in[l][:, :hidden].astype(_BF16), w_ffn_in[l][:, hidden:].astype(_BF16),
                  w_ffn_out[l].astype(_BF16), w_ple[l].astype(_BF16), row(g_ple[l]),
                  w_ple_gate[l].astype(_BF16))

        kt, vt32, lft, qs, kb, vt, augq, augk, rec, h_tiles, x_tiles = _proj_prompt(
            xp, seq, gmix, w_main, wft, bf_col, gq * LOG2_E, gk, rec_w, n_heads, head_dim, rw)
        att = _fox_prompt(qs, augq, kb, augk, vt, batch, seq, n_heads, head_dim).reshape(batch * seq, aw)
        xp = _tail(xp, att, rec, p_prompt[l].reshape(batch * seq, -1), tail_w)
        outs[0].append(kt.reshape(batch, n_heads, head_dim, seq).transpose(0, 3, 1, 2))
        outs[1].append(vt32.reshape(batch, n_heads, head_dim, seq).transpose(0, 3, 1, 2))
        outs[2].append(lft.transpose(0, 2, 1))
        outs[3].append(x_tiles.reshape(batch, -1, V7X_SUBLANES, rw)[:, -1, V7X_SUBLANES - (CONV_WIDTH - 1):])
        outs[4].append(h_tiles.reshape(batch, -1, V7X_SUBLANES, rw)[:, -1, -1])

        q_s, k_s, v_s, lf_s, cn_s, xr_s, gr_s = _proj_sample(
            xs, t_new, gmix, w_main, wf_pad, bf_row, gq, gk, n_heads, head_dim, rw)
        lf_s = lf_s[:, :n_heads]
        cache_lft = jnp.swapaxes(cache_logf[l], 1, 2)
        cache_kt = cache_k[l].transpose(0, 2, 3, 1).reshape(n_phys, aw, page)
        cache_vt = cache_v[l].transpose(0, 2, 3, 1).reshape(n_phys, aw, page)
        att_s = _attn_sample(q_s, k_s, v_s, cn_s[:, :n_heads], cache_kt, cache_vt, cache_lft,
                             page_table, t_new, n_heads, head_dim)
        hist = jnp.concatenate([state_conv[l], jnp.zeros((db, t_new, rw), _F32)], axis=1)
        prevs = tuple(hist[:, CONV_WIDTH - 1 - s:CONV_WIDTH - 1 - s + t_new].reshape(db * t_new, rw)
                      for s in range(1, CONV_WIDTH))
        h0 = jnp.repeat(state_h[l], t_new, axis=0)
        rec_s, h_s = _rglru_sample(xr_s, gr_s, prevs, h0, t_new, rec_w)
        xs = _tail(xs, att_s, rec_s, p_sample[l].reshape(db * t_new, -1), tail_w)
        outs[5].append(k_s.reshape(db, t_new, n_heads, head_dim))
        outs[6].append(v_s.reshape(db, t_new, n_heads, head_dim))
        outs[7].append(lf_s.reshape(db, t_new, n_heads))
        outs[8].append(xr_s.reshape(db, t_new, rw)[:, t_new - (CONV_WIDTH - 1):])
        outs[9].append(h_s.reshape(db, t_new, rw)[:, -1])
    return (xp.reshape(batch, seq, d_model), xs.reshape(db, t_new, d_model),
            *(jnp.stack(o) for o in outs))
```

```python
import functools

import numpy as np
import jax
import jax.numpy as jnp
from jax import lax
from jax.experimental import pallas as pl
from jax.experimental.pallas import tpu as pltpu

RG_C = 8.0
NORM_EPS = 1e-6
LOG2_E = 1.4426950408889634
CONV_WIDTH = 4

V7X_LANES = 128
V7X_SUBLANES = 8
V7X_MXU_DIM = 256
V7X_VMEM_LIMIT_BYTES = 56 * 1024 * 1024

PROJ_ROWS = 512
ATTN_BLOCK = 512
FOX_LOOKAHEAD = 2
FFN_CHUNK = 512

_F32 = jnp.float32
_BF16 = jnp.bfloat16
_NT = (((1,), (1,)), ((), ()))


def _dot(a, b):
    return jnp.dot(a, b, preferred_element_type=_F32)


def _dot_nt(a, b):
    return lax.dot_general(a, b, _NT, preferred_element_type=_F32)


def _rms(x, g):
    return x * lax.rsqrt(jnp.mean(x * x, axis=-1, keepdims=True) + NORM_EPS) * g


def _split3(z):
    hi = z.astype(_BF16).astype(_F32)
    r = z - hi
    mid = r.astype(_BF16).astype(_F32)
    lo = r - mid
    return hi, mid, lo


def _stack3(z, rows):
    hi, mid, lo = _split3(z)
    pad = jnp.zeros((rows - 3 * V7X_SUBLANES, z.shape[1]), _F32)
    return jnp.concatenate([hi, mid, lo, pad], axis=0).astype(_BF16)


def _sum3(r):
    return r[0:8] + r[8:16] + r[16:24]


def _head_norm(z, g, gmat):
    zz = (z * z).astype(_BF16)
    half = gmat.shape[0]
    ms = jnp.concatenate([_dot(zz[:, :half], gmat), _dot(zz[:, half:], gmat)], axis=1)
    return z * lax.rsqrt(ms + NORM_EPS) * g


def _const_spec(shape):
    nd = len(shape)
    return pl.BlockSpec(shape, lambda *_: (0,) * nd, pipeline_mode=pl.Buffered(1))


def _params(n_axes):
    return pltpu.CompilerParams(
        dimension_semantics=("arbitrary",) * n_axes,
        vmem_limit_bytes=V7X_VMEM_LIMIT_BYTES,
    )


def _sigmoid(z):
    return 0.5 * jnp.tanh(0.5 * z) + 0.5


def _rglru_coeffs(xr, shifted, rec_refs):
    wconv_ref, bconv_ref, wa_ref, ba_ref, wx_ref, bx_ref, lam_ref = rec_refs
    w = wconv_ref[...]
    xc = bconv_ref[...] + shifted[2] * w[0:1]
    xc = xc + shifted[1] * w[1:2]
    xc = xc + shifted[0] * w[2:3]
    xc = xc + xr * w[3:4]

    xcb = xc.astype(_BF16)
    half = wa_ref.shape[1]

    def gate(w_ref, b_ref):
        z = jnp.concatenate([_dot(xcb[:, :half], w_ref[0]), _dot(xcb[:, half:], w_ref[1])], axis=1)
        return _sigmoid(z + b_ref[...])

    r_gate = gate(wa_ref, ba_ref)
    i_gate = gate(wx_ref, bx_ref)
    log_a = -RG_C * r_gate * jax.nn.softplus(-lam_ref[...])
    a = jnp.exp(log_a)
    om = -jnp.tanh(log_a) * (a * a + 1.0)
    root = jnp.where(om > 0.0, om * lax.rsqrt(om), 0.0)
    return a, root * (i_gate * xc)


def _scan_step(a, b, a_s, b_s):
    return a * a_s, b + a * b_s


def _rglru_prompt_tile(xr, gr, hc, rec_refs, xbuf_ref, abuf_ref, bbuf_ref):
    tm, rw = xr.shape
    sub = V7X_SUBLANES
    xbuf_ref[sub:sub + tm] = xr
    shifted = [xbuf_ref[sub - k:sub - k + tm] for k in range(1, CONV_WIDTH)]
    a, b = _rglru_coeffs(xr, shifted, rec_refs)
    abuf_ref[0:sub] = jnp.ones((sub, rw), _F32)
    bbuf_ref[0:sub] = jnp.zeros((sub, rw), _F32)
    d = 1
    while d < tm:
        if d % sub == 0:
            a_new, b_new = _scan_step(a[d:], b[d:], a[:tm - d], b[:tm - d])
            a = jnp.concatenate([a[:d], a_new], axis=0)
            b = jnp.concatenate([b[:d], b_new], axis=0)
        else:
            abuf_ref[sub:sub + tm] = a
            bbuf_ref[sub:sub + tm] = b
            a, b = _scan_step(a, b, abuf_ref[sub - d:sub - d + tm], bbuf_ref[sub - d:sub - d + tm])
        d *= 2
    h = a * hc + b
    return h * jax.nn.gelu(gr), h


def _proj_kernel(x_ref, gmix_ref, w_ref, wft_ref, bf_ref, gq_ref, gk_ref, gmat_ref, cum_ref,
                 aq_ref, ak_ref, cq_ref, ck_ref, *refs, tiles_per_seq, aw, rw):
    rec_refs = refs[:7]
    (k_out, v_out, lft_out, qs_out, kb_out, vt_out, augq_out, augk_out, rec_out, h_out, xtail_out,
     carry_ref, hc_ref, xbuf_ref, abuf_ref, bbuf_ref) = refs[7:]
    i = pl.program_id(0)
    tm = x_ref.shape[0]
    sub = V7X_SUBLANES
    seq_start = i % tiles_per_seq == 0

    @pl.when(seq_start)
    def _():
        carry_ref[...] = jnp.zeros_like(carry_ref)
        hc_ref[...] = jnp.zeros_like(hc_ref)
        xbuf_ref[0:sub] = jnp.zeros((sub, rw), _F32)

    @pl.when(jnp.logical_not(seq_start))
    def _():
        xbuf_ref[0:sub] = xbuf_ref[tm:tm + sub]

    hn = _rms(x_ref[...], gmix_ref[...]).astype(_BF16)
    w_rec = 3 * aw
    proj_rec = _dot(hn, w_ref[:, w_rec:w_rec + 2 * rw])
    xr = proj_rec[:, 0:rw]
    y, h = _rglru_prompt_tile(xr, proj_rec[:, rw:2 * rw], hc_ref[...], rec_refs,
                              xbuf_ref, abuf_ref, bbuf_ref)
    hc_ref[...] = h[tm - 1:tm]
    rec_out[...] = y
    h_out[...] = h[tm - sub:tm]
    xtail_out[...] = xr[tm - sub:tm]

    proj = _dot(hn, w_ref[:, 0:w_rec])
    q = proj[:, 0:aw]
    k = proj[:, aw:2 * aw]
    v = proj[:, 2 * aw:3 * aw]

    gmat = gmat_ref[...]
    qn = _head_norm(q, gq_ref[...], gmat)
    kn = _head_norm(k, gk_ref[...], gmat)
    vt = v.T
    k_out[...] = kn.T
    v_out[...] = vt
    qs_out[...] = qn.astype(_BF16)
    kb_out[...] = kn.astype(_BF16)
    vt_out[...] = vt.astype(_BF16)

    ft = _dot_nt(wft_ref[...], hn)
    lft = jax.nn.log_sigmoid(ft[0:8] + bf_ref[...])
    lft_out[...] = lft

    p3 = _stack3(lft, V7X_LANES)
    blk = cum_ref.shape[0]
    carry = carry_ref[...]
    cums = []
    for c in range(tm // blk):
        r = _dot(p3[:, c * blk:(c + 1) * blk], cum_ref[...])
        cums.append(_sum3(r[:, :blk]) + carry)
        carry = carry + _sum3(r[:, blk:])
    carry_ref[...] = carry
    cum = jnp.concatenate(cums, axis=1)

    c3 = _stack3(cum * LOG2_E, V7X_LANES)
    qat = _dot(aq_ref[...], c3) + cq_ref[...]
    kat = _dot(ak_ref[...], c3) + ck_ref[...]
    augq_out[...] = qat.T.astype(_BF16)
    augk_out[...] = kat.T.astype(_BF16)


def _bias_placement(n_heads):
    aq = np.zeros((V7X_LANES, V7X_LANES), np.float32)
    ak = np.zeros((V7X_LANES, V7X_LANES), np.float32)
    cq = np.zeros((V7X_LANES, 1), np.float32)
    ck = np.zeros((V7X_LANES, 1), np.float32)
    for h in range(n_heads):
        for j in range(3):
            aq[h * 8 + j, j * 8 + h] = 1.0
            cq[h * 8 + 3 + j, 0] = 1.0
            ak[h * 8 + 3 + j, j * 8 + h] = -1.0
            ck[h * 8 + j, 0] = 1.0
    return aq, ak, cq, ck


def _group_mean_matrix(head_dim):
    idx = np.arange(V7X_MXU_DIM) // head_dim
    return (idx[:, None] == idx[None, :]).astype(np.float32) / head_dim


def _prefix_matrix():
    idx = np.arange(V7X_MXU_DIM)
    incl = (idx[:, None] <= idx[None, :]).astype(np.float32)
    return np.concatenate([incl, np.ones_like(incl)], axis=1)


def _proj_prompt(x2d, seq, gmix, w_main, wft, bf_col, gq, gk, rec_w, n_heads, head_dim, rw):
    n, d = x2d.shape
    tm = PROJ_ROWS
    aw = n_heads * head_dim
    assert seq & (seq - 1) == 0, "sequence length must be a power of two"
    aq, ak, cq, ck = _bias_placement(n_heads)
    consts = [
        gmix, w_main, wft, bf_col, gq, gk,
        jnp.asarray(_group_mean_matrix(head_dim), _BF16),
        jnp.asarray(_prefix_matrix(), _BF16),
        jnp.asarray(aq, _BF16), jnp.asarray(ak, _BF16), jnp.asarray(cq), jnp.asarray(ck),
        *rec_w,
    ]
    tps = seq // tm
    batch = n // seq
    row = lambda w: pl.BlockSpec((tm, w), lambda i: (i, 0))
    seq_t = lambda r: pl.BlockSpec((None, r, tm), lambda i: (i // tps, 0, i % tps))
    out_shape = (
        jax.ShapeDtypeStruct((batch, aw, seq), _F32),
        jax.ShapeDtypeStruct((batch, aw, seq), _F32),
        jax.ShapeDtypeStruct((batch, V7X_SUBLANES, seq), _F32),
        jax.ShapeDtypeStruct((n, aw), _BF16),
        jax.ShapeDtypeStruct((n, aw), _BF16),
        jax.ShapeDtypeStruct((batch, aw, seq), _BF16),
        jax.ShapeDtypeStruct((n, V7X_LANES), _BF16),
        jax.ShapeDtypeStruct((n, V7X_LANES), _BF16),
        jax.ShapeDtypeStruct((n, rw), _F32),
        jax.ShapeDtypeStruct((n // tm * V7X_SUBLANES, rw), _F32),
        jax.ShapeDtypeStruct((n // tm * V7X_SUBLANES, rw), _F32),
    )
    tail8 = pl.BlockSpec((V7X_SUBLANES, rw), lambda i: (i, 0))
    out_specs = (
        seq_t(aw), seq_t(aw), seq_t(V7X_SUBLANES), row(aw), row(aw), seq_t(aw),
        row(V7X_LANES), row(V7X_LANES), row(rw), tail8, tail8,
    )
    return pl.pallas_call(
        functools.partial(_proj_kernel, tiles_per_seq=tps, aw=aw, rw=rw),
        grid=(n // tm,),
        in_specs=[row(d)] + [_const_spec(c.shape) for c in consts],
        out_specs=out_specs,
        out_shape=out_shape,
        scratch_shapes=[pltpu.VMEM((V7X_SUBLANES, V7X_MXU_DIM), _F32), pltpu.VMEM((1, rw), _F32)]
        + [pltpu.VMEM((V7X_SUBLANES + tm, rw), _F32)] * 3,
        compiler_params=_params(1),
        name="proj_prompt",
    )(x2d, *consts)


class _PagedSide:
    def __init__(self, small, k_refs, v_refs, lf_refs, uo_ref, o_ref, scratch, t_new, head_dim, n_pages):
        self.q_ref, self.kn_ref, self.vn_ref, self.cnq_ref, self.cnk_ref = small
        self.k_refs, self.v_refs, self.lf_refs, self.uo_ref, self.o_ref = k_refs, v_refs, lf_refs, uo_ref, o_ref
        self.qbd_ref, self.m_ref, self.l_ref, self.acc_ref, self.carry_ref = scratch
        self.t_new, self.head_dim, self.n_pages = t_new, head_dim, n_pages
        self.nh = V7X_SUBLANES
        self.aw = self.q_ref.shape[1]
        self.page = k_refs[0].shape[1]

    def _head_mask(self):
        feat = lax.broadcasted_iota(jnp.int32, (self.nh, self.aw), 1)
        head = lax.broadcasted_iota(jnp.int32, (self.nh, self.aw), 0)
        return (feat >= head * self.head_dim) & (feat < (head + 1) * self.head_dim)

    def init(self):
        nh, aw, page, t_new = self.nh, self.aw, self.page, self.t_new
        rows = t_new * nh
        q = self.q_ref[...]
        head_mask = self._head_mask()
        qbd = jnp.concatenate(
            [jnp.where(head_mask, jnp.broadcast_to(q[t:t + 1], (nh, aw)), 0.0) for t in range(t_new)],
            axis=0).astype(_BF16)
        self.qbd_ref[...] = qbd
        pad = jnp.zeros((page - self.kn_ref.shape[0], aw), _F32)
        kn = jnp.concatenate([self.kn_ref[...], pad], axis=0).astype(_BF16)
        vn = jnp.concatenate([self.vn_ref[...], pad], axis=0).astype(_BF16)
        s = _dot_nt(qbd, kn) + self.cnq_ref[...] - self.cnk_ref[...]
        row = lax.broadcasted_iota(jnp.int32, (rows, page), 0)
        t_col = lax.broadcasted_iota(jnp.int32, (rows, page), 1)
        s = jnp.where(t_col * nh <= row, s, -jnp.inf)
        m = jnp.max(s, axis=1, keepdims=True)
        p = jnp.exp(s - m)
        self.m_ref[...] = m
        self.l_ref[...] = jnp.sum(p, axis=1, keepdims=True)
        self.acc_ref[...] = _dot(p.astype(_BF16), vn)
        self.carry_ref[...] = jnp.zeros_like(self.carry_ref)

    def scores(self, step):
        pps, page = len(self.k_refs), self.page
        qbd = self.qbd_ref[...]
        carry = self.carry_ref[...]
        scores = []
        for i in range(pps):
            valid = step * pps + i < self.n_pages
            s = _dot(qbd, self.k_refs[i][...].astype(_BF16))
            lf = jnp.where(valid, self.lf_refs[i][...], 0.0)
            r = _dot(_stack3(lf, 4 * V7X_SUBLANES), self.uo_ref[...])
            bias = _sum3(r[:, :page]) + carry
            carry = carry + _sum3(r[:, page:])
            s = s + jnp.concatenate([bias] * self.t_new, axis=0)
            scores.append(jnp.where(valid, s, -jnp.inf))
        self.carry_ref[...] = carry
        s_all = jnp.concatenate(scores, axis=1) + self.cnq_ref[...]
        m_old = self.m_ref[...]
        m_new = jnp.maximum(m_old, jnp.max(s_all, axis=1, keepdims=True))
        alpha = jnp.exp(m_old - m_new)
        p = jnp.exp(s_all - m_new)
        self.l_ref[...] = alpha * self.l_ref[...] + jnp.sum(p, axis=1, keepdims=True)
        self.m_ref[...] = m_new
        return p.astype(_BF16), alpha

    def values(self, p, alpha):
        vt_all = jnp.concatenate([r[...].astype(_BF16) for r in self.v_refs], axis=1)
        self.acc_ref[...] = alpha * self.acc_ref[...] + _dot_nt(p, vt_all)

    def finish(self):
        nh = self.nh
        a = self.acc_ref[...] / self.l_ref[...]
        head_mask = self._head_mask()
        for t in range(self.t_new):
            self.o_ref[t:t + 1, :] = jnp.sum(jnp.where(head_mask, a[t * nh:(t + 1) * nh], 0.0),
                                             axis=0, keepdims=True)


def _fox_kernel(qi_tab, ki_tab, pt_ref, q_ref, aq_ref, k_ref, ak_ref, vt_ref, *refs,
                n_heads, head_dim, n_samp, pps, t_new, n_pages):
    del pt_ref
    n_small = 5
    small = refs[:n_small * n_samp]
    pages = refs[n_small * n_samp:(n_small + 3 * pps) * n_samp]
    k_pages, v_pages, lf_pages = (pages[j * n_samp * pps:(j + 1) * n_samp * pps] for j in range(3))
    rest = refs[(n_small + 3 * pps) * n_samp:]
    uo_ref, o_ref = rest[0], rest[1]
    so_refs = rest[2:2 + n_samp]
    qf_ref, m_ref, acc_ref = rest[2 + n_samp:5 + n_samp]
    s_scratch = rest[5 + n_samp:]
    paged = [
        _PagedSide(small[u * n_small:(u + 1) * n_small], k_pages[u * pps:(u + 1) * pps],
                   v_pages[u * pps:(u + 1) * pps], lf_pages[u * pps:(u + 1) * pps], uo_ref, so_refs[u],
                   s_scratch[u * 5:(u + 1) * 5], t_new, head_dim, n_pages)
        for u in range(n_samp)]

    step = pl.program_id(1)
    qi = qi_tab[step]
    ki = ki_tab[step]
    bq = q_ref.shape[0]
    bk = k_ref.shape[0]
    pair = 2 * head_dim

    @pl.when(step == 0)
    def _paged_init():
        for side in paged:
            side.init()

    @pl.when(ki == 0)
    def _init():
        lane = lax.broadcasted_iota(jnp.int32, (bq, V7X_LANES), 1)
        a = aq_ref[...]
        zero = jnp.zeros_like(a)
        for h in range(n_heads):
            q2 = q_ref[:, (h // 2) * pair:(h // 2 + 1) * pair]
            lo = head_dim * (h % 2)
            qh = jnp.where((lane >= lo) & (lane < lo + head_dim), q2, zero)
            ah = jnp.where((lane >= h * 8) & (lane < h * 8 + 8), a, zero)
            qf_ref[h] = jnp.concatenate([qh, ah], axis=1)
        m_ref[...] = jnp.full_like(m_ref, -jnp.inf)
        acc_ref[...] = jnp.zeros_like(acc_ref)

    def step_body(masked):
        ak = ak_ref[...]
        ones_blk = (lax.broadcasted_iota(jnp.int32, (16, bk), 0) == 0).astype(_BF16)
        if masked:
            causal = (lax.broadcasted_iota(jnp.int32, (bk, bq), 0)
                      <= lax.broadcasted_iota(jnp.int32, (bk, bq), 1))
        def scores(h):
            hp = h // 2
            kf = jnp.concatenate([k_ref[:, hp * pair:(hp + 1) * pair], ak], axis=1)
            return _dot_nt(kf, qf_ref[h])

        paged_p = [side.scores(step) for side in paged]
        ahead = [scores(h) for h in range(FOX_LOOKAHEAD)]
        for side, (p_s, alpha_s) in zip(paged, paged_p):
            side.values(p_s, alpha_s)
        for h in range(n_heads):
            s = ahead.pop(0)
            if h + FOX_LOOKAHEAD < n_heads:
                ahead.append(scores(h + FOX_LOOKAHEAD))
            if masked:
                s = jnp.where(causal, s, -jnp.inf)
            m_old = m_ref[h]
            m_new = jnp.maximum(m_old, jnp.max(s, axis=0, keepdims=True))
            alpha = jnp.exp2(m_old - m_new)
            p = jnp.exp2(s - m_new).astype(_BF16)
            lhs = jnp.concatenate([vt_ref[h * head_dim:(h + 1) * head_dim, :], ones_blk], axis=0)
            acc_ref[h] = alpha * acc_ref[h] + _dot(lhs, p)
            m_ref[h] = m_new

    @pl.when(ki < qi)
    def _off_diagonal():
        step_body(False)

    @pl.when(ki == qi)
    def _diagonal():
        step_body(True)
        for hp in range(n_heads // 2):
            outs = []
            for h in (2 * hp, 2 * hp + 1):
                a = acc_ref[h]
                outs.append(a[0:head_dim] / a[head_dim:head_dim + 1])
            o_ref[:, hp * pair:(hp + 1) * pair] = jnp.concatenate(outs, axis=0).T

    @pl.when(step == pl.num_programs(1) - 1)
    def _paged_finish():
        for side in paged:
            side.finish()


def _suffix_matrix(page):
    idx = np.arange(page)
    strict = (idx[:, None] > idx[None, :]).astype(np.float32)
    return np.concatenate([strict, np.ones_like(strict)], axis=1)


def _attention(qs, augq, kb, augk, vt, batch, seq, q_s, k_new, v_new, cn, cache_kt, cache_vt, cache_lft,
               page_table, t_new, n_heads, head_dim):
    aw = n_heads * head_dim
    bq = bk = ATTN_BLOCK
    nq = seq // bq
    qs3 = qs.reshape(batch, seq, aw)
    kb3 = kb.reshape(batch, seq, aw)
    aq3 = augq.reshape(batch, seq, V7X_LANES)
    ak3 = augk.reshape(batch, seq, V7X_LANES)
    pairs = [(qi, ki) for qi in range(nq) for ki in range(qi + 1)]
    qi_tab = jnp.asarray([p[0] for p in pairs], jnp.int32)
    ki_tab = jnp.asarray([p[1] for p in pairs], jnp.int32)
    n_steps = len(pairs)

    db, n_pages = page_table.shape
    page = cache_kt.shape[2]
    assert db % batch == 0, "sample batches must split evenly over the prompt batches"
    n_samp = db // batch
    pps = -(-n_pages // n_steps)
    rows = t_new * n_heads
    pad_t = V7X_SUBLANES - t_new
    q3 = q_s.reshape(db, t_new, aw)
    kn = jnp.pad(k_new.reshape(db, t_new, aw), ((0, 0), (0, pad_t), (0, 0)))
    vn = jnp.pad(v_new.reshape(db, t_new, aw), ((0, 0), (0, pad_t), (0, 0)))
    cn3 = cn.reshape(db, t_new, n_heads)
    cnq = cn3.reshape(db, rows, 1)
    cnk = jnp.tile(jnp.swapaxes(cn3, 1, 2), (1, t_new, 1))
    cnk = jnp.pad(cnk, ((0, 0), (0, 0), (0, page - t_new)))
    uo = jnp.asarray(_suffix_matrix(page), _BF16)

    qmap = lambda b, s, qt, kt, pt: (b, qt[s], 0)
    kmap = lambda b, s, qt, kt, pt: (b, kt[s], 0)

    def samp_spec(shape, u):
        return pl.BlockSpec((None,) + shape, lambda b, s, qt, kt, pt: (b * n_samp + u, 0, 0))

    def page_spec(shape, u, i):
        return pl.BlockSpec(
            (None,) + shape,
            lambda b, s, qt, kt, pt: (pt[b * n_samp + u, jnp.maximum(n_pages - 1 - (s * pps + i), 0)], 0, 0))

    small_specs, small_args = [], []
    for u in range(n_samp):
        small_specs += [samp_spec((t_new, aw), u), samp_spec((V7X_SUBLANES, aw), u),
                        samp_spec((V7X_SUBLANES, aw), u), samp_spec((rows, 1), u), samp_spec((rows, page), u)]
        small_args += [q3, kn, vn, cnq, cnk]
    page_specs = []
    for shape in ((aw, page), (aw, page), (V7X_SUBLANES, page)):
        page_specs += [page_spec(shape, u, i) for u in range(n_samp) for i in range(pps)]
    page_args = [cache_kt] * (n_samp * pps) + [cache_vt] * (n_samp * pps) + [cache_lft] * (n_samp * pps)

    grid_spec = pltpu.PrefetchScalarGridSpec(
        num_scalar_prefetch=3,
        grid=(batch, n_steps),
        in_specs=[
            pl.BlockSpec((None, bq, aw), qmap),
            pl.BlockSpec((None, bq, V7X_LANES), qmap),
            pl.BlockSpec((None, bk, aw), kmap),
            pl.BlockSpec((None, bk, V7X_LANES), kmap),
            pl.BlockSpec((None, aw, bk), lambda b, s, qt, kt, pt: (b, 0, kt[s])),
        ] + small_specs + page_specs + [pl.BlockSpec(uo.shape, lambda b, s, qt, kt, pt: (0, 0))],
        out_specs=[pl.BlockSpec((None, bq, aw), qmap)]
        + [pl.BlockSpec((None, t_new, aw), lambda b, s, qt, kt, pt: (b, 0, 0))] * n_samp,
        scratch_shapes=[
            pltpu.VMEM((n_heads, bq, 2 * V7X_LANES), _BF16),
            pltpu.VMEM((n_heads, 1, bq), _F32),
            pltpu.VMEM((n_heads, head_dim + 16, bq), _F32),
        ] + [
            pltpu.VMEM((rows, aw), _BF16),
            pltpu.VMEM((rows, 1), _F32),
            pltpu.VMEM((rows, 1), _F32),
            pltpu.VMEM((rows, aw), _F32),
            pltpu.VMEM((V7X_SUBLANES, page), _F32),
        ] * n_samp,
    )
    outs = pl.pallas_call(
        functools.partial(_fox_kernel, n_heads=n_heads, head_dim=head_dim, n_samp=n_samp, pps=pps,
                          t_new=t_new, n_pages=n_pages),
        grid_spec=grid_spec,
        out_shape=[jax.ShapeDtypeStruct((batch, seq, aw), _F32)]
        + [jax.ShapeDtypeStruct((batch, t_new, aw), _F32)] * n_samp,
        compiler_params=_params(2),
        name="attention",
    )(qi_tab, ki_tab, page_table, qs3, aq3, kb3, ak3, vt, *small_args, *page_args, uo)
    att = outs[0].reshape(batch * seq, aw)
    att_s = jnp.stack(outs[1:], axis=1)
    return att, att_s.reshape(db * t_new, aw)


def _rglru_sample_kernel(xr_ref, gr_ref, p1_ref, p2_ref, p3_ref, h0_ref, *refs, seg_len):
    rec_refs = refs[:7]
    y_ref, h_ref = refs[7:]
    tm, rw = xr_ref.shape
    xr = xr_ref[...]
    t = lax.broadcasted_iota(jnp.int32, (tm, rw), 0) & (seg_len - 1)
    prev = (p1_ref, p2_ref, p3_ref)
    shifted = [jnp.where(t >= k, pltpu.roll(xr, k, 0), prev[k - 1][...]) for k in range(1, CONV_WIDTH)]
    a, b = _rglru_coeffs(xr, shifted, rec_refs)
    d = 1
    while d < seg_len:
        a_new, b_new = _scan_step(a, b, pltpu.roll(a, d, 0), pltpu.roll(b, d, 0))
        a = jnp.where(t >= d, a_new, a)
        b = jnp.where(t >= d, b_new, b)
        d *= 2
    h = a * h0_ref[...] + b
    y_ref[...] = h * jax.nn.gelu(gr_ref[...])
    h_ref[...] = h


def _rglru_sample(xr, gr, prevs, h0, seg_len, rec_w):
    n, rw = xr.shape
    assert seg_len & (seg_len - 1) == 0 and n % seg_len == 0
    full = pl.BlockSpec((n, rw), lambda i: (0, 0))
    return pl.pallas_call(
        functools.partial(_rglru_sample_kernel, seg_len=seg_len),
        grid=(1,),
        in_specs=[full] * 6 + [_const_spec(c.shape) for c in rec_w],
        out_specs=(full, full),
        out_shape=(jax.ShapeDtypeStruct((n, rw), _F32), jax.ShapeDtypeStruct((n, rw), _F32)),
        compiler_params=_params(1),
        name="rglru_sample",
    )(xr, gr, *prevs, h0, *rec_w)


def _tail_kernel(x_ref, att_ref, rec_ref, p_ref, goa_ref, gor_ref, wout_ref, gffn_ref, wg_ref, wu_ref,
                 wo_ref, wple_ref, gple_ref, wpg_ref, o_ref):
    aw = att_ref.shape[1]
    an = _rms(att_ref[...], goa_ref[...]).astype(_BF16)
    rn = _rms(rec_ref[...], gor_ref[...]).astype(_BF16)
    x1 = x_ref[...] + _dot(an, wout_ref[0:aw, :]) + _dot(rn, wout_ref[aw:, :])
    hn = _rms(x1, gffn_ref[...]).astype(_BF16)
    hidden = wg_ref.shape[1]
    ffn = None
    for c0 in range(0, hidden, FFN_CHUNK):
        cw = min(FFN_CHUNK, hidden - c0)
        g = _dot(hn, wg_ref[:, c0:c0 + cw])
        u = _dot(hn, wu_ref[:, c0:c0 + cw])
        part = _dot((jax.nn.silu(g) * u).astype(_BF16), wo_ref[c0:c0 + cw, :])
        ffn = part if ffn is None else ffn + part
    x2 = x1 + ffn
    gate = jax.nn.sigmoid(_dot(_rms(x2, gple_ref[...]).astype(_BF16), wpg_ref[...]))
    o_ref[...] = x2 + _dot(p_ref[...].astype(_BF16), wple_ref[...]) * gate


def _tail(x2d, att, rec, p2d, tail_w):
    n, d = x2d.shape
    tm = min(PROJ_ROWS, n)
    row = lambda w: pl.BlockSpec((tm, w), lambda i: (i, 0))
    return pl.pallas_call(
        _tail_kernel,
        grid=(n // tm,),
        in_specs=[row(d), row(att.shape[1]), row(rec.shape[1]), row(p2d.shape[1])]
        + [_const_spec(c.shape) for c in tail_w],
        out_specs=row(d),
        out_shape=jax.ShapeDtypeStruct((n, d), _F32),
        compiler_params=_params(1),
        name="tail",
    )(x2d, att, rec, p2d, *tail_w)


def _sproj_kernel(x_ref, gmix_ref, w_ref, wf_ref, bf_ref, gq_ref, gk_ref, gmat_ref, seg_ref,
                  q_out, k_out, v_out, lf_out, cn_out, xr_out, gr_out, *, aw, rw):
    hn = _rms(x_ref[...], gmix_ref[...]).astype(_BF16)
    proj = _dot(hn, w_ref[...])
    gmat = gmat_ref[...]
    q_out[...] = _head_norm(proj[:, 0:aw], gq_ref[...], gmat)
    k_out[...] = _head_norm(proj[:, aw:2 * aw], gk_ref[...], gmat)
    v_out[...] = proj[:, 2 * aw:3 * aw]
    xr_out[...] = proj[:, 3 * aw:3 * aw + rw]
    gr_out[...] = proj[:, 3 * aw + rw:3 * aw + 2 * rw]
    lf = jax.nn.log_sigmoid(_dot(hn, wf_ref[...]) + bf_ref[...])
    lf_out[...] = lf
    hi, mid, lo = _split3(lf)
    seg = seg_ref[...]
    cn_out[...] = _dot(seg, hi.astype(_BF16)) + _dot(seg, mid.astype(_BF16)) + _dot(seg, lo.astype(_BF16))


def _proj_sample(x2d, t_new, gmix, w_main, wf_pad, bf_row, gq, gk, n_heads, head_dim, rw):
    n, d = x2d.shape
    aw = n_heads * head_dim
    idx = np.arange(n)
    seg = ((idx[:, None] // t_new == idx[None, :] // t_new) & (idx[None, :] <= idx[:, None])).astype(np.float32)
    ins = [x2d, gmix, w_main, wf_pad, bf_row, gq, gk,
           jnp.asarray(_group_mean_matrix(head_dim), _BF16), jnp.asarray(seg, _BF16)]
    full = lambda s: pl.BlockSpec(s, lambda i: (0,) * len(s))
    widths = (aw, aw, aw, V7X_LANES, V7X_LANES, rw, rw)
    return pl.pallas_call(
        functools.partial(_sproj_kernel, aw=aw, rw=rw),
        grid=(1,),
        in_specs=[full(a.shape) for a in ins],
        out_specs=tuple(full((n, w)) for w in widths),
        out_shape=tuple(jax.ShapeDtypeStruct((n, w), _F32) for w in widths),
        compiler_params=_params(1),
        name="proj_sample",
    )(*ins)


def _block_diag_pair(w):
    nb, dd, _ = w.shape
    per = V7X_MXU_DIM // dd
    tiles = []
    for half in range(nb // per):
        tile = jnp.zeros((V7X_MXU_DIM, V7X_MXU_DIM), w.dtype)
        for j in range(per):
            tile = lax.dynamic_update_slice(tile, w[half * per + j], (j * dd, j * dd))
        tiles.append(tile)
    return jnp.stack(tiles).astype(_BF16)


def kernel(x_prompt, x_sample, p_prompt, p_sample, cache_k, cache_v, cache_logf, state_conv, state_h, page_table, g_mix, w_in, b_f, g_q, g_k, w_conv, b_conv, w_a, b_a, w_x, b_x, lam, g_out_attn, g_out_rec, w_out, g_ffn, w_ffn_in, w_ffn_out, w_ple, g_ple, w_ple_gate):
    batch, seq, d_model = x_prompt.shape
    db, t_new, _ = x_sample.shape
    depth, n_heads, head_dim = g_q.shape
    aw = n_heads * head_dim
    rw = lam.shape[1]
    hidden = w_ffn_out.shape[1]
    n_phys, page = cache_k.shape[1], cache_k.shape[2]
    assert n_heads == V7X_SUBLANES and aw == 2 * V7X_MXU_DIM and rw == 2 * V7X_MXU_DIM
    assert seq % ATTN_BLOCK == 0 and seq % PROJ_ROWS == 0

    xp = x_prompt.reshape(batch * seq, d_model)
    xs = x_sample.reshape(db * t_new, d_model)
    outs = [[] for _ in range(10)]
    row = lambda a: a.reshape(1, -1)
    for l in range(depth):
        w_l = w_in[l]
        w_main = jnp.concatenate([w_l[:, :3 * aw], w_l[:, 3 * aw + n_heads:]], axis=1).astype(_BF16)
        w_f = w_l[:, 3 * aw:3 * aw + n_heads]
        wft = jnp.pad(w_f.T, ((0, 16 - n_heads), (0, 0))).astype(_BF16)
        wf_pad = jnp.pad(w_f, ((0, 0), (0, V7X_LANES - n_heads))).astype(_BF16)
        bf_col = b_f[l].reshape(n_heads, 1)
        bf_row = jnp.pad(b_f[l], (0, V7X_LANES - n_heads)).reshape(1, V7X_LANES)
        gmix = row(g_mix[l])
        gq = row(g_q[l]) * (head_dim ** -0.5)
        gk = row(g_k[l])
        rec_w = (w_conv[l], row(b_conv[l]), _block_diag_pair(w_a[l]), row(b_a[l]),
                 _block_diag_pair(w_x[l]), row(b_x[l]), row(lam[l]))
        tail_w = (row(g_out_attn[l]), row(g_out_rec[l]), w_out[l].astype(_BF16), row(g_ffn[l]),
                  w_ffn_in[l][:, :hidden].astype(_BF16), w_ffn_in[l][:, hidden:].astype(_BF16),
                  w_ffn_out[l].astype(_BF16), w_ple[l].astype(_BF16), row(g_ple[l]),
                  w_ple_gate[l].astype(_BF16))

        kt, vt32, lft, qs, kb, vt, augq, augk, rec, h_tiles, x_tiles = _proj_prompt(
            xp, seq, gmix, w_main, wft, bf_col, gq * LOG2_E, gk, rec_w, n_heads, head_dim, rw)
        q_s, k_s, v_s, lf_s, cn_s, xr_s, gr_s = _proj_sample(
            xs, t_new, gmix, w_main, wf_pad, bf_row, gq, gk, n_heads, head_dim, rw)
        lf_s = lf_s[:, :n_heads]
        cache_lft = jnp.swapaxes(cache_logf[l], 1, 2)
        cache_kt = cache_k[l].transpose(0, 2, 3, 1).reshape(n_phys, aw, page)
        cache_vt = cache_v[l].transpose(0, 2, 3, 1).reshape(n_phys, aw, page)
        att, att_s = _attention(qs, augq, kb, augk, vt, batch, seq, q_s, k_s, v_s, cn_s[:, :n_heads],
                                cache_kt, cache_vt, cache_lft, page_table, t_new, n_heads, head_dim)

        xp = _tail(xp, att, rec, p_prompt[l].reshape(batch * seq, -1), tail_w)
        outs[0].append(kt.reshape(batch, n_heads, head_dim, seq).transpose(0, 3, 1, 2))
        outs[1].append(vt32.reshape(batch, n_heads, head_dim, seq).transpose(0, 3, 1, 2))
        outs[2].append(lft.transpose(0, 2, 1))
        outs[3].append(x_tiles.reshape(batch, -1, V7X_SUBLANES, rw)[:, -1, V7X_SUBLANES - (CONV_WIDTH - 1):])
        outs[4].append(h_tiles.reshape(batch, -1, V7X_SUBLANES, rw)[:, -1, -1])

        hist = jnp.concatenate([state_conv[l], jnp.zeros((db, t_new, rw), _F32)], axis=1)
        prevs = tuple(hist[:, CONV_WIDTH - 1 - s:CONV_WIDTH - 1 - s + t_new].reshape(db * t_new, rw)
                      for s in range(1, CONV_WIDTH))
        h0 = jnp.repeat(state_h[l], t_new, axis=0)
        rec_s, h_s = _rglru_sample(xr_s, gr_s, prevs, h0, t_new, rec_w)
        xs = _tail(xs, att_s, rec_s, p_sample[l].reshape(db * t_new, -1), tail_w)
        outs[5].append(k_s.reshape(db, t_new, n_heads, head_dim))
        outs[6].append(v_s.reshape(db, t_new, n_heads, head_dim))
        outs[7].append(lf_s.reshape(db, t_new, n_heads))
        outs[8].append(xr_s.reshape(db, t_new, rw)[:, t_new - (CONV_WIDTH - 1):])
        outs[9].append(h_s.reshape(db, t_new, rw)[:, -1])
    return (xp.reshape(batch, seq, d_model), xs.reshape(db, t_new, d_model),
            *(jnp.stack(o) for o in outs))
```

```python
import functools

import numpy as np
import jax
import jax.numpy as jnp
from jax import lax
from jax.experimental import pallas as pl
from jax.experimental.pallas import tpu as pltpu

RG_C = 8.0
NORM_EPS = 1e-6
LOG2_E = 1.4426950408889634
CONV_WIDTH = 4

V7X_LANES = 128
V7X_SUBLANES = 8
V7X_MXU_DIM = 256
V7X_VMEM_LIMIT_BYTES = 56 * 1024 * 1024

PROJ_ROWS = 512
ATTN_BLOCK = 512
FOX_LOOKAHEAD = 3
FFN_CHUNK = 512

_F32 = jnp.float32
_BF16 = jnp.bfloat16
_NT = (((1,), (1,)), ((), ()))


def _dot(a, b):
    return jnp.dot(a, b, preferred_element_type=_F32)


def _dot_nt(a, b):
    return lax.dot_general(a, b, _NT, preferred_element_type=_F32)


def _rms(x, g):
    return x * lax.rsqrt(jnp.mean(x * x, axis=-1, keepdims=True) + NORM_EPS) * g


def _split3(z):
    hi = z.astype(_BF16).astype(_F32)
    r = z - hi
    mid = r.astype(_BF16).astype(_F32)
    lo = r - mid
    return hi, mid, lo


def _stack3(z, rows):
    hi, mid, lo = _split3(z)
    pad = jnp.zeros((rows - 3 * V7X_SUBLANES, z.shape[1]), _F32)
    return jnp.concatenate([hi, mid, lo, pad], axis=0).astype(_BF16)


def _sum3(r):
    return r[0:8] + r[8:16] + r[16:24]


def _head_norm(z, g, gmat):
    zz = (z * z).astype(_BF16)
    half = gmat.shape[0]
    ms = jnp.concatenate([_dot(zz[:, :half], gmat), _dot(zz[:, half:], gmat)], axis=1)
    return z * lax.rsqrt(ms + NORM_EPS) * g


def _const_spec(shape):
    nd = len(shape)
    return pl.BlockSpec(shape, lambda *_: (0,) * nd, pipeline_mode=pl.Buffered(1))


def _params(n_axes):
    return pltpu.CompilerParams(
        dimension_semantics=("arbitrary",) * n_axes,
        vmem_limit_bytes=V7X_VMEM_LIMIT_BYTES,
    )


def _sigmoid(z):
    return 0.5 * jnp.tanh(0.5 * z) + 0.5


def _rglru_coeffs(xr, shifted, rec_refs):
    wconv_ref, bconv_ref, wa_ref, ba_ref, wx_ref, bx_ref, lam_ref = rec_refs
    w = wconv_ref[...]
    xc = bconv_ref[...] + shifted[2] * w[0:1]
    xc = xc + shifted[1] * w[1:2]
    xc = xc + shifted[0] * w[2:3]
    xc = xc + xr * w[3:4]

    xcb = xc.astype(_BF16)
    half = wa_ref.shape[1]

    def gate(w_ref, b_ref):
        z = jnp.concatenate([_dot(xcb[:, :half], w_ref[0]), _dot(xcb[:, half:], w_ref[1])], axis=1)
        return _sigmoid(z + b_ref[...])

    r_gate = gate(wa_ref, ba_ref)
    i_gate = gate(wx_ref, bx_ref)
    log_a = -RG_C * r_gate * jax.nn.softplus(-lam_ref[...])
    a = jnp.exp(log_a)
    om = -jnp.tanh(log_a) * (a * a + 1.0)
    root = jnp.where(om > 0.0, om * lax.rsqrt(om), 0.0)
    return a, root * (i_gate * xc)


def _scan_step(a, b, a_s, b_s):
    return a * a_s, b + a * b_s


def _rglru_prompt_tile(xr, gr, hc, rec_refs, xbuf_ref, abuf_ref, bbuf_ref, finish):
    tm, rw = xr.shape
    sub = V7X_SUBLANES
    xbuf_ref[sub:sub + tm] = xr
    shifted = [xbuf_ref[sub - k:sub - k + tm] for k in range(1, CONV_WIDTH)]
    a, b = _rglru_coeffs(xr, shifted, rec_refs)
    abuf_ref[0:sub] = jnp.ones((sub, rw), _F32)
    bbuf_ref[0:sub] = jnp.zeros((sub, rw), _F32)
    d = 1
    while d < tm:
        if d % sub == 0:
            a_new, b_new = _scan_step(a[d:], b[d:], a[:tm - d], b[:tm - d])
            a = jnp.concatenate([a[:d], a_new], axis=0)
            b = jnp.concatenate([b[:d], b_new], axis=0)
        else:
            abuf_ref[sub:sub + tm] = a
            bbuf_ref[sub:sub + tm] = b
            a, b = _scan_step(a, b, abuf_ref[sub - d:sub - d + tm], bbuf_ref[sub - d:sub - d + tm])
        d *= 2
    h = a * hc + b
    finish(h * jax.nn.gelu(gr), h)


def _proj_kernel(x_ref, gmix_ref, w_ref, wft_ref, bf_ref, gq_ref, gk_ref, gmat_ref, cum_ref,
                 aq_ref, ak_ref, cq_ref, ck_ref, *refs, tiles_per_seq, aw, rw):
    rec_refs = refs[:7]
    (k_out, v_out, lft_out, qs_out, kb_out, vt_out, augq_out, augk_out, rec_out, h_out, xtail_out,
     carry_ref, hc_ref, xbuf_ref, abuf_ref, bbuf_ref) = refs[7:]
    i = pl.program_id(0)
    tm = x_ref.shape[0]
    sub = V7X_SUBLANES
    seq_start = i % tiles_per_seq == 0

    @pl.when(seq_start)
    def _():
        carry_ref[...] = jnp.zeros_like(carry_ref)
        hc_ref[...] = jnp.zeros_like(hc_ref)
        xbuf_ref[0:sub] = jnp.zeros((sub, rw), _F32)

    @pl.when(jnp.logical_not(seq_start))
    def _():
        xbuf_ref[0:sub] = xbuf_ref[tm:tm + sub]

    hn = _rms(x_ref[...], gmix_ref[...]).astype(_BF16)
    w_rec = 3 * aw
    proj_rec = _dot(hn, w_ref[:, w_rec:w_rec + 2 * rw])
    xr = proj_rec[:, 0:rw]
    xtail_out[...] = xr[tm - sub:tm]

    def finish(y, h):
        hc_ref[...] = h[tm - 1:tm]
        rec_out[...] = y
        h_out[...] = h[tm - sub:tm]

    _rglru_prompt_tile(xr, proj_rec[:, rw:2 * rw], hc_ref[...], rec_refs, xbuf_ref, abuf_ref, bbuf_ref, finish)

    proj = _dot(hn, w_ref[:, 0:w_rec])
    gmat = gmat_ref[...]
    qn = _head_norm(proj[:, 0:aw], gq_ref[...], gmat)
    kn = _head_norm(proj[:, aw:2 * aw], gk_ref[...], gmat)
    vt = proj[:, 2 * aw:3 * aw].T
    k_out[...] = kn.T
    v_out[...] = vt
    qs_out[...] = qn.astype(_BF16)
    kb_out[...] = kn.astype(_BF16)
    vt_out[...] = vt.astype(_BF16)

    ft = _dot_nt(wft_ref[...], hn)
    lft = jax.nn.log_sigmoid(ft[0:8] + bf_ref[...])
    lft_out[...] = lft

    p3 = _stack3(lft, V7X_LANES)
    blk = cum_ref.shape[0]
    carry = carry_ref[...]
    cums = []
    for c in range(tm // blk):
        r = _dot(p3[:, c * blk:(c + 1) * blk], cum_ref[...])
        cums.append(_sum3(r[:, :blk]) + carry)
        carry = carry + _sum3(r[:, blk:])
    carry_ref[...] = carry
    cum = jnp.concatenate(cums, axis=1)

    c3 = _stack3(cum * LOG2_E, V7X_LANES)
    qat = _dot(aq_ref[...], c3) + cq_ref[...]
    kat = _dot(ak_ref[...], c3) + ck_ref[...]
    augq_out[...] = qat.T.astype(_BF16)
    augk_out[...] = kat.T.astype(_BF16)


def _bias_placement(n_heads):
    aq = np.zeros((V7X_LANES, V7X_LANES), np.float32)
    ak = np.zeros((V7X_LANES, V7X_LANES), np.float32)
    cq = np.zeros((V7X_LANES, 1), np.float32)
    ck = np.zeros((V7X_LANES, 1), np.float32)
    for h in range(n_heads):
        for j in range(3):
            aq[h * 8 + j, j * 8 + h] = 1.0
            cq[h * 8 + 3 + j, 0] = 1.0
            ak[h * 8 + 3 + j, j * 8 + h] = -1.0
            ck[h * 8 + j, 0] = 1.0
    return aq, ak, cq, ck


def _group_mean_matrix(head_dim):
    idx = np.arange(V7X_MXU_DIM) // head_dim
    return (idx[:, None] == idx[None, :]).astype(np.float32) / head_dim


def _prefix_matrix():
    idx = np.arange(V7X_MXU_DIM)
    incl = (idx[:, None] <= idx[None, :]).astype(np.float32)
    return np.concatenate([incl, np.ones_like(incl)], axis=1)


def _proj_prompt(x2d, seq, gmix, w_main, wft, bf_col, gq, gk, rec_w, n_heads, head_dim, rw):
    n, d = x2d.shape
    tm = PROJ_ROWS
    aw = n_heads * head_dim
    aq, ak, cq, ck = _bias_placement(n_heads)
    consts = [
        gmix, w_main, wft, bf_col, gq, gk,
        jnp.asarray(_group_mean_matrix(head_dim), _BF16),
        jnp.asarray(_prefix_matrix(), _BF16),
        jnp.asarray(aq, _BF16), jnp.asarray(ak, _BF16), jnp.asarray(cq), jnp.asarray(ck),
        *rec_w,
    ]
    tps = seq // tm
    batch = n // seq
    row = lambda w: pl.BlockSpec((tm, w), lambda i: (i, 0))
    seq_t = lambda r: pl.BlockSpec((None, r, tm), lambda i: (i // tps, 0, i % tps))
    out_shape = (
        jax.ShapeDtypeStruct((batch, aw, seq), _F32),
        jax.ShapeDtypeStruct((batch, aw, seq), _F32),
        jax.ShapeDtypeStruct((batch, V7X_SUBLANES, seq), _F32),
        jax.ShapeDtypeStruct((n, aw), _BF16),
        jax.ShapeDtypeStruct((n, aw), _BF16),
        jax.ShapeDtypeStruct((batch, aw, seq), _BF16),
        jax.ShapeDtypeStruct((n, V7X_LANES), _BF16),
        jax.ShapeDtypeStruct((n, V7X_LANES), _BF16),
        jax.ShapeDtypeStruct((n, rw), _F32),
        jax.ShapeDtypeStruct((n // tm * V7X_SUBLANES, rw), _F32),
        jax.ShapeDtypeStruct((n // tm * V7X_SUBLANES, rw), _F32),
    )
    tail8 = pl.BlockSpec((V7X_SUBLANES, rw), lambda i: (i, 0))
    out_specs = (
        seq_t(aw), seq_t(aw), seq_t(V7X_SUBLANES), row(aw), row(aw), seq_t(aw),
        row(V7X_LANES), row(V7X_LANES), row(rw), tail8, tail8,
    )
    return pl.pallas_call(
        functools.partial(_proj_kernel, tiles_per_seq=tps, aw=aw, rw=rw),
        grid=(n // tm,),
        in_specs=[row(d)] + [_const_spec(c.shape) for c in consts],
        out_specs=out_specs,
        out_shape=out_shape,
        scratch_shapes=[pltpu.VMEM((V7X_SUBLANES, V7X_MXU_DIM), _F32), pltpu.VMEM((1, rw), _F32)]
        + [pltpu.VMEM((V7X_SUBLANES + tm, rw), _F32)] * 3,
        compiler_params=_params(1),
        name="proj_prompt",
    )(x2d, *consts)


class _PagedSide:
    def __init__(self, small, pages, pps, page, uo_ref, o_ref, scratch, t_new, head_dim, n_pages):
        self.q_ref, self.kn_ref, self.vn_ref, self.cnq_ref, self.cnk_ref = small
        self.k_page, self.v_page, self.lf_page = pages
        self.pps, self.page, self.uo_ref, self.o_ref = pps, page, uo_ref, o_ref
        self.qbd_ref, self.m_ref, self.l_ref, self.acc_ref, self.carry_ref = scratch
        self.t_new, self.head_dim, self.n_pages = t_new, head_dim, n_pages
        self.nh = V7X_SUBLANES
        self.aw = self.q_ref.shape[1]

    def _head_mask(self):
        feat = lax.broadcasted_iota(jnp.int32, (self.nh, self.aw), 1)
        head = lax.broadcasted_iota(jnp.int32, (self.nh, self.aw), 0)
        return (feat >= head * self.head_dim) & (feat < (head + 1) * self.head_dim)

    def init(self):
        nh, aw, page, t_new = self.nh, self.aw, self.page, self.t_new
        rows = t_new * nh
        q = self.q_ref[...]
        head_mask = self._head_mask()
        qbd = jnp.concatenate(
            [jnp.where(head_mask, jnp.broadcast_to(q[t:t + 1], (nh, aw)), 0.0) for t in range(t_new)],
            axis=0).astype(_BF16)
        self.qbd_ref[...] = qbd
        pad = jnp.zeros((page - self.kn_ref.shape[0], aw), _F32)
        kn = jnp.concatenate([self.kn_ref[...], pad], axis=0).astype(_BF16)
        vn = jnp.concatenate([self.vn_ref[...], pad], axis=0).astype(_BF16)
        s = _dot_nt(qbd, kn) + self.cnq_ref[...] - self.cnk_ref[...]
        row = lax.broadcasted_iota(jnp.int32, (rows, page), 0)
        t_col = lax.broadcasted_iota(jnp.int32, (rows, page), 1)
        s = jnp.where(t_col * nh <= row, s, -jnp.inf)
        m = jnp.max(s, axis=1, keepdims=True)
        p = jnp.exp(s - m)
        self.m_ref[...] = m
        self.l_ref[...] = jnp.sum(p, axis=1, keepdims=True)
        self.acc_ref[...] = _dot(p.astype(_BF16), vn)
        self.carry_ref[...] = jnp.zeros_like(self.carry_ref)

    def scores(self, step):
        pps, page = self.pps, self.page
        qbd = self.qbd_ref[...]
        carry = self.carry_ref[...]
        scores = []
        for i in range(pps):
            valid = step * pps + i < self.n_pages
            s = _dot(qbd, self.k_page(i).astype(_BF16))
            lf = jnp.where(valid, self.lf_page(i), 0.0)
            r = _dot(_stack3(lf, 4 * V7X_SUBLANES), self.uo_ref[...])
            bias = _sum3(r[:, :page]) + carry
            carry = carry + _sum3(r[:, page:])
            s = s + jnp.concatenate([bias] * self.t_new, axis=0)
            scores.append(jnp.where(valid, s, -jnp.inf))
        self.carry_ref[...] = carry
        s_all = jnp.concatenate(scores, axis=1) + self.cnq_ref[...]
        m_old = self.m_ref[...]
        m_new = jnp.maximum(m_old, jnp.max(s_all, axis=1, keepdims=True))
        alpha = jnp.exp(m_old - m_new)
        p = jnp.exp(s_all - m_new)
        self.l_ref[...] = alpha * self.l_ref[...] + jnp.sum(p, axis=1, keepdims=True)
        self.m_ref[...] = m_new
        return p.astype(_BF16), alpha

    def values(self, p, alpha):
        vt_all = jnp.concatenate([self.v_page(i).astype(_BF16) for i in range(self.pps)], axis=1)
        self.acc_ref[...] = alpha * self.acc_ref[...] + _dot_nt(p, vt_all)

    def finish(self):
        nh = self.nh
        a = self.acc_ref[...] / self.l_ref[...]
        head_mask = self._head_mask()
        for t in range(self.t_new):
            self.o_ref[t:t + 1, :] = jnp.sum(jnp.where(head_mask, a[t * nh:(t + 1) * nh], 0.0),
                                             axis=0, keepdims=True)


def _fox_kernel(qi_tab, ki_tab, pt_ref, q_ref, aq_ref, k_ref, ak_ref, vt_ref, *refs,
                n_heads, head_dim, n_samp, pps, t_new, n_pages):
    n_small = 5
    small = refs[:n_small * n_samp]
    rest = refs[n_small * n_samp:]
    cache_refs = rest[0:3]
    uo_ref, o_ref = rest[3], rest[4]
    so_refs = rest[5:5 + n_samp]
    qf_ref, m_ref, acc_ref = rest[5 + n_samp:8 + n_samp]
    s_scratch = rest[8 + n_samp:8 + 6 * n_samp]
    page_bufs = rest[8 + 6 * n_samp:11 + 6 * n_samp]
    sem = rest[11 + 6 * n_samp]

    batch_i = pl.program_id(0)
    step = pl.program_id(1)
    n_steps = pl.num_programs(1)
    g = batch_i * n_steps + step
    slot = g % 2
    page = page_bufs[0].shape[3]

    def page_copies(b2, s2, dst_slot, lookup):
        copies = []
        for u in range(n_samp):
            for i in range(pps):
                pid = 0
                if lookup:
                    pid = pt_ref[b2 * n_samp + u, jnp.maximum(n_pages - 1 - (s2 * pps + i), 0)]
                for c, (cache, buf) in enumerate(zip(cache_refs, page_bufs)):
                    copies.append(pltpu.make_async_copy(cache.at[pid], buf.at[dst_slot, u * pps + i],
                                                        sem.at[dst_slot, c]))
        return copies

    @pl.when(g == 0)
    def _first_pages():
        for c in page_copies(0, 0, 0, True):
            c.start()

    @pl.when(g + 1 < pl.num_programs(0) * n_steps)
    def _next_pages():
        wrap = step + 1 == n_steps
        for c in page_copies(jnp.where(wrap, batch_i + 1, batch_i), jnp.where(wrap, 0, step + 1),
                             1 - slot, True):
            c.start()

    for c in page_copies(batch_i, step, slot, False):
        c.wait()

    def reader(buf, u):
        return lambda i: buf[slot, u * pps + i]

    paged = [
        _PagedSide(small[u * n_small:(u + 1) * n_small], [reader(buf, u) for buf in page_bufs], pps, page,
                   uo_ref, so_refs[u], s_scratch[u * 5:(u + 1) * 5], t_new, head_dim, n_pages)
        for u in range(n_samp)]

    qi = qi_tab[step]
    ki = ki_tab[step]
    bq = q_ref.shape[0]
    bk = k_ref.shape[0]
    pair = 2 * head_dim

    @pl.when(step == 0)
    def _paged_init():
        for side in paged:
            side.init()

    @pl.when(ki == 0)
    def _init():
        lane = lax.broadcasted_iota(jnp.int32, (bq, V7X_LANES), 1)
        a = aq_ref[...]
        zero = jnp.zeros_like(a)
        for h in range(n_heads):
            q2 = q_ref[:, (h // 2) * pair:(h // 2 + 1) * pair]
            lo = head_dim * (h % 2)
            qh = jnp.where((lane >= lo) & (lane < lo + head_dim), q2, zero)
            ah = jnp.where((lane >= h * 8) & (lane < h * 8 + 8), a, zero)
            qf_ref[h] = jnp.concatenate([qh, ah], axis=1)
        m_ref[...] = jnp.full_like(m_ref, -jnp.inf)
        acc_ref[...] = jnp.zeros_like(acc_ref)

    def step_body(masked):
        ak = ak_ref[...]
        ones_blk = (lax.broadcasted_iota(jnp.int32, (16, bk), 0) == 0).astype(_BF16)
        if masked:
            causal = (lax.broadcasted_iota(jnp.int32, (bk, bq), 0)
                      <= lax.broadcasted_iota(jnp.int32, (bk, bq), 1))
        def scores(h):
            hp = h // 2
            kf = jnp.concatenate([k_ref[:, hp * pair:(hp + 1) * pair], ak], axis=1)
            return _dot_nt(kf, qf_ref[h])

        paged_p = [side.scores(step) for side in paged]
        ahead = [scores(h) for h in range(FOX_LOOKAHEAD)]
        for side, (p_s, alpha_s) in zip(paged, paged_p):
            side.values(p_s, alpha_s)
        for h in range(n_heads):
            s = ahead.pop(0)
            if h + FOX_LOOKAHEAD < n_heads:
                ahead.append(scores(h + FOX_LOOKAHEAD))
            if masked:
                s = jnp.where(causal, s, -jnp.inf)
            m_old = m_ref[h]
            m_new = jnp.maximum(m_old, jnp.max(s, axis=0, keepdims=True))
            alpha = jnp.exp2(m_old - m_new)
            p = jnp.exp2(s - m_new).astype(_BF16)
            lhs = jnp.concatenate([vt_ref[h * head_dim:(h + 1) * head_dim, :], ones_blk], axis=0)
            acc_ref[h] = alpha * acc_ref[h] + _dot(lhs, p)
            m_ref[h] = m_new

    @pl.when(ki < qi)
    def _off_diagonal():
        step_body(False)

    @pl.when(ki == qi)
    def _diagonal():
        step_body(True)
        for hp in range(n_heads // 2):
            outs = []
            for h in (2 * hp, 2 * hp + 1):
                a = acc_ref[h]
                outs.append(a[0:head_dim] / a[head_dim:head_dim + 1])
            o_ref[:, hp * pair:(hp + 1) * pair] = jnp.concatenate(outs, axis=0).T

    @pl.when(step == pl.num_programs(1) - 1)
    def _paged_finish():
        for side in paged:
            side.finish()


def _suffix_matrix(page):
    idx = np.arange(page)
    strict = (idx[:, None] > idx[None, :]).astype(np.float32)
    return np.concatenate([strict, np.ones_like(strict)], axis=1)


def _attention(qs, augq, kb, augk, vt, batch, seq, q_s, k_new, v_new, cn, cache_kt, cache_vt, cache_lft,
               page_table, t_new, n_heads, head_dim):
    aw = n_heads * head_dim
    bq = bk = ATTN_BLOCK
    nq = seq // bq
    qs3 = qs.reshape(batch, seq, aw)
    kb3 = kb.reshape(batch, seq, aw)
    aq3 = augq.reshape(batch, seq, V7X_LANES)
    ak3 = augk.reshape(batch, seq, V7X_LANES)
    pairs = [(qi, ki) for qi in range(nq) for ki in range(qi + 1)]
    qi_tab = jnp.asarray([p[0] for p in pairs], jnp.int32)
    ki_tab = jnp.asarray([p[1] for p in pairs], jnp.int32)
    n_steps = len(pairs)

    db, n_pages = page_table.shape
    page = cache_kt.shape[2]
    assert db % batch == 0, "sample batches must split evenly over the prompt batches"
    n_samp = db // batch
    pps = -(-n_pages // n_steps)
    rows = t_new * n_heads
    pad_t = V7X_SUBLANES - t_new
    q3 = q_s.reshape(db, t_new, aw)
    kn = jnp.pad(k_new.reshape(db, t_new, aw), ((0, 0), (0, pad_t), (0, 0)))
    vn = jnp.pad(v_new.reshape(db, t_new, aw), ((0, 0), (0, pad_t), (0, 0)))
    cn3 = cn.reshape(db, t_new, n_heads)
    cnq = cn3.reshape(db, rows, 1)
    cnk = jnp.tile(jnp.swapaxes(cn3, 1, 2), (1, t_new, 1))
    cnk = jnp.pad(cnk, ((0, 0), (0, 0), (0, page - t_new)))
    uo = jnp.asarray(_suffix_matrix(page), _BF16)

    qmap = lambda b, s, qt, kt, pt: (b, qt[s], 0)
    kmap = lambda b, s, qt, kt, pt: (b, kt[s], 0)

    def samp_spec(shape, u):
        return pl.BlockSpec((None,) + shape, lambda b, s, qt, kt, pt: (b * n_samp + u, 0, 0))

    small_specs, small_args = [], []
    for u in range(n_samp):
        small_specs += [samp_spec((t_new, aw), u), samp_spec((V7X_SUBLANES, aw), u),
                        samp_spec((V7X_SUBLANES, aw), u), samp_spec((rows, 1), u), samp_spec((rows, page), u)]
        small_args += [q3, kn, vn, cnq, cnk]
    caches = (cache_kt, cache_vt, cache_lft)

    grid_spec = pltpu.PrefetchScalarGridSpec(
        num_scalar_prefetch=3,
        grid=(batch, n_steps),
        in_specs=[
            pl.BlockSpec((None, bq, aw), qmap),
            pl.BlockSpec((None, bq, V7X_LANES), qmap),
            pl.BlockSpec((None, bk, aw), kmap),
            pl.BlockSpec((None, bk, V7X_LANES), kmap),
            pl.BlockSpec((None, aw, bk), lambda b, s, qt, kt, pt: (b, 0, kt[s])),
        ] + small_specs + [pl.BlockSpec(memory_space=pl.ANY)] * len(caches)
        + [pl.BlockSpec(uo.shape, lambda b, s, qt, kt, pt: (0, 0))],
        out_specs=[pl.BlockSpec((None, bq, aw), qmap)]
        + [pl.BlockSpec((None, t_new, aw), lambda b, s, qt, kt, pt: (b, 0, 0))] * n_samp,
        scratch_shapes=[
            pltpu.VMEM((n_heads, bq, 2 * V7X_LANES), _BF16),
            pltpu.VMEM((n_heads, 1, bq), _F32),
            pltpu.VMEM((n_heads, head_dim + 16, bq), _F32),
        ] + [
            pltpu.VMEM((rows, aw), _BF16),
            pltpu.VMEM((rows, 1), _F32),
            pltpu.VMEM((rows, 1), _F32),
            pltpu.VMEM((rows, aw), _F32),
            pltpu.VMEM((V7X_SUBLANES, page), _F32),
        ] * n_samp + [
            pltpu.VMEM((2, n_samp * pps) + c.shape[1:], c.dtype) for c in caches
        ] + [pltpu.SemaphoreType.DMA((2, len(caches)))],
    )
    outs = pl.pallas_call(
        functools.partial(_fox_kernel, n_heads=n_heads, head_dim=head_dim, n_samp=n_samp, pps=pps,
                          t_new=t_new, n_pages=n_pages),
        grid_spec=grid_spec,
        out_shape=[jax.ShapeDtypeStruct((batch, seq, aw), _F32)]
        + [jax.ShapeDtypeStruct((batch, t_new, aw), _F32)] * n_samp,
        compiler_params=_params(2),
        name="attention",
    )(qi_tab, ki_tab, page_table, qs3, aq3, kb3, ak3, vt, *small_args, *caches, uo)
    att = outs[0].reshape(batch * seq, aw)
    att_s = jnp.stack(outs[1:], axis=1)
    return att, att_s.reshape(db * t_new, aw)


def _rglru_sample_kernel(xr_ref, gr_ref, p1_ref, p2_ref, p3_ref, h0_ref, *refs, seg_len):
    rec_refs = refs[:7]
    y_ref, h_ref = refs[7:]
    tm, rw = xr_ref.shape
    xr = xr_ref[...]
    t = lax.broadcasted_iota(jnp.int32, (tm, rw), 0) & (seg_len - 1)
    prev = (p1_ref, p2_ref, p3_ref)
    shifted = [jnp.where(t >= k, pltpu.roll(xr, k, 0), prev[k - 1][...]) for k in range(1, CONV_WIDTH)]
    a, b = _rglru_coeffs(xr, shifted, rec_refs)
    d = 1
    while d < seg_len:
        a_new, b_new = _scan_step(a, b, pltpu.roll(a, d, 0), pltpu.roll(b, d, 0))
        a = jnp.where(t >= d, a_new, a)
        b = jnp.where(t >= d, b_new, b)
        d *= 2
    h = a * h0_ref[...] + b
    y_ref[...] = h * jax.nn.gelu(gr_ref[...])
    h_ref[...] = h


def _rglru_sample(xr, gr, prevs, h0, seg_len, rec_w):
    n, rw = xr.shape
    assert seg_len & (seg_len - 1) == 0 and n % seg_len == 0
    full = pl.BlockSpec((n, rw), lambda i: (0, 0))
    return pl.pallas_call(
        functools.partial(_rglru_sample_kernel, seg_len=seg_len),
        grid=(1,),
        in_specs=[full] * 6 + [_const_spec(c.shape) for c in rec_w],
        out_specs=(full, full),
        out_shape=(jax.ShapeDtypeStruct((n, rw), _F32), jax.ShapeDtypeStruct((n, rw), _F32)),
        compiler_params=_params(1),
        name="rglru_sample",
    )(xr, gr, *prevs, h0, *rec_w)


def _tail_math(x, att, rec, p, goa_ref, gor_ref, wout_ref, gffn_ref, wg_ref, wu_ref, wo_ref, wple_ref,
               gple_ref, wpg_ref):
    aw = att.shape[1]
    an = _rms(att, goa_ref[...]).astype(_BF16)
    rn = _rms(rec, gor_ref[...]).astype(_BF16)
    x1 = x + _dot(an, wout_ref[0:aw, :]) + _dot(rn, wout_ref[aw:, :])
    hn = _rms(x1, gffn_ref[...]).astype(_BF16)
    hidden = wg_ref.shape[1]
    ffn = None
    for c0 in range(0, hidden, FFN_CHUNK):
        cw = min(FFN_CHUNK, hidden - c0)
        g = _dot(hn, wg_ref[:, c0:c0 + cw])
        u = _dot(hn, wu_ref[:, c0:c0 + cw])
        part = _dot((jax.nn.silu(g) * u).astype(_BF16), wo_ref[c0:c0 + cw, :])
        ffn = part if ffn is None else ffn + part
    x2 = x1 + ffn
    gate = jax.nn.sigmoid(_dot(_rms(x2, gple_ref[...]).astype(_BF16), wpg_ref[...]))
    return x2 + _dot(p.astype(_BF16), wple_ref[...]) * gate


def _tail_kernel(x_ref, att_ref, rec_ref, p_ref, *refs):
    o_ref = refs[-1]
    o_ref[...] = _tail_math(x_ref[...], att_ref[...], rec_ref[...], p_ref[...], *refs[:-1])


def _tail(x2d, att, rec, p2d, tail_w):
    n, d = x2d.shape
    tm = min(PROJ_ROWS, n)
    row = lambda w: pl.BlockSpec((tm, w), lambda i: (i, 0))
    return pl.pallas_call(
        _tail_kernel,
        grid=(n // tm,),
        in_specs=[row(d), row(att.shape[1]), row(rec.shape[1]), row(p2d.shape[1])]
        + [_const_spec(c.shape) for c in tail_w],
        out_specs=row(d),
        out_shape=jax.ShapeDtypeStruct((n, d), _F32),
        compiler_params=_params(1),
        name="tail",
    )(x2d, att, rec, p2d, *tail_w)


def _sproj_kernel(x_ref, gmix_ref, w_ref, wf_ref, bf_ref, gq_ref, gk_ref, gmat_ref, seg_ref,
                  q_out, k_out, v_out, lf_out, cn_out, xr_out, gr_out, *, aw, rw):
    hn = _rms(x_ref[...], gmix_ref[...]).astype(_BF16)
    proj = _dot(hn, w_ref[...])
    gmat = gmat_ref[...]
    q_out[...] = _head_norm(proj[:, 0:aw], gq_ref[...], gmat)
    k_out[...] = _head_norm(proj[:, aw:2 * aw], gk_ref[...], gmat)
    v_out[...] = proj[:, 2 * aw:3 * aw]
    xr_out[...] = proj[:, 3 * aw:3 * aw + rw]
    gr_out[...] = proj[:, 3 * aw + rw:3 * aw + 2 * rw]
    lf = jax.nn.log_sigmoid(_dot(hn, wf_ref[...]) + bf_ref[...])
    lf_out[...] = lf
    hi, mid, lo = _split3(lf)
    seg = seg_ref[...]
    cn_out[...] = _dot(seg, hi.astype(_BF16)) + _dot(seg, mid.astype(_BF16)) + _dot(seg, lo.astype(_BF16))


def _proj_sample(x2d, t_new, gmix, w_main, wf_pad, bf_row, gq, gk, n_heads, head_dim, rw):
    n, d = x2d.shape
    aw = n_heads * head_dim
    idx = np.arange(n)
    seg = ((idx[:, None] // t_new == idx[None, :] // t_new) & (idx[None, :] <= idx[:, None])).astype(np.float32)
    ins = [x2d, gmix, w_main, wf_pad, bf_row, gq, gk,
           jnp.asarray(_group_mean_matrix(head_dim), _BF16), jnp.asarray(seg, _BF16)]
    full = lambda s: pl.BlockSpec(s, lambda i: (0,) * len(s))
    widths = (aw, aw, aw, V7X_LANES, V7X_LANES, rw, rw)
    return pl.pallas_call(
        functools.partial(_sproj_kernel, aw=aw, rw=rw),
        grid=(1,),
        in_specs=[full(a.shape) for a in ins],
        out_specs=tuple(full((n, w)) for w in widths),
        out_shape=tuple(jax.ShapeDtypeStruct((n, w), _F32) for w in widths),
        compiler_params=_params(1),
        name="proj_sample",
    )(*ins)


def _block_diag_pair(w):
    nb, dd, _ = w.shape
    per = V7X_MXU_DIM // dd
    tiles = []
    for half in range(nb // per):
        tile = jnp.zeros((V7X_MXU_DIM, V7X_MXU_DIM), w.dtype)
        for j in range(per):
            tile = lax.dynamic_update_slice(tile, w[half * per + j], (j * dd, j * dd))
        tiles.append(tile)
    return jnp.stack(tiles).astype(_BF16)


def kernel(x_prompt, x_sample, p_prompt, p_sample, cache_k, cache_v, cache_logf, state_conv, state_h, page_table, g_mix, w_in, b_f, g_q, g_k, w_conv, b_conv, w_a, b_a, w_x, b_x, lam, g_out_attn, g_out_rec, w_out, g_ffn, w_ffn_in, w_ffn_out, w_ple, g_ple, w_ple_gate):
    batch, seq, d_model = x_prompt.shape
    db, t_new, _ = x_sample.shape
    depth, n_heads, head_dim = g_q.shape
    aw = n_heads * head_dim
    rw = lam.shape[1]
    hidden = w_ffn_out.shape[1]
    n_phys, page = cache_k.shape[1], cache_k.shape[2]
    assert n_heads == V7X_SUBLANES and aw == 2 * V7X_MXU_DIM and rw == 2 * V7X_MXU_DIM
    assert seq % ATTN_BLOCK == 0 and seq % PROJ_ROWS == 0

    xp = x_prompt.reshape(batch * seq, d_model)
    xs = x_sample.reshape(db * t_new, d_model)
    outs = [[] for _ in range(10)]
    row = lambda a: a.reshape(1, -1)
    for l in range(depth):
        w_l = w_in[l]
        w_main = jnp.concatenate([w_l[:, :3 * aw], w_l[:, 3 * aw + n_heads:]], axis=1).astype(_BF16)
        w_f = w_l[:, 3 * aw:3 * aw + n_heads]
        wft = jnp.pad(w_f.T, ((0, 16 - n_heads), (0, 0))).astype(_BF16)
        wf_pad = jnp.pad(w_f, ((0, 0), (0, V7X_LANES - n_heads))).astype(_BF16)
        bf_col = b_f[l].reshape(n_heads, 1)
        bf_row = jnp.pad(b_f[l], (0, V7X_LANES - n_heads)).reshape(1, V7X_LANES)
        gmix = row(g_mix[l])
        gq = row(g_q[l]) * (head_dim ** -0.5)
        gk = row(g_k[l])
        rec_w = (w_conv[l], row(b_conv[l]), _block_diag_pair(w_a[l]), row(b_a[l]),
                 _block_diag_pair(w_x[l]), row(b_x[l]), row(lam[l]))
        tail_w = (row(g_out_attn[l]), row(g_out_rec[l]), w_out[l].astype(_BF16), row(g_ffn[l]),
                  w_ffn_in[l][:, :hidden].astype(_BF16), w_ffn_in[l][:, hidden:].astype(_BF16),
                  w_ffn_out[l].astype(_BF16), w_ple[l].astype(_BF16), row(g_ple[l]),
                  w_ple_gate[l].astype(_BF16))

        kt, vt32, lft, qs, kb, vt, augq, augk, rec, h_tiles, x_tiles = _proj_prompt(
            xp, seq, gmix, w_main, wft, bf_col, gq * LOG2_E, gk, rec_w, n_heads, head_dim, rw)
        q_s, k_s, v_s, lf_s, cn_s, xr_s, gr_s = _proj_sample(
            xs, t_new, gmix, w_main, wf_pad, bf_row, gq, gk, n_heads, head_dim, rw)
        lf_s = lf_s[:, :n_heads]
        cache_lft = jnp.swapaxes(cache_logf[l], 1, 2)
        cache_kt = cache_k[l].transpose(0, 2, 3, 1).reshape(n_phys, aw, page)
        cache_vt = cache_v[l].transpose(0, 2, 3, 1).reshape(n_phys, aw, page)
        att, att_s = _attention(qs, augq, kb, augk, vt, batch, seq, q_s, k_s, v_s, cn_s[:, :n_heads],
                                cache_kt, cache_vt, cache_lft, page_table, t_new, n_heads, head_dim)

        xp = _tail(xp, att, rec, p_prompt[l].reshape(batch * seq, -1), tail_w)
        outs[0].append(kt.reshape(batch, n_heads, head_dim, seq).transpose(0, 3, 1, 2))
        outs[1].append(vt32.reshape(batch, n_heads, head_dim, seq).transpose(0, 3, 1, 2))
        outs[2].append(lft.transpose(0, 2, 1))
        outs[3].append(x_tiles.reshape(batch, -1, V7X_SUBLANES, rw)[:, -1, V7X_SUBLANES - (CONV_WIDTH - 1):])
        outs[4].append(h_tiles.reshape(batch, -1, V7X_SUBLANES, rw)[:, -1, -1])

        hist = jnp.concatenate([state_conv[l], jnp.zeros((db, t_new, rw), _F32)], axis=1)
        prevs = tuple(hist[:, CONV_WIDTH - 1 - s:CONV_WIDTH - 1 - s + t_new].reshape(db * t_new, rw)
                      for s in range(1, CONV_WIDTH))
        h0 = jnp.repeat(state_h[l], t_new, axis=0)
        rec_s, h_s = _rglru_sample(xr_s, gr_s, prevs, h0, t_new, rec_w)
        xs = _tail(xs, att_s, rec_s, p_sample[l].reshape(db * t_new, -1), tail_w)
        outs[5].append(k_s.reshape(db, t_new, n_heads, head_dim))
        outs[6].append(v_s.reshape(db, t_new, n_heads, head_dim))
        outs[7].append(lf_s.reshape(db, t_new, n_heads))
        outs[8].append(xr_s.reshape(db, t_new, rw)[:, t_new - (CONV_WIDTH - 1):])
        outs[9].append(h_s.reshape(db, t_new, rw)[:, -1])
    return (xp.reshape(batch, seq, d_model), xs.reshape(db, t_new, d_model),
            *(jnp.stack(o) for o in outs))
```

```python
import functools

import numpy as np
import jax
import jax.numpy as jnp
from jax import lax
from jax.experimental import pallas as pl
from jax.experimental.pallas import tpu as pltpu

RG_C = 8.0
NORM_EPS = 1e-6
LOG2_E = 1.4426950408889634
CONV_WIDTH = 4

V7X_LANES = 128
V7X_SUBLANES = 8
V7X_MXU_DIM = 256
V7X_VMEM_LIMIT_BYTES = 56 * 1024 * 1024

PROJ_ROWS = 512
ATTN_BLOCK = 512
FOX_LOOKAHEAD = 3
FFN_CHUNK = 256

_F32 = jnp.float32
_BF16 = jnp.bfloat16
_NT = (((1,), (1,)), ((), ()))


def _dot(a, b):
    return jnp.dot(a, b, preferred_element_type=_F32)


def _dot_nt(a, b):
    return lax.dot_general(a, b, _NT, preferred_element_type=_F32)


def _rms(x, g):
    return x * lax.rsqrt(jnp.mean(x * x, axis=-1, keepdims=True) + NORM_EPS) * g


def _split3(z):
    hi = z.astype(_BF16).astype(_F32)
    r = z - hi
    mid = r.astype(_BF16).astype(_F32)
    lo = r - mid
    return hi, mid, lo


def _stack3(z, rows):
    hi, mid, lo = _split3(z)
    pad = jnp.zeros((rows - 3 * V7X_SUBLANES, z.shape[1]), _F32)
    return jnp.concatenate([hi, mid, lo, pad], axis=0).astype(_BF16)


def _sum3(r):
    return r[0:8] + r[8:16] + r[16:24]


def _head_norm(z, g, gmat):
    zz = (z * z).astype(_BF16)
    half = gmat.shape[0]
    ms = jnp.concatenate([_dot(zz[:, :half], gmat), _dot(zz[:, half:], gmat)], axis=1)
    return z * lax.rsqrt(ms + NORM_EPS) * g


def _const_spec(shape):
    nd = len(shape)
    return pl.BlockSpec(shape, lambda *_: (0,) * nd, pipeline_mode=pl.Buffered(1))


def _params(n_axes):
    return pltpu.CompilerParams(
        dimension_semantics=("arbitrary",) * n_axes,
        vmem_limit_bytes=V7X_VMEM_LIMIT_BYTES,
    )


def _sigmoid(z):
    return 0.5 * jnp.tanh(0.5 * z) + 0.5


def _rglru_coeffs(xr, shifted, rec_refs):
    wconv_ref, bconv_ref, wa_ref, ba_ref, wx_ref, bx_ref, lam_ref = rec_refs
    w = wconv_ref[...]
    xc = bconv_ref[...] + shifted[2] * w[0:1]
    xc = xc + shifted[1] * w[1:2]
    xc = xc + shifted[0] * w[2:3]
    xc = xc + xr * w[3:4]

    xcb = xc.astype(_BF16)
    half = wa_ref.shape[1]

    def gate(w_ref, b_ref):
        z = jnp.concatenate([_dot(xcb[:, :half], w_ref[0]), _dot(xcb[:, half:], w_ref[1])], axis=1)
        return _sigmoid(z + b_ref[...])

    r_gate = gate(wa_ref, ba_ref)
    i_gate = gate(wx_ref, bx_ref)
    log_a = -RG_C * r_gate * jax.nn.softplus(-lam_ref[...])
    a = jnp.exp(log_a)
    om = -jnp.tanh(log_a) * (a * a + 1.0)
    root = jnp.where(om > 0.0, om * lax.rsqrt(om), 0.0)
    return a, root * (i_gate * xc)


def _scan_step(a, b, a_s, b_s):
    return a * a_s, b + a * b_s


def _rglru_prompt_tile(xr, gr, hc, rec_refs, xbuf_ref, abuf_ref, bbuf_ref):
    tm, rw = xr.shape
    sub = V7X_SUBLANES
    xbuf_ref[sub:sub + tm] = xr
    shifted = [xbuf_ref[sub - k:sub - k + tm] for k in range(1, CONV_WIDTH)]
    a, b = _rglru_coeffs(xr, shifted, rec_refs)
    abuf_ref[0:sub] = jnp.ones((sub, rw), _F32)
    bbuf_ref[0:sub] = jnp.zeros((sub, rw), _F32)
    d = 1
    while d < tm:
        if d % sub == 0:
            a_new, b_new = _scan_step(a[d:], b[d:], a[:tm - d], b[:tm - d])
            a = jnp.concatenate([a[:d], a_new], axis=0)
            b = jnp.concatenate([b[:d], b_new], axis=0)
        else:
            abuf_ref[sub:sub + tm] = a
            bbuf_ref[sub:sub + tm] = b
            a, b = _scan_step(a, b, abuf_ref[sub - d:sub - d + tm], bbuf_ref[sub - d:sub - d + tm])
        d *= 2
    h = a * hc + b
    return h * jax.nn.gelu(gr), h


def _proj_kernel(x_ref, gmix_ref, w_ref, wft_ref, bf_ref, gq_ref, gk_ref, gmat_ref, cum_ref,
                 aq_ref, ak_ref, cq_ref, ck_ref, *refs, tiles_per_seq, aw, rw):
    rec_refs = refs[:7]
    (k_out, v_out, lft_out, qs_out, kb_out, vt_out, augq_out, augk_out, rec_out, h_out, xtail_out,
     carry_ref, hc_ref, xbuf_ref, abuf_ref, bbuf_ref) = refs[7:]
    i = pl.program_id(0)
    tm = x_ref.shape[0]
    sub = V7X_SUBLANES
    seq_start = i % tiles_per_seq == 0

    @pl.when(seq_start)
    def _():
        carry_ref[...] = jnp.zeros_like(carry_ref)
        hc_ref[...] = jnp.zeros_like(hc_ref)
        xbuf_ref[0:sub] = jnp.zeros((sub, rw), _F32)

    @pl.when(jnp.logical_not(seq_start))
    def _():
        xbuf_ref[0:sub] = xbuf_ref[tm:tm + sub]

    hn = _rms(x_ref[...], gmix_ref[...]).astype(_BF16)
    w_rec = 3 * aw
    proj_rec = _dot(hn, w_ref[:, w_rec:w_rec + 2 * rw])
    xr = proj_rec[:, 0:rw]
    xtail_out[...] = xr[tm - sub:tm]

    y, h = _rglru_prompt_tile(xr, proj_rec[:, rw:2 * rw], hc_ref[...], rec_refs, xbuf_ref, abuf_ref, bbuf_ref)
    hc_ref[...] = h[tm - 1:tm]
    rec_out[...] = y
    h_out[...] = h[tm - sub:tm]

    proj = _dot(hn, w_ref[:, 0:w_rec])
    gmat = gmat_ref[...]
    qn = _head_norm(proj[:, 0:aw], gq_ref[...], gmat)
    kn = _head_norm(proj[:, aw:2 * aw], gk_ref[...], gmat)
    vt = proj[:, 2 * aw:3 * aw].T
    k_out[...] = kn.T
    v_out[...] = vt
    qs_out[...] = qn.astype(_BF16)
    kb_out[...] = kn.astype(_BF16)
    vt_out[...] = vt.astype(_BF16)

    ft = _dot_nt(wft_ref[...], hn)
    lft = jax.nn.log_sigmoid(ft[0:8] + bf_ref[...])
    lft_out[...] = lft

    p3 = _stack3(lft, V7X_LANES)
    blk = cum_ref.shape[0]
    carry = carry_ref[...]
    cums = []
    for c in range(tm // blk):
        r = _dot(p3[:, c * blk:(c + 1) * blk], cum_ref[...])
        cums.append(_sum3(r[:, :blk]) + carry)
        carry = carry + _sum3(r[:, blk:])
    carry_ref[...] = carry
    cum = jnp.concatenate(cums, axis=1)

    c3 = _stack3(cum * LOG2_E, V7X_LANES)
    qat = _dot(aq_ref[...], c3) + cq_ref[...]
    kat = _dot(ak_ref[...], c3) + ck_ref[...]
    augq_out[...] = qat.T.astype(_BF16)
    augk_out[...] = kat.T.astype(_BF16)


def _bias_placement(n_heads):
    aq = np.zeros((V7X_LANES, V7X_LANES), np.float32)
    ak = np.zeros((V7X_LANES, V7X_LANES), np.float32)
    cq = np.zeros((V7X_LANES, 1), np.float32)
    ck = np.zeros((V7X_LANES, 1), np.float32)
    for h in range(n_heads):
        for j in range(3):
            aq[h * 8 + j, j * 8 + h] = 1.0
            cq[h * 8 + 3 + j, 0] = 1.0
            ak[h * 8 + 3 + j, j * 8 + h] = -1.0
            ck[h * 8 + j, 0] = 1.0
    return aq, ak, cq, ck


def _group_mean_matrix(head_dim):
    idx = np.arange(V7X_MXU_DIM) // head_dim
    return (idx[:, None] == idx[None, :]).astype(np.float32) / head_dim


def _prefix_matrix():
    idx = np.arange(V7X_MXU_DIM)
    incl = (idx[:, None] <= idx[None, :]).astype(np.float32)
    return np.concatenate([incl, np.ones_like(incl)], axis=1)


def _proj_prompt(x2d, seq, gmix, w_main, wft, bf_col, gq, gk, rec_w, n_heads, head_dim, rw):
    n, d = x2d.shape
    tm = PROJ_ROWS
    aw = n_heads * head_dim
    aq, ak, cq, ck = _bias_placement(n_heads)
    consts = [
        gmix, w_main, wft, bf_col, gq, gk,
        jnp.asarray(_group_mean_matrix(head_dim), _BF16),
        jnp.asarray(_prefix_matrix(), _BF16),
        jnp.asarray(aq, _BF16), jnp.asarray(ak, _BF16), jnp.asarray(cq), jnp.asarray(ck),
        *rec_w,
    ]
    tps = seq // tm
    batch = n // seq
    row = lambda w: pl.BlockSpec((tm, w), lambda i: (i, 0))
    seq_t = lambda r: pl.BlockSpec((None, r, tm), lambda i: (i // tps, 0, i % tps))
    out_shape = (
        jax.ShapeDtypeStruct((batch, aw, seq), _F32),
        jax.ShapeDtypeStruct((batch, aw, seq), _F32),
        jax.ShapeDtypeStruct((batch, V7X_SUBLANES, seq), _F32),
        jax.ShapeDtypeStruct((n, aw), _BF16),
        jax.ShapeDtypeStruct((n, aw), _BF16),
        jax.ShapeDtypeStruct((batch, aw, seq), _BF16),
        jax.ShapeDtypeStruct((n, V7X_LANES), _BF16),
        jax.ShapeDtypeStruct((n, V7X_LANES), _BF16),
        jax.ShapeDtypeStruct((n, rw), _F32),
        jax.ShapeDtypeStruct((n // tm * V7X_SUBLANES, rw), _F32),
        jax.ShapeDtypeStruct((n // tm * V7X_SUBLANES, rw), _F32),
    )
    tail8 = pl.BlockSpec((V7X_SUBLANES, rw), lambda i: (i, 0))
    out_specs = (
        seq_t(aw), seq_t(aw), seq_t(V7X_SUBLANES), row(aw), row(aw), seq_t(aw),
        row(V7X_LANES), row(V7X_LANES), row(rw), tail8, tail8,
    )
    return pl.pallas_call(
        functools.partial(_proj_kernel, tiles_per_seq=tps, aw=aw, rw=rw),
        grid=(n // tm,),
        in_specs=[row(d)] + [_const_spec(c.shape) for c in consts],
        out_specs=out_specs,
        out_shape=out_shape,
        scratch_shapes=[pltpu.VMEM((V7X_SUBLANES, V7X_MXU_DIM), _F32), pltpu.VMEM((1, rw), _F32)]
        + [pltpu.VMEM((V7X_SUBLANES + tm, rw), _F32)] * 3,
        compiler_params=_params(1),
        name="proj_prompt",
    )(x2d, *consts)


class _PagedSide:
    def __init__(self, small, pages, pps, page, uo_ref, o_ref, scratch, t_new, head_dim, n_pages):
        self.q_ref, self.kn_ref, self.vn_ref, self.cnq_ref, self.cnk_ref = small
        self.k_page, self.v_page, self.lf_page = pages
        self.pps, self.page, self.uo_ref, self.o_ref = pps, page, uo_ref, o_ref
        self.qbd_ref, self.m_ref, self.l_ref, self.acc_ref, self.carry_ref = scratch
        self.t_new, self.head_dim, self.n_pages = t_new, head_dim, n_pages
        self.nh = V7X_SUBLANES
        self.aw = self.q_ref.shape[1]

    def _head_mask(self):
        feat = lax.broadcasted_iota(jnp.int32, (self.nh, self.aw), 1)
        head = lax.broadcasted_iota(jnp.int32, (self.nh, self.aw), 0)
        return (feat >= head * self.head_dim) & (feat < (head + 1) * self.head_dim)

    def init(self):
        nh, aw, page, t_new = self.nh, self.aw, self.page, self.t_new
        rows = t_new * nh
        q = self.q_ref[...]
        head_mask = self._head_mask()
        qbd = jnp.concatenate(
            [jnp.where(head_mask, jnp.broadcast_to(q[t:t + 1], (nh, aw)), 0.0) for t in range(t_new)],
            axis=0).astype(_BF16)
        self.qbd_ref[...] = qbd
        pad = jnp.zeros((page - self.kn_ref.shape[0], aw), _F32)
        kn = jnp.concatenate([self.kn_ref[...], pad], axis=0).astype(_BF16)
        vn = jnp.concatenate([self.vn_ref[...], pad], axis=0).astype(_BF16)
        s = _dot_nt(qbd, kn) + self.cnq_ref[...] - self.cnk_ref[...]
        row = lax.broadcasted_iota(jnp.int32, (rows, page), 0)
        t_col = lax.broadcasted_iota(jnp.int32, (rows, page), 1)
        s = jnp.where(t_col * nh <= row, s, -jnp.inf)
        m = jnp.max(s, axis=1, keepdims=True)
        p = jnp.exp(s - m)
        self.m_ref[...] = m
        self.l_ref[...] = jnp.sum(p, axis=1, keepdims=True)
        self.acc_ref[...] = _dot(p.astype(_BF16), vn)
        self.carry_ref[...] = jnp.zeros_like(self.carry_ref)

    def scores(self, step):
        pps, page = self.pps, self.page
        qbd = self.qbd_ref[...]
        carry = self.carry_ref[...]
        scores = []
        for i in range(pps):
            valid = step * pps + i < self.n_pages
            s = _dot(qbd, self.k_page(i).astype(_BF16))
            lf = jnp.where(valid, self.lf_page(i), 0.0)
            r = _dot(_stack3(lf, 4 * V7X_SUBLANES), self.uo_ref[...])
            bias = _sum3(r[:, :page]) + carry
            carry = carry + _sum3(r[:, page:])
            s = s + jnp.concatenate([bias] * self.t_new, axis=0)
            scores.append(jnp.where(valid, s, -jnp.inf))
        self.carry_ref[...] = carry
        s_all = jnp.concatenate(scores, axis=1) + self.cnq_ref[...]
        m_old = self.m_ref[...]
        m_new = jnp.maximum(m_old, jnp.max(s_all, axis=1, keepdims=True))
        alpha = jnp.exp(m_old - m_new)
        p = jnp.exp(s_all - m_new)
        self.l_ref[...] = alpha * self.l_ref[...] + jnp.sum(p, axis=1, keepdims=True)
        self.m_ref[...] = m_new
        return p.astype(_BF16), alpha

    def values(self, p, alpha):
        vt_all = jnp.concatenate([self.v_page(i).astype(_BF16) for i in range(self.pps)], axis=1)
        self.acc_ref[...] = alpha * self.acc_ref[...] + _dot_nt(p, vt_all)

    def finish(self):
        nh = self.nh
        a = self.acc_ref[...] / self.l_ref[...]
        head_mask = self._head_mask()
        for t in range(self.t_new):
            self.o_ref[t:t + 1, :] = jnp.sum(jnp.where(head_mask, a[t * nh:(t + 1) * nh], 0.0),
                                             axis=0, keepdims=True)


def _fox_kernel(qi_tab, ki_tab, pt_ref, q_ref, aq_ref, k_ref, ak_ref, vt_ref, *refs,
                n_heads, head_dim, n_samp, pps, t_new, n_pages):
    n_small = 5
    small = refs[:n_small * n_samp]
    rest = refs[n_small * n_samp:]
    cache_refs = rest[0:3]
    uo_ref, o_ref = rest[3], rest[4]
    so_refs = rest[5:5 + n_samp]
    qf_ref, m_ref, acc_ref = rest[5 + n_samp:8 + n_samp]
    s_scratch = rest[8 + n_samp:8 + 6 * n_samp]
    page_bufs = rest[8 + 6 * n_samp:11 + 6 * n_samp]
    sem = rest[11 + 6 * n_samp]

    batch_i = pl.program_id(0)
    step = pl.program_id(1)
    n_steps = pl.num_programs(1)
    g = batch_i * n_steps + step
    slot = g % 2
    page = page_bufs[0].shape[3]

    def page_copies(b2, s2, dst_slot, lookup):
        copies = []
        for u in range(n_samp):
            for i in range(pps):
                pid = 0
                if lookup:
                    pid = pt_ref[b2 * n_samp + u, jnp.maximum(n_pages - 1 - (s2 * pps + i), 0)]
                for c, (cache, buf) in enumerate(zip(cache_refs, page_bufs)):
                    copies.append(pltpu.make_async_copy(cache.at[pid], buf.at[dst_slot, u * pps + i],
                                                        sem.at[dst_slot, c]))
        return copies

    @pl.when(g == 0)
    def _first_pages():
        for c in page_copies(0, 0, 0, True):
            c.start()

    @pl.when(g + 1 < pl.num_programs(0) * n_steps)
    def _next_pages():
        wrap = step + 1 == n_steps
        for c in page_copies(jnp.where(wrap, batch_i + 1, batch_i), jnp.where(wrap, 0, step + 1),
                             1 - slot, True):
            c.start()

    for c in page_copies(batch_i, step, slot, False):
        c.wait()

    def reader(buf, u):
        return lambda i: buf[slot, u * pps + i]

    paged = [
        _PagedSide(small[u * n_small:(u + 1) * n_small], [reader(buf, u) for buf in page_bufs], pps, page,
                   uo_ref, so_refs[u], s_scratch[u * 5:(u + 1) * 5], t_new, head_dim, n_pages)
        for u in range(n_samp)]

    qi = qi_tab[step]
    ki = ki_tab[step]
    bq = q_ref.shape[0]
    bk = k_ref.shape[0]
    pair = 2 * head_dim

    @pl.when(step == 0)
    def _paged_init():
        for side in paged:
            side.init()

    @pl.when(ki == 0)
    def _init():
        lane = lax.broadcasted_iota(jnp.int32, (bq, V7X_LANES), 1)
        a = aq_ref[...]
        zero = jnp.zeros_like(a)
        for h in range(n_heads):
            q2 = q_ref[:, (h // 2) * pair:(h // 2 + 1) * pair]
            lo = head_dim * (h % 2)
            qh = jnp.where((lane >= lo) & (lane < lo + head_dim), q2, zero)
            ah = jnp.where((lane >= h * 8) & (lane < h * 8 + 8), a, zero)
            qf_ref[h] = jnp.concatenate([qh, ah], axis=1)
        m_ref[...] = jnp.full_like(m_ref, -jnp.inf)
        acc_ref[...] = jnp.zeros_like(acc_ref)

    def step_body(masked):
        ak = ak_ref[...]
        hk, hq = bk // 2, bq // 2
        ones_blk = (lax.broadcasted_iota(jnp.int32, (16, bk), 0) == 0).astype(_BF16)
        if masked:
            tri = (lax.broadcasted_iota(jnp.int32, (hk, hq), 0)
                   <= lax.broadcasted_iota(jnp.int32, (hk, hq), 1))

        def scores(h):
            hp = h // 2
            kf = jnp.concatenate([k_ref[:, hp * pair:(hp + 1) * pair], ak], axis=1)
            if not masked:
                return _dot_nt(kf, qf_ref[h])
            return (_dot_nt(kf[:hk], qf_ref[h]),
                    _dot_nt(kf[hk:], qf_ref[h, hq:, :]))

        paged_p = [side.scores(step) for side in paged]
        ahead = [scores(h) for h in range(FOX_LOOKAHEAD)]
        for side, (p_s, alpha_s) in zip(paged, paged_p):
            side.values(p_s, alpha_s)
        for h in range(n_heads):
            s = ahead.pop(0)
            if h + FOX_LOOKAHEAD < n_heads:
                ahead.append(scores(h + FOX_LOOKAHEAD))
            m_old = m_ref[h]
            lhs = jnp.concatenate([vt_ref[h * head_dim:(h + 1) * head_dim, :], ones_blk], axis=0)
            if not masked:
                m_new = jnp.maximum(m_old, jnp.max(s, axis=0, keepdims=True))
                pv = _dot(lhs, jnp.exp2(s - m_new).astype(_BF16))
            else:
                s_a, s_b = s
                s_a = jnp.concatenate([jnp.where(tri, s_a[:, :hq], -jnp.inf), s_a[:, hq:]], axis=1)
                s_b = jnp.where(tri, s_b, -jnp.inf)
                top = jnp.max(s_a, axis=0, keepdims=True)
                top = jnp.concatenate([top[:, :hq], jnp.maximum(top[:, hq:], jnp.max(s_b, axis=0, keepdims=True))],
                                      axis=1)
                m_new = jnp.maximum(m_old, top)
                pv = _dot(lhs[:, :hk], jnp.exp2(s_a - m_new).astype(_BF16))
                pv_b = _dot(lhs[:, hk:], jnp.exp2(s_b - m_new[:, hq:]).astype(_BF16))
                pv = jnp.concatenate([pv[:, :hq], pv[:, hq:] + pv_b], axis=1)
            acc_ref[h] = jnp.exp2(m_old - m_new) * acc_ref[h] + pv
            m_ref[h] = m_new

    @pl.when(ki < qi)
    def _off_diagonal():
        step_body(False)

    @pl.when(ki == qi)
    def _diagonal():
        step_body(True)
        for hp in range(n_heads // 2):
            outs = []
            for h in (2 * hp, 2 * hp + 1):
                a = acc_ref[h]
                outs.append(a[0:head_dim] / a[head_dim:head_dim + 1])
            o_ref[:, hp * pair:(hp + 1) * pair] = jnp.concatenate(outs, axis=0).T

    @pl.when(step == pl.num_programs(1) - 1)
    def _paged_finish():
        for side in paged:
            side.finish()


def _suffix_matrix(page):
    idx = np.arange(page)
    strict = (idx[:, None] > idx[None, :]).astype(np.float32)
    return np.concatenate([strict, np.ones_like(strict)], axis=1)


def _attention(qs, augq, kb, augk, vt, batch, seq, q_s, k_new, v_new, cn, cache_kt, cache_vt, cache_lft,
               page_table, t_new, n_heads, head_dim):
    aw = n_heads * head_dim
    bq = bk = ATTN_BLOCK
    nq = seq // bq
    qs3 = qs.reshape(batch, seq, aw)
    kb3 = kb.reshape(batch, seq, aw)
    aq3 = augq.reshape(batch, seq, V7X_LANES)
    ak3 = augk.reshape(batch, seq, V7X_LANES)
    pairs = [(qi, ki) for qi in range(nq) for ki in range(qi + 1)]
    qi_tab = jnp.asarray([p[0] for p in pairs], jnp.int32)
    ki_tab = jnp.asarray([p[1] for p in pairs], jnp.int32)
    n_steps = len(pairs)

    db, n_pages = page_table.shape
    page = cache_kt.shape[2]
    assert db % batch == 0, "sample batches must split evenly over the prompt batches"
    n_samp = db // batch
    pps = -(-n_pages // n_steps)
    rows = t_new * n_heads
    pad_t = V7X_SUBLANES - t_new
    q3 = q_s.reshape(db, t_new, aw)
    kn = jnp.pad(k_new.reshape(db, t_new, aw), ((0, 0), (0, pad_t), (0, 0)))
    vn = jnp.pad(v_new.reshape(db, t_new, aw), ((0, 0), (0, pad_t), (0, 0)))
    cn3 = cn.reshape(db, t_new, n_heads)
    cnq = cn3.reshape(db, rows, 1)
    cnk = jnp.tile(jnp.swapaxes(cn3, 1, 2), (1, t_new, 1))
    cnk = jnp.pad(cnk, ((0, 0), (0, 0), (0, page - t_new)))
    uo = jnp.asarray(_suffix_matrix(page), _BF16)

    qmap = lambda b, s, qt, kt, pt: (b, qt[s], 0)
    kmap = lambda b, s, qt, kt, pt: (b, kt[s], 0)

    def samp_spec(shape, u):
        return pl.BlockSpec((None,) + shape, lambda b, s, qt, kt, pt: (b * n_samp + u, 0, 0))

    small_specs, small_args = [], []
    for u in range(n_samp):
        small_specs += [samp_spec((t_new, aw), u), samp_spec((V7X_SUBLANES, aw), u),
                        samp_spec((V7X_SUBLANES, aw), u), samp_spec((rows, 1), u), samp_spec((rows, page), u)]
        small_args += [q3, kn, vn, cnq, cnk]
    caches = (cache_kt, cache_vt, cache_lft)

    grid_spec = pltpu.PrefetchScalarGridSpec(
        num_scalar_prefetch=3,
        grid=(batch, n_steps),
        in_specs=[
            pl.BlockSpec((None, bq, aw), qmap),
            pl.BlockSpec((None, bq, V7X_LANES), qmap),
            pl.BlockSpec((None, bk, aw), kmap),
            pl.BlockSpec((None, bk, V7X_LANES), kmap),
            pl.BlockSpec((None, aw, bk), lambda b, s, qt, kt, pt: (b, 0, kt[s])),
        ] + small_specs + [pl.BlockSpec(memory_space=pl.ANY)] * len(caches)
        + [pl.BlockSpec(uo.shape, lambda b, s, qt, kt, pt: (0, 0))],
        out_specs=[pl.BlockSpec((None, bq, aw), qmap)]
        + [pl.BlockSpec((None, t_new, aw), lambda b, s, qt, kt, pt: (b, 0, 0))] * n_samp,
        scratch_shapes=[
            pltpu.VMEM((n_heads, bq, 2 * V7X_LANES), _BF16),
            pltpu.VMEM((n_heads, 1, bq), _F32),
            pltpu.VMEM((n_heads, head_dim + 16, bq), _F32),
        ] + [
            pltpu.VMEM((rows, aw), _BF16),
            pltpu.VMEM((rows, 1), _F32),
            pltpu.VMEM((rows, 1), _F32),
            pltpu.VMEM((rows, aw), _F32),
            pltpu.VMEM((V7X_SUBLANES, page), _F32),
        ] * n_samp + [
            pltpu.VMEM((2, n_samp * pps) + c.shape[1:], c.dtype) for c in caches
        ] + [pltpu.SemaphoreType.DMA((2, len(caches)))],
    )
    outs = pl.pallas_call(
        functools.partial(_fox_kernel, n_heads=n_heads, head_dim=head_dim, n_samp=n_samp, pps=pps,
                          t_new=t_new, n_pages=n_pages),
        grid_spec=grid_spec,
        out_shape=[jax.ShapeDtypeStruct((batch, seq, aw), _F32)]
        + [jax.ShapeDtypeStruct((batch, t_new, aw), _F32)] * n_samp,
        compiler_params=_params(2),
        name="attention",
    )(qi_tab, ki_tab, page_table, qs3, aq3, kb3, ak3, vt, *small_args, *caches, uo)
    att = outs[0].reshape(batch * seq, aw)
    att_s = jnp.stack(outs[1:], axis=1)
    return att, att_s.reshape(db * t_new, aw)


def _rglru_sample_kernel(xr_ref, gr_ref, p1_ref, p2_ref, p3_ref, h0_ref, *refs, seg_len):
    rec_refs = refs[:7]
    y_ref, h_ref = refs[7:]
    tm, rw = xr_ref.shape
    xr = xr_ref[...]
    t = lax.broadcasted_iota(jnp.int32, (tm, rw), 0) & (seg_len - 1)
    prev = (p1_ref, p2_ref, p3_ref)
    shifted = [jnp.where(t >= k, pltpu.roll(xr, k, 0), prev[k - 1][...]) for k in range(1, CONV_WIDTH)]
    a, b = _rglru_coeffs(xr, shifted, rec_refs)
    d = 1
    while d < seg_len:
        a_new, b_new = _scan_step(a, b, pltpu.roll(a, d, 0), pltpu.roll(b, d, 0))
        a = jnp.where(t >= d, a_new, a)
        b = jnp.where(t >= d, b_new, b)
        d *= 2
    h = a * h0_ref[...] + b
    y_ref[...] = h * jax.nn.gelu(gr_ref[...])
    h_ref[...] = h


def _rglru_sample(xr, gr, prevs, h0, seg_len, rec_w):
    n, rw = xr.shape
    assert seg_len & (seg_len - 1) == 0 and n % seg_len == 0
    full = pl.BlockSpec((n, rw), lambda i: (0, 0))
    return pl.pallas_call(
        functools.partial(_rglru_sample_kernel, seg_len=seg_len),
        grid=(1,),
        in_specs=[full] * 6 + [_const_spec(c.shape) for c in rec_w],
        out_specs=(full, full),
        out_shape=(jax.ShapeDtypeStruct((n, rw), _F32), jax.ShapeDtypeStruct((n, rw), _F32)),
        compiler_params=_params(1),
        name="rglru_sample",
    )(xr, gr, *prevs, h0, *rec_w)


def _tail_math(x, att, rec, p, goa_ref, gor_ref, wout_ref, gffn_ref, wg_ref, wu_ref, wo_ref, wple_ref,
               gple_ref, wpg_ref):
    aw = att.shape[1]
    an = _rms(att, goa_ref[...]).astype(_BF16)
    rn = _rms(rec, gor_ref[...]).astype(_BF16)
    x1 = x + _dot(an, wout_ref[0:aw, :]) + _dot(rn, wout_ref[aw:, :])
    hn = _rms(x1, gffn_ref[...]).astype(_BF16)
    hidden = wg_ref.shape[1]
    ffn = None
    for c0 in range(0, hidden, FFN_CHUNK):
        cw = min(FFN_CHUNK, hidden - c0)
        g = _dot(hn, wg_ref[:, c0:c0 + cw])
        u = _dot(hn, wu_ref[:, c0:c0 + cw])
        part = _dot((jax.nn.silu(g) * u).astype(_BF16), wo_ref[c0:c0 + cw, :])
        ffn = part if ffn is None else ffn + part
    x2 = x1 + ffn
    gate = jax.nn.sigmoid(_dot(_rms(x2, gple_ref[...]).astype(_BF16), wpg_ref[...]))
    return x2 + _dot(p.astype(_BF16), wple_ref[...]) * gate


def _tail_kernel(x_ref, att_ref, rec_ref, p_ref, *refs):
    o_ref = refs[-1]
    o_ref[...] = _tail_math(x_ref[...], att_ref[...], rec_ref[...], p_ref[...], *refs[:-1])


def _tail(x2d, att, rec, p2d, tail_w):
    n, d = x2d.shape
    tm = min(PROJ_ROWS, n)
    row = lambda w: pl.BlockSpec((tm, w), lambda i: (i, 0))
    return pl.pallas_call(
        _tail_kernel,
        grid=(n // tm,),
        in_specs=[row(d), row(att.shape[1]), row(rec.shape[1]), row(p2d.shape[1])]
        + [_const_spec(c.shape) for c in tail_w],
        out_specs=row(d),
        out_shape=jax.ShapeDtypeStruct((n, d), _F32),
        compiler_params=_params(1),
        name="tail",
    )(x2d, att, rec, p2d, *tail_w)


def _sproj_kernel(x_ref, gmix_ref, w_ref, wf_ref, bf_ref, gq_ref, gk_ref, gmat_ref, seg_ref,
                  q_out, k_out, v_out, lf_out, cn_out, xr_out, gr_out, *, aw, rw):
    hn = _rms(x_ref[...], gmix_ref[...]).astype(_BF16)
    proj = _dot(hn, w_ref[...])
    gmat = gmat_ref[...]
    q_out[...] = _head_norm(proj[:, 0:aw], gq_ref[...], gmat)
    k_out[...] = _head_norm(proj[:, aw:2 * aw], gk_ref[...], gmat)
    v_out[...] = proj[:, 2 * aw:3 * aw]
    xr_out[...] = proj[:, 3 * aw:3 * aw + rw]
    gr_out[...] = proj[:, 3 * aw + rw:3 * aw + 2 * rw]
    lf = jax.nn.log_sigmoid(_dot(hn, wf_ref[...]) + bf_ref[...])
    lf_out[...] = lf
    hi, mid, lo = _split3(lf)
    seg = seg_ref[...]
    cn_out[...] = _dot(seg, hi.astype(_BF16)) + _dot(seg, mid.astype(_BF16)) + _dot(seg, lo.astype(_BF16))


def _proj_sample(x2d, t_new, gmix, w_main, wf_pad, bf_row, gq, gk, n_heads, head_dim, rw):
    n, d = x2d.shape
    aw = n_heads * head_dim
    idx = np.arange(n)
    seg = ((idx[:, None] // t_new == idx[None, :] // t_new) & (idx[None, :] <= idx[:, None])).astype(np.float32)
    ins = [x2d, gmix, w_main, wf_pad, bf_row, gq, gk,
           jnp.asarray(_group_mean_matrix(head_dim), _BF16), jnp.asarray(seg, _BF16)]
    full = lambda s: pl.BlockSpec(s, lambda i: (0,) * len(s))
    widths = (aw, aw, aw, V7X_LANES, V7X_LANES, rw, rw)
    return pl.pallas_call(
        functools.partial(_sproj_kernel, aw=aw, rw=rw),
        grid=(1,),
        in_specs=[full(a.shape) for a in ins],
        out_specs=tuple(full((n, w)) for w in widths),
        out_shape=tuple(jax.ShapeDtypeStruct((n, w), _F32) for w in widths),
        compiler_params=_params(1),
        name="proj_sample",
    )(*ins)


def _block_diag_pair(w):
    nb, dd, _ = w.shape
    per = V7X_MXU_DIM // dd
    tiles = []
    for half in range(nb // per):
        tile = jnp.zeros((V7X_MXU_DIM, V7X_MXU_DIM), w.dtype)
        for j in range(per):
            tile = lax.dynamic_update_slice(tile, w[half * per + j], (j * dd, j * dd))
        tiles.append(tile)
    return jnp.stack(tiles).astype(_BF16)


def kernel(x_prompt, x_sample, p_prompt, p_sample, cache_k, cache_v, cache_logf, state_conv, state_h, page_table, g_mix, w_in, b_f, g_q, g_k, w_conv, b_conv, w_a, b_a, w_x, b_x, lam, g_out_attn, g_out_rec, w_out, g_ffn, w_ffn_in, w_ffn_out, w_ple, g_ple, w_ple_gate):
    batch, seq, d_model = x_prompt.shape
    db, t_new, _ = x_sample.shape
    depth, n_heads, head_dim = g_q.shape
    aw = n_heads * head_dim
    rw = lam.shape[1]
    hidden = w_ffn_out.shape[1]
    n_phys, page = cache_k.shape[1], cache_k.shape[2]
    assert n_heads == V7X_SUBLANES and aw == 2 * V7X_MXU_DIM and rw == 2 * V7X_MXU_DIM
    assert seq % ATTN_BLOCK == 0 and seq % PROJ_ROWS == 0

    xp = x_prompt.reshape(batch * seq, d_model)
    xs = x_sample.reshape(db * t_new, d_model)
    outs = [[] for _ in range(10)]
    row = lambda a: a.reshape(1, -1)
    for l in range(depth):
        w_l = w_in[l]
        w_main = jnp.concatenate([w_l[:, :3 * aw], w_l[:, 3 * aw + n_heads:]], axis=1).astype(_BF16)
        w_f = w_l[:, 3 * aw:3 * aw + n_heads]
        wft = jnp.pad(w_f.T, ((0, 16 - n_heads), (0, 0))).astype(_BF16)
        wf_pad = jnp.pad(w_f, ((0, 0), (0, V7X_LANES - n_heads))).astype(_BF16)
        bf_col = b_f[l].reshape(n_heads, 1)
        bf_row = jnp.pad(b_f[l], (0, V7X_LANES - n_heads)).reshape(1, V7X_LANES)
        gmix = row(g_mix[l])
        gq = row(g_q[l]) * (head_dim ** -0.5)
        gk = row(g_k[l])
        rec_w = (w_conv[l], row(b_conv[l]), _block_diag_pair(w_a[l]), row(b_a[l]),
                 _block_diag_pair(w_x[l]), row(b_x[l]), row(lam[l]))
        tail_w = (row(g_out_attn[l]), row(g_out_rec[l]), w_out[l].astype(_BF16), row(g_ffn[l]),
                  w_ffn_in[l][:, :hidden].astype(_BF16), w_ffn_in[l][:, hidden:].astype(_BF16),
                  w_ffn_out[l].astype(_BF16), w_ple[l].astype(_BF16), row(g_ple[l]),
                  w_ple_gate[l].astype(_BF16))

        kt, vt32, lft, qs, kb, vt, augq, augk, rec, h_tiles, x_tiles = _proj_prompt(
            xp, seq, gmix, w_main, wft, bf_col, gq * LOG2_E, gk, rec_w, n_heads, head_dim, rw)
        q_s, k_s, v_s, lf_s, cn_s, xr_s, gr_s = _proj_sample(
            xs, t_new, gmix, w_main, wf_pad, bf_row, gq, gk, n_heads, head_dim, rw)
        lf_s = lf_s[:, :n_heads]
        cache_lft = jnp.swapaxes(cache_logf[l], 1, 2)
        cache_kt = cache_k[l].transpose(0, 2, 3, 1).reshape(n_phys, aw, page)
        cache_vt = cache_v[l].transpose(0, 2, 3, 1).reshape(n_phys, aw, page)
        att, att_s = _attention(qs, augq, kb, augk, vt, batch, seq, q_s, k_s, v_s, cn_s[:, :n_heads],
                                cache_kt, cache_vt, cache_lft, page_table, t_new, n_heads, head_dim)

        xp = _tail(xp, att, rec, p_prompt[l].reshape(batch * seq, -1), tail_w)
        outs[0].append(kt.reshape(batch, n_heads, head_dim, seq).transpose(0, 3, 1, 2))
        outs[1].append(vt32.reshape(batch, n_heads, head_dim, seq).transpose(0, 3, 1, 2))
        outs[2].append(lft.transpose(0, 2, 1))
        outs[3].append(x_tiles.reshape(batch, -1, V7X_SUBLANES, rw)[:, -1, V7X_SUBLANES - (CONV_WIDTH - 1):])
        outs[4].append(h_tiles.reshape(batch, -1, V7X_SUBLANES, rw)[:, -1, -1])

        hist = jnp.concatenate([state_conv[l], jnp.zeros((db, t_new, rw), _F32)], axis=1)
        prevs = tuple(hist[:, CONV_WIDTH - 1 - s:CONV_WIDTH - 1 - s + t_new].reshape(db * t_new, rw)
                      for s in range(1, CONV_WIDTH))
        h0 = jnp.repeat(state_h[l], t_new, axis=0)
        rec_s, h_s = _rglru_sample(xr_s, gr_s, prevs, h0, t_new, rec_w)
        xs = _tail(xs, att_s, rec_s, p_sample[l].reshape(db * t_new, -1), tail_w)
        outs[5].append(k_s.reshape(db, t_new, n_heads, head_dim))
        outs[6].append(v_s.reshape(db, t_new, n_heads, head_dim))
        outs[7].append(lf_s.reshape(db, t_new, n_heads))
        outs[8].append(xr_s.reshape(db, t_new, rw)[:, t_new - (CONV_WIDTH - 1):])
        outs[9].append(h_s.reshape(db, t_new, rw)[:, -1])
    return (xp.reshape(batch, seq, d_model), xs.reshape(db, t_new, d_model),
            *(jnp.stack(o) for o in outs))
```

```python
import functools

import numpy as np
import jax
import jax.numpy as jnp
from jax import lax
from jax.experimental import pallas as pl
from jax.experimental.pallas import tpu as pltpu

RG_C = 8.0
NORM_EPS = 1e-6
LOG2_E = 1.4426950408889634
CONV_WIDTH = 4

V7X_LANES = 128
V7X_SUBLANES = 8
V7X_MXU_DIM = 256
V7X_VMEM_LIMIT_BYTES = 56 * 1024 * 1024

PROJ_ROWS = 512
ATTN_BLOCK = 512
FOX_LOOKAHEAD = 4
FFN_CHUNK = 256

_F32 = jnp.float32
_BF16 = jnp.bfloat16
_NT = (((1,), (1,)), ((), ()))


def _dot(a, b):
    return jnp.dot(a, b, preferred_element_type=_F32)


def _dot_nt(a, b):
    return lax.dot_general(a, b, _NT, preferred_element_type=_F32)


def _rms(x, g):
    return x * lax.rsqrt(jnp.mean(x * x, axis=-1, keepdims=True) + NORM_EPS) * g


def _split3(z):
    hi = z.astype(_BF16).astype(_F32)
    r = z - hi
    mid = r.astype(_BF16).astype(_F32)
    lo = r - mid
    return hi, mid, lo


def _stack3(z, rows):
    hi, mid, lo = _split3(z)
    pad = jnp.zeros((rows - 3 * V7X_SUBLANES, z.shape[1]), _F32)
    return jnp.concatenate([hi, mid, lo, pad], axis=0).astype(_BF16)


def _sum3(r):
    return r[0:8] + r[8:16] + r[16:24]


def _head_norm(z, g, gmat):
    zz = (z * z).astype(_BF16)
    half = gmat.shape[0]
    ms = jnp.concatenate([_dot(zz[:, :half], gmat), _dot(zz[:, half:], gmat)], axis=1)
    return z * lax.rsqrt(ms + NORM_EPS) * g


def _const_spec(shape):
    nd = len(shape)
    return pl.BlockSpec(shape, lambda *_: (0,) * nd, pipeline_mode=pl.Buffered(1))


def _params(n_axes):
    return pltpu.CompilerParams(
        dimension_semantics=("arbitrary",) * n_axes,
        vmem_limit_bytes=V7X_VMEM_LIMIT_BYTES,
    )


def _sigmoid(z):
    return 0.5 * jnp.tanh(0.5 * z) + 0.5


def _rglru_coeffs(xr, shifted, rec_refs):
    wconv_ref, bconv_ref, wa_ref, ba_ref, wx_ref, bx_ref, lam_ref = rec_refs
    w = wconv_ref[...]
    xc = bconv_ref[...] + shifted[2] * w[0:1]
    xc = xc + shifted[1] * w[1:2]
    xc = xc + shifted[0] * w[2:3]
    xc = xc + xr * w[3:4]

    xcb = xc.astype(_BF16)
    half = wa_ref.shape[1]

    def gate(w_ref, b_ref):
        z = jnp.concatenate([_dot(xcb[:, :half], w_ref[0]), _dot(xcb[:, half:], w_ref[1])], axis=1)
        return _sigmoid(z + b_ref[...])

    r_gate = gate(wa_ref, ba_ref)
    i_gate = gate(wx_ref, bx_ref)
    log_a = -RG_C * r_gate * jax.nn.softplus(-lam_ref[...])
    a = jnp.exp(log_a)
    om = -jnp.tanh(log_a) * (a * a + 1.0)
    root = jnp.where(om > 0.0, om * lax.rsqrt(om), 0.0)
    return a, root * (i_gate * xc)


def _scan_step(a, b, a_s, b_s):
    return a * a_s, b + a * b_s


def _rglru_prompt_tile(xr, gr, hc, rec_refs, xbuf_ref, abuf_ref, bbuf_ref):
    tm, rw = xr.shape
    sub = V7X_SUBLANES
    xbuf_ref[sub:sub + tm] = xr
    shifted = [xbuf_ref[sub - k:sub - k + tm] for k in range(1, CONV_WIDTH)]
    a, b = _rglru_coeffs(xr, shifted, rec_refs)
    abuf_ref[0:sub] = jnp.ones((sub, rw), _F32)
    bbuf_ref[0:sub] = jnp.zeros((sub, rw), _F32)
    d = 1
    while d < tm:
        if d % sub == 0:
            a_new, b_new = _scan_step(a[d:], b[d:], a[:tm - d], b[:tm - d])
            a = jnp.concatenate([a[:d], a_new], axis=0)
            b = jnp.concatenate([b[:d], b_new], axis=0)
        else:
            abuf_ref[sub:sub + tm] = a
            bbuf_ref[sub:sub + tm] = b
            a, b = _scan_step(a, b, abuf_ref[sub - d:sub - d + tm], bbuf_ref[sub - d:sub - d + tm])
        d *= 2
    h = a * hc + b
    return h * jax.nn.gelu(gr), h


def _proj_kernel(x_ref, gmix_ref, w_ref, wft_ref, bf_ref, gq_ref, gk_ref, gmat_ref, cum_ref,
                 aq_ref, ak_ref, cq_ref, ck_ref, *refs, tiles_per_seq, aw, rw):
    rec_refs = refs[:7]
    (k_out, v_out, lft_out, qs_out, kb_out, vt_out, augq_out, augk_out, rec_out, h_out, xtail_out,
     carry_ref, hc_ref, xbuf_ref, abuf_ref, bbuf_ref) = refs[7:]
    i = pl.program_id(0)
    tm = x_ref.shape[0]
    sub = V7X_SUBLANES
    seq_start = i % tiles_per_seq == 0

    @pl.when(seq_start)
    def _():
        carry_ref[...] = jnp.zeros_like(carry_ref)
        hc_ref[...] = jnp.zeros_like(hc_ref)
        xbuf_ref[0:sub] = jnp.zeros((sub, rw), _F32)

    @pl.when(jnp.logical_not(seq_start))
    def _():
        xbuf_ref[0:sub] = xbuf_ref[tm:tm + sub]

    hn = _rms(x_ref[...], gmix_ref[...]).astype(_BF16)
    w_rec = 3 * aw
    proj_rec = _dot(hn, w_ref[:, w_rec:w_rec + 2 * rw])
    xr = proj_rec[:, 0:rw]
    xtail_out[...] = xr[tm - sub:tm]

    y, h = _rglru_prompt_tile(xr, proj_rec[:, rw:2 * rw], hc_ref[...], rec_refs, xbuf_ref, abuf_ref, bbuf_ref)
    hc_ref[...] = h[tm - 1:tm]
    rec_out[...] = y
    h_out[...] = h[tm - sub:tm]

    proj = _dot(hn, w_ref[:, 0:w_rec])
    gmat = gmat_ref[...]
    qn = _head_norm(proj[:, 0:aw], gq_ref[...], gmat)
    kn = _head_norm(proj[:, aw:2 * aw], gk_ref[...], gmat)
    vt = proj[:, 2 * aw:3 * aw].T
    k_out[...] = kn.T
    v_out[...] = vt
    qs_out[...] = qn.astype(_BF16)
    kb_out[...] = kn.astype(_BF16)
    vt_out[...] = vt.astype(_BF16)

    ft = _dot_nt(wft_ref[...], hn)
    lft = jax.nn.log_sigmoid(ft[0:8] + bf_ref[...])
    lft_out[...] = lft

    p3 = _stack3(lft, V7X_LANES)
    blk = cum_ref.shape[0]
    carry = carry_ref[...]
    cums = []
    for c in range(tm // blk):
        r = _dot(p3[:, c * blk:(c + 1) * blk], cum_ref[...])
        cums.append(_sum3(r[:, :blk]) + carry)
        carry = carry + _sum3(r[:, blk:])
    carry_ref[...] = carry
    cum = jnp.concatenate(cums, axis=1)

    c3 = _stack3(cum * LOG2_E, V7X_LANES)
    qat = _dot(aq_ref[...], c3) + cq_ref[...]
    kat = _dot(ak_ref[...], c3) + ck_ref[...]
    augq_out[...] = qat.T.astype(_BF16)
    augk_out[...] = kat.T.astype(_BF16)


def _bias_placement(n_heads):
    aq = np.zeros((V7X_LANES, V7X_LANES), np.float32)
    ak = np.zeros((V7X_LANES, V7X_LANES), np.float32)
    cq = np.zeros((V7X_LANES, 1), np.float32)
    ck = np.zeros((V7X_LANES, 1), np.float32)
    for h in range(n_heads):
        for j in range(3):
            aq[h * 8 + j, j * 8 + h] = 1.0
            cq[h * 8 + 3 + j, 0] = 1.0
            ak[h * 8 + 3 + j, j * 8 + h] = -1.0
            ck[h * 8 + j, 0] = 1.0
    return aq, ak, cq, ck


def _group_mean_matrix(head_dim):
    idx = np.arange(V7X_MXU_DIM) // head_dim
    return (idx[:, None] == idx[None, :]).astype(np.float32) / head_dim


def _prefix_matrix():
    idx = np.arange(V7X_MXU_DIM)
    incl = (idx[:, None] <= idx[None, :]).astype(np.float32)
    return np.concatenate([incl, np.ones_like(incl)], axis=1)


def _proj_prompt(x2d, seq, gmix, w_main, wft, bf_col, gq, gk, rec_w, n_heads, head_dim, rw):
    n, d = x2d.shape
    tm = PROJ_ROWS
    aw = n_heads * head_dim
    aq, ak, cq, ck = _bias_placement(n_heads)
    consts = [
        gmix, w_main, wft, bf_col, gq, gk,
        jnp.asarray(_group_mean_matrix(head_dim), _BF16),
        jnp.asarray(_prefix_matrix(), _BF16),
        jnp.asarray(aq, _BF16), jnp.asarray(ak, _BF16), jnp.asarray(cq), jnp.asarray(ck),
        *rec_w,
    ]
    tps = seq // tm
    batch = n // seq
    row = lambda w: pl.BlockSpec((tm, w), lambda i: (i, 0))
    seq_t = lambda r: pl.BlockSpec((None, r, tm), lambda i: (i // tps, 0, i % tps))
    out_shape = (
        jax.ShapeDtypeStruct((batch, aw, seq), _F32),
        jax.ShapeDtypeStruct((batch, aw, seq), _F32),
        jax.ShapeDtypeStruct((batch, V7X_SUBLANES, seq), _F32),
        jax.ShapeDtypeStruct((n, aw), _BF16),
        jax.ShapeDtypeStruct((n, aw), _BF16),
        jax.ShapeDtypeStruct((batch, aw, seq), _BF16),
        jax.ShapeDtypeStruct((n, V7X_LANES), _BF16),
        jax.ShapeDtypeStruct((n, V7X_LANES), _BF16),
        jax.ShapeDtypeStruct((n, rw), _F32),
        jax.ShapeDtypeStruct((n // tm * V7X_SUBLANES, rw), _F32),
        jax.ShapeDtypeStruct((n // tm * V7X_SUBLANES, rw), _F32),
    )
    tail8 = pl.BlockSpec((V7X_SUBLANES, rw), lambda i: (i, 0))
    out_specs = (
        seq_t(aw), seq_t(aw), seq_t(V7X_SUBLANES), row(aw), row(aw), seq_t(aw),
        row(V7X_LANES), row(V7X_LANES), row(rw), tail8, tail8,
    )
    return pl.pallas_call(
        functools.partial(_proj_kernel, tiles_per_seq=tps, aw=aw, rw=rw),
        grid=(n // tm,),
        in_specs=[row(d)] + [_const_spec(c.shape) for c in consts],
        out_specs=out_specs,
        out_shape=out_shape,
        scratch_shapes=[pltpu.VMEM((V7X_SUBLANES, V7X_MXU_DIM), _F32), pltpu.VMEM((1, rw), _F32)]
        + [pltpu.VMEM((V7X_SUBLANES + tm, rw), _F32)] * 3,
        compiler_params=_params(1),
        name="proj_prompt",
    )(x2d, *consts)


class _PagedSide:
    def __init__(self, small, pages, pps, page, uo_ref, o_ref, scratch, t_new, head_dim, n_pages):
        self.q_ref, self.kn_ref, self.vn_ref, self.cnq_ref, self.cnk_ref = small
        self.k_page, self.v_page, self.lf_page = pages
        self.pps, self.page, self.uo_ref, self.o_ref = pps, page, uo_ref, o_ref
        self.qbd_ref, self.m_ref, self.l_ref, self.acc_ref, self.carry_ref = scratch
        self.t_new, self.head_dim, self.n_pages = t_new, head_dim, n_pages
        self.nh = V7X_SUBLANES
        self.aw = self.q_ref.shape[1]

    def _head_mask(self):
        feat = lax.broadcasted_iota(jnp.int32, (self.nh, self.aw), 1)
        head = lax.broadcasted_iota(jnp.int32, (self.nh, self.aw), 0)
        return (feat >= head * self.head_dim) & (feat < (head + 1) * self.head_dim)

    def init(self):
        nh, aw, page, t_new = self.nh, self.aw, self.page, self.t_new
        rows = t_new * nh
        q = self.q_ref[...]
        head_mask = self._head_mask()
        qbd = jnp.concatenate(
            [jnp.where(head_mask, jnp.broadcast_to(q[t:t + 1], (nh, aw)), 0.0) for t in range(t_new)],
            axis=0).astype(_BF16)
        self.qbd_ref[...] = qbd
        pad = jnp.zeros((page - self.kn_ref.shape[0], aw), _F32)
        kn = jnp.concatenate([self.kn_ref[...], pad], axis=0).astype(_BF16)
        vn = jnp.concatenate([self.vn_ref[...], pad], axis=0).astype(_BF16)
        s = _dot_nt(qbd, kn) + self.cnq_ref[...] - self.cnk_ref[...]
        row = lax.broadcasted_iota(jnp.int32, (rows, page), 0)
        t_col = lax.broadcasted_iota(jnp.int32, (rows, page), 1)
        s = jnp.where(t_col * nh <= row, s, -jnp.inf)
        m = jnp.max(s, axis=1, keepdims=True)
        p = jnp.exp(s - m)
        self.m_ref[...] = m
        self.l_ref[...] = jnp.sum(p, axis=1, keepdims=True)
        self.acc_ref[...] = _dot(p.astype(_BF16), vn)
        self.carry_ref[...] = jnp.zeros_like(self.carry_ref)

    def scores(self, step):
        pps, page = self.pps, self.page
        qbd = self.qbd_ref[...]
        carry = self.carry_ref[...]
        scores = []
        for i in range(pps):
            valid = step * pps + i < self.n_pages
            s = _dot(qbd, self.k_page(i).astype(_BF16))
            lf = jnp.where(valid, self.lf_page(i), 0.0)
            r = _dot(_stack3(lf, 4 * V7X_SUBLANES), self.uo_ref[...])
            bias = _sum3(r[:, :page]) + carry
            carry = carry + _sum3(r[:, page:])
            s = s + jnp.concatenate([bias] * self.t_new, axis=0)
            scores.append(jnp.where(valid, s, -jnp.inf))
        self.carry_ref[...] = carry
        s_all = jnp.concatenate(scores, axis=1) + self.cnq_ref[...]
        m_old = self.m_ref[...]
        m_new = jnp.maximum(m_old, jnp.max(s_all, axis=1, keepdims=True))
        alpha = jnp.exp(m_old - m_new)
        p = jnp.exp(s_all - m_new)
        self.l_ref[...] = alpha * self.l_ref[...] + jnp.sum(p, axis=1, keepdims=True)
        self.m_ref[...] = m_new
        return p.astype(_BF16), alpha

    def values(self, p, alpha):
        vt_all = jnp.concatenate([self.v_page(i).astype(_BF16) for i in range(self.pps)], axis=1)
        self.acc_ref[...] = alpha * self.acc_ref[...] + _dot_nt(p, vt_all)

    def finish(self):
        nh = self.nh
        a = self.acc_ref[...] / self.l_ref[...]
        head_mask = self._head_mask()
        for t in range(self.t_new):
            self.o_ref[t:t + 1, :] = jnp.sum(jnp.where(head_mask, a[t * nh:(t + 1) * nh], 0.0),
                                             axis=0, keepdims=True)


def _fox_kernel(qi_tab, ki_tab, pt_ref, q_ref, aq_ref, k_ref, ak_ref, vt_ref, *refs,
                n_heads, head_dim, n_samp, pps, t_new, n_pages):
    n_small = 5
    small = refs[:n_small * n_samp]
    rest = refs[n_small * n_samp:]
    cache_refs = rest[0:3]
    uo_ref, o_ref = rest[3], rest[4]
    so_refs = rest[5:5 + n_samp]
    qf_ref, m_ref, acc_ref = rest[5 + n_samp:8 + n_samp]
    s_scratch = rest[8 + n_samp:8 + 6 * n_samp]
    page_bufs = rest[8 + 6 * n_samp:11 + 6 * n_samp]
    sem = rest[11 + 6 * n_samp]

    batch_i = pl.program_id(0)
    step = pl.program_id(1)
    n_steps = pl.num_programs(1)
    g = batch_i * n_steps + step
    slot = g % 2
    page = page_bufs[0].shape[3]

    def page_copies(b2, s2, dst_slot, lookup):
        copies = []
        for u in range(n_samp):
            for i in range(pps):
                pid = 0
                if lookup:
                    pid = pt_ref[b2 * n_samp + u, jnp.maximum(n_pages - 1 - (s2 * pps + i), 0)]
                for c, (cache, buf) in enumerate(zip(cache_refs, page_bufs)):
                    copies.append(pltpu.make_async_copy(cache.at[pid], buf.at[dst_slot, u * pps + i],
                                                        sem.at[dst_slot, c]))
        return copies

    @pl.when(g == 0)
    def _first_pages():
        for c in page_copies(0, 0, 0, True):
            c.start()

    @pl.when(g + 1 < pl.num_programs(0) * n_steps)
    def _next_pages():
        wrap = step + 1 == n_steps
        for c in page_copies(jnp.where(wrap, batch_i + 1, batch_i), jnp.where(wrap, 0, step + 1),
                             1 - slot, True):
            c.start()

    for c in page_copies(batch_i, step, slot, False):
        c.wait()

    def reader(buf, u):
        return lambda i: buf[slot, u * pps + i]

    paged = [
        _PagedSide(small[u * n_small:(u + 1) * n_small], [reader(buf, u) for buf in page_bufs], pps, page,
                   uo_ref, so_refs[u], s_scratch[u * 5:(u + 1) * 5], t_new, head_dim, n_pages)
        for u in range(n_samp)]

    qi = qi_tab[step]
    ki = ki_tab[step]
    bq = q_ref.shape[0]
    bk = k_ref.shape[0]
    pair = 2 * head_dim

    @pl.when(step == 0)
    def _paged_init():
        for side in paged:
            side.init()

    @pl.when(ki == 0)
    def _init():
        lane = lax.broadcasted_iota(jnp.int32, (bq, V7X_LANES), 1)
        a = aq_ref[...]
        zero = jnp.zeros_like(a)
        for h in range(n_heads):
            q2 = q_ref[:, (h // 2) * pair:(h // 2 + 1) * pair]
            lo = head_dim * (h % 2)
            qh = jnp.where((lane >= lo) & (lane < lo + head_dim), q2, zero)
            ah = jnp.where((lane >= h * 8) & (lane < h * 8 + 8), a, zero)
            qf_ref[h] = jnp.concatenate([qh, ah], axis=1)
        m_ref[...] = jnp.full_like(m_ref, -jnp.inf)
        acc_ref[...] = jnp.zeros_like(acc_ref)

    def step_body(masked):
        ak = ak_ref[...]
        hk, hq = bk // 2, bq // 2
        ones_blk = (lax.broadcasted_iota(jnp.int32, (16, bk), 0) == 0).astype(_BF16)
        if masked:
            tri = (lax.broadcasted_iota(jnp.int32, (hk, hq), 0)
                   <= lax.broadcasted_iota(jnp.int32, (hk, hq), 1))

        def scores(h):
            hp = h // 2
            kf = jnp.concatenate([k_ref[:, hp * pair:(hp + 1) * pair], ak], axis=1)
            if not masked:
                return _dot_nt(kf, qf_ref[h])
            return (_dot_nt(kf[:hk], qf_ref[h]),
                    _dot_nt(kf[hk:], qf_ref[h, hq:, :]))

        paged_p = [side.scores(step) for side in paged]
        ahead = [scores(h) for h in range(FOX_LOOKAHEAD)]
        for side, (p_s, alpha_s) in zip(paged, paged_p):
            side.values(p_s, alpha_s)
        for h in range(n_heads):
            s = ahead.pop(0)
            if h + FOX_LOOKAHEAD < n_heads:
                ahead.append(scores(h + FOX_LOOKAHEAD))
            m_old = m_ref[h]
            lhs = jnp.concatenate([vt_ref[h * head_dim:(h + 1) * head_dim, :], ones_blk], axis=0)
            if not masked:
                m_new = jnp.maximum(m_old, jnp.max(s, axis=0, keepdims=True))
                pv = _dot(lhs, jnp.exp2(s - m_new).astype(_BF16))
            else:
                s_a, s_b = s
                s_a = jnp.concatenate([jnp.where(tri, s_a[:, :hq], -jnp.inf), s_a[:, hq:]], axis=1)
                s_b = jnp.where(tri, s_b, -jnp.inf)
                top = jnp.max(s_a, axis=0, keepdims=True)
                top = jnp.concatenate([top[:, :hq], jnp.maximum(top[:, hq:], jnp.max(s_b, axis=0, keepdims=True))],
                                      axis=1)
                m_new = jnp.maximum(m_old, top)
                pv = _dot(lhs[:, :hk], jnp.exp2(s_a - m_new).astype(_BF16))
                pv_b = _dot(lhs[:, hk:], jnp.exp2(s_b - m_new[:, hq:]).astype(_BF16))
                pv = jnp.concatenate([pv[:, :hq], pv[:, hq:] + pv_b], axis=1)
            acc_ref[h] = jnp.exp2(m_old - m_new) * acc_ref[h] + pv
            m_ref[h] = m_new

    @pl.when(ki < qi)
    def _off_diagonal():
        step_body(False)

    @pl.when(ki == qi)
    def _diagonal():
        step_body(True)
        for hp in range(n_heads // 2):
            outs = []
            for h in (2 * hp, 2 * hp + 1):
                a = acc_ref[h]
                outs.append(a[0:head_dim] / a[head_dim:head_dim + 1])
            o_ref[:, hp * pair:(hp + 1) * pair] = jnp.concatenate(outs, axis=0).T

    @pl.when(step == pl.num_programs(1) - 1)
    def _paged_finish():
        for side in paged:
            side.finish()


def _suffix_matrix(page):
    idx = np.arange(page)
    strict = (idx[:, None] > idx[None, :]).astype(np.float32)
    return np.concatenate([strict, np.ones_like(strict)], axis=1)


def _attention(qs, augq, kb, augk, vt, batch, seq, q_s, k_new, v_new, cn, cache_kt, cache_vt, cache_lft,
               page_table, t_new, n_heads, head_dim):
    aw = n_heads * head_dim
    bq = bk = ATTN_BLOCK
    nq = seq // bq
    qs3 = qs.reshape(batch, seq, aw)
    kb3 = kb.reshape(batch, seq, aw)
    aq3 = augq.reshape(batch, seq, V7X_LANES)
    ak3 = augk.reshape(batch, seq, V7X_LANES)
    pairs = [(qi, ki) for qi in range(nq) for ki in range(qi + 1)]
    qi_tab = jnp.asarray([p[0] for p in pairs], jnp.int32)
    ki_tab = jnp.asarray([p[1] for p in pairs], jnp.int32)
    n_steps = len(pairs)

    db, n_pages = page_table.shape
    page = cache_kt.shape[2]
    assert db % batch == 0, "sample batches must split evenly over the prompt batches"
    n_samp = db // batch
    pps = -(-n_pages // n_steps)
    rows = t_new * n_heads
    pad_t = V7X_SUBLANES - t_new
    q3 = q_s.reshape(db, t_new, aw)
    kn = jnp.pad(k_new.reshape(db, t_new, aw), ((0, 0), (0, pad_t), (0, 0)))
    vn = jnp.pad(v_new.reshape(db, t_new, aw), ((0, 0), (0, pad_t), (0, 0)))
    cn3 = cn.reshape(db, t_new, n_heads)
    cnq = cn3.reshape(db, rows, 1)
    cnk = jnp.tile(jnp.swapaxes(cn3, 1, 2), (1, t_new, 1))
    cnk = jnp.pad(cnk, ((0, 0), (0, 0), (0, page - t_new)))
    uo = jnp.asarray(_suffix_matrix(page), _BF16)

    qmap = lambda b, s, qt, kt, pt: (b, qt[s], 0)
    kmap = lambda b, s, qt, kt, pt: (b, kt[s], 0)

    def samp_spec(shape, u):
        return pl.BlockSpec((None,) + shape, lambda b, s, qt, kt, pt: (b * n_samp + u, 0, 0))

    small_specs, small_args = [], []
    for u in range(n_samp):
        small_specs += [samp_spec((t_new, aw), u), samp_spec((V7X_SUBLANES, aw), u),
                        samp_spec((V7X_SUBLANES, aw), u), samp_spec((rows, 1), u), samp_spec((rows, page), u)]
        small_args += [q3, kn, vn, cnq, cnk]
    caches = (cache_kt, cache_vt, cache_lft)

    grid_spec = pltpu.PrefetchScalarGridSpec(
        num_scalar_prefetch=3,
        grid=(batch, n_steps),
        in_specs=[
            pl.BlockSpec((None, bq, aw), qmap),
            pl.BlockSpec((None, bq, V7X_LANES), qmap),
            pl.BlockSpec((None, bk, aw), kmap),
            pl.BlockSpec((None, bk, V7X_LANES), kmap),
            pl.BlockSpec((None, aw, bk), lambda b, s, qt, kt, pt: (b, 0, kt[s])),
        ] + small_specs + [pl.BlockSpec(memory_space=pl.ANY)] * len(caches)
        + [pl.BlockSpec(uo.shape, lambda b, s, qt, kt, pt: (0, 0))],
        out_specs=[pl.BlockSpec((None, bq, aw), qmap)]
        + [pl.BlockSpec((None, t_new, aw), lambda b, s, qt, kt, pt: (b, 0, 0))] * n_samp,
        scratch_shapes=[
            pltpu.VMEM((n_heads, bq, 2 * V7X_LANES), _BF16),
            pltpu.VMEM((n_heads, 1, bq), _F32),
            pltpu.VMEM((n_heads, head_dim + 16, bq), _F32),
        ] + [
            pltpu.VMEM((rows, aw), _BF16),
            pltpu.VMEM((rows, 1), _F32),
            pltpu.VMEM((rows, 1), _F32),
            pltpu.VMEM((rows, aw), _F32),
            pltpu.VMEM((V7X_SUBLANES, page), _F32),
        ] * n_samp + [
            pltpu.VMEM((2, n_samp * pps) + c.shape[1:], c.dtype) for c in caches
        ] + [pltpu.SemaphoreType.DMA((2, len(caches)))],
    )
    outs = pl.pallas_call(
        functools.partial(_fox_kernel, n_heads=n_heads, head_dim=head_dim, n_samp=n_samp, pps=pps,
                          t_new=t_new, n_pages=n_pages),
        grid_spec=grid_spec,
        out_shape=[jax.ShapeDtypeStruct((batch, seq, aw), _F32)]
        + [jax.ShapeDtypeStruct((batch, t_new, aw), _F32)] * n_samp,
        compiler_params=_params(2),
        name="attention",
    )(qi_tab, ki_tab, page_table, qs3, aq3, kb3, ak3, vt, *small_args, *caches, uo)
    att = outs[0].reshape(batch * seq, aw)
    att_s = jnp.stack(outs[1:], axis=1)
    return att, att_s.reshape(db * t_new, aw)


def _rglru_sample_kernel(xr_ref, gr_ref, p1_ref, p2_ref, p3_ref, h0_ref, *refs, seg_len):
    rec_refs = refs[:7]
    y_ref, h_ref = refs[7:]
    tm, rw = xr_ref.shape
    xr = xr_ref[...]
    t = lax.broadcasted_iota(jnp.int32, (tm, rw), 0) & (seg_len - 1)
    prev = (p1_ref, p2_ref, p3_ref)
    shifted = [jnp.where(t >= k, pltpu.roll(xr, k, 0), prev[k - 1][...]) for k in range(1, CONV_WIDTH)]
    a, b = _rglru_coeffs(xr, shifted, rec_refs)
    d = 1
    while d < seg_len:
        a_new, b_new = _scan_step(a, b, pltpu.roll(a, d, 0), pltpu.roll(b, d, 0))
        a = jnp.where(t >= d, a_new, a)
        b = jnp.where(t >= d, b_new, b)
        d *= 2
    h = a * h0_ref[...] + b
    y_ref[...] = h * jax.nn.gelu(gr_ref[...])
    h_ref[...] = h


def _rglru_sample(xr, gr, prevs, h0, seg_len, rec_w):
    n, rw = xr.shape
    assert seg_len & (seg_len - 1) == 0 and n % seg_len == 0
    full = pl.BlockSpec((n, rw), lambda i: (0, 0))
    return pl.pallas_call(
        functools.partial(_rglru_sample_kernel, seg_len=seg_len),
        grid=(1,),
        in_specs=[full] * 6 + [_const_spec(c.shape) for c in rec_w],
        out_specs=(full, full),
        out_shape=(jax.ShapeDtypeStruct((n, rw), _F32), jax.ShapeDtypeStruct((n, rw), _F32)),
        compiler_params=_params(1),
        name="rglru_sample",
    )(xr, gr, *prevs, h0, *rec_w)


def _tail_math(x, att, rec, p, goa_ref, gor_ref, wout_ref, gffn_ref, wi_ref, wo_ref, wple_ref,
               gple_ref, wpg_ref):
    aw = att.shape[1]
    an = _rms(att, goa_ref[...]).astype(_BF16)
    rn = _rms(rec, gor_ref[...]).astype(_BF16)
    x1 = x + _dot(an, wout_ref[0:aw, :]) + _dot(rn, wout_ref[aw:, :])
    hn = _rms(x1, gffn_ref[...]).astype(_BF16)
    hidden = wo_ref.shape[0]
    ffn = None
    for c0 in range(0, hidden, FFN_CHUNK):
        cw = min(FFN_CHUNK, hidden - c0)
        g = _dot(hn, wi_ref[:, c0:c0 + cw])
        u = _dot(hn, wi_ref[:, hidden + c0:hidden + c0 + cw])
        part = _dot((jax.nn.silu(g) * u).astype(_BF16), wo_ref[c0:c0 + cw, :])
        ffn = part if ffn is None else ffn + part
    x2 = x1 + ffn
    gate = jax.nn.sigmoid(_dot(_rms(x2, gple_ref[...]).astype(_BF16), wpg_ref[...]))
    return x2 + _dot(p.astype(_BF16), wple_ref[...]) * gate


def _tail_kernel(x_ref, att_ref, rec_ref, p_ref, *refs):
    o_ref = refs[-1]
    o_ref[...] = _tail_math(x_ref[...], att_ref[...], rec_ref[...], p_ref[...], *refs[:-1])


def _tail(x2d, att, rec, p2d, tail_w):
    n, d = x2d.shape
    tm = min(PROJ_ROWS, n)
    row = lambda w: pl.BlockSpec((tm, w), lambda i: (i, 0))
    return pl.pallas_call(
        _tail_kernel,
        grid=(n // tm,),
        in_specs=[row(d), row(att.shape[1]), row(rec.shape[1]), row(p2d.shape[1])]
        + [_const_spec(c.shape) for c in tail_w],
        out_specs=row(d),
        out_shape=jax.ShapeDtypeStruct((n, d), _F32),
        compiler_params=_params(1),
        name="tail",
    )(x2d, att, rec, p2d, *tail_w)


def _sproj_kernel(x_ref, gmix_ref, w_ref, wf_ref, bf_ref, gq_ref, gk_ref, gmat_ref, seg_ref,
                  q_out, k_out, v_out, lf_out, cn_out, xr_out, gr_out, *, aw, rw):
    hn = _rms(x_ref[...], gmix_ref[...]).astype(_BF16)
    proj = _dot(hn, w_ref[...])
    gmat = gmat_ref[...]
    q_out[...] = _head_norm(proj[:, 0:aw], gq_ref[...], gmat)
    k_out[...] = _head_norm(proj[:, aw:2 * aw], gk_ref[...], gmat)
    v_out[...] = proj[:, 2 * aw:3 * aw]
    xr_out[...] = proj[:, 3 * aw:3 * aw + rw]
    gr_out[...] = proj[:, 3 * aw + rw:3 * aw + 2 * rw]
    lf = jax.nn.log_sigmoid(_dot(hn, wf_ref[...]) + bf_ref[...])
    lf_out[...] = lf
    hi, mid, lo = _split3(lf)
    seg = seg_ref[...]
    cn_out[...] = _dot(seg, hi.astype(_BF16)) + _dot(seg, mid.astype(_BF16)) + _dot(seg, lo.astype(_BF16))


def _proj_sample(x2d, t_new, gmix, w_main, wf_pad, bf_row, gq, gk, n_heads, head_dim, rw):
    n, d = x2d.shape
    aw = n_heads * head_dim
    idx = np.arange(n)
    seg = ((idx[:, None] // t_new == idx[None, :] // t_new) & (idx[None, :] <= idx[:, None])).astype(np.float32)
    ins = [x2d, gmix, w_main, wf_pad, bf_row, gq, gk,
           jnp.asarray(_group_mean_matrix(head_dim), _BF16), jnp.asarray(seg, _BF16)]
    full = lambda s: pl.BlockSpec(s, lambda i: (0,) * len(s))
    widths = (aw, aw, aw, V7X_LANES, V7X_LANES, rw, rw)
    return pl.pallas_call(
        functools.partial(_sproj_kernel, aw=aw, rw=rw),
        grid=(1,),
        in_specs=[full(a.shape) for a in ins],
        out_specs=tuple(full((n, w)) for w in widths),
        out_shape=tuple(jax.ShapeDtypeStruct((n, w), _F32) for w in widths),
        compiler_params=_params(1),
        name="proj_sample",
    )(*ins)


def _block_diag_pair(w):
    nb, dd, _ = w.shape
    per = V7X_MXU_DIM // dd
    tiles = []
    for half in range(nb // per):
        tile = jnp.zeros((V7X_MXU_DIM, V7X_MXU_DIM), w.dtype)
        for j in range(per):
            tile = lax.dynamic_update_slice(tile, w[half * per + j], (j * dd, j * dd))
        tiles.append(tile)
    return jnp.stack(tiles).astype(_BF16)


def kernel(x_prompt, x_sample, p_prompt, p_sample, cache_k, cache_v, cache_logf, state_conv, state_h, page_table, g_mix, w_in, b_f, g_q, g_k, w_conv, b_conv, w_a, b_a, w_x, b_x, lam, g_out_attn, g_out_rec, w_out, g_ffn, w_ffn_in, w_ffn_out, w_ple, g_ple, w_ple_gate):
    batch, seq, d_model = x_prompt.shape
    db, t_new, _ = x_sample.shape
    depth, n_heads, head_dim = g_q.shape
    aw = n_heads * head_dim
    rw = lam.shape[1]
    hidden = w_ffn_out.shape[1]
    n_phys, page = cache_k.shape[1], cache_k.shape[2]
    assert n_heads == V7X_SUBLANES and aw == 2 * V7X_MXU_DIM and rw == 2 * V7X_MXU_DIM
    assert seq % ATTN_BLOCK == 0 and seq % PROJ_ROWS == 0

    xp = x_prompt.reshape(batch * seq, d_model)
    xs = x_sample.reshape(db * t_new, d_model)
    outs = [[] for _ in range(10)]
    row = lambda a: a.reshape(1, -1)
    for l in range(depth):
        w_l = w_in[l]
        w_main = jnp.concatenate([w_l[:, :3 * aw], w_l[:, 3 * aw + n_heads:]], axis=1).astype(_BF16)
        w_f = w_l[:, 3 * aw:3 * aw + n_heads]
        wft = jnp.pad(w_f.T, ((0, 16 - n_heads), (0, 0))).astype(_BF16)
        wf_pad = jnp.pad(w_f, ((0, 0), (0, V7X_LANES - n_heads))).astype(_BF16)
        bf_col = b_f[l].reshape(n_heads, 1)
        bf_row = jnp.pad(b_f[l], (0, V7X_LANES - n_heads)).reshape(1, V7X_LANES)
        gmix = row(g_mix[l])
        gq = row(g_q[l]) * (head_dim ** -0.5)
        gk = row(g_k[l])
        rec_w = (w_conv[l], row(b_conv[l]), _block_diag_pair(w_a[l]), row(b_a[l]),
                 _block_diag_pair(w_x[l]), row(b_x[l]), row(lam[l]))
        tail_w = (row(g_out_attn[l]), row(g_out_rec[l]), w_out[l].astype(_BF16), row(g_ffn[l]),
                  w_ffn_in[l].astype(_BF16),
                  w_ffn_out[l].astype(_BF16), w_ple[l].astype(_BF16), row(g_ple[l]),
                  w_ple_gate[l].astype(_BF16))

        kt, vt32, lft, qs, kb, vt, augq, augk, rec, h_tiles, x_tiles = _proj_prompt(
            xp, seq, gmix, w_main, wft, bf_col, gq * LOG2_E, gk, rec_w, n_heads, head_dim, rw)
        q_s, k_s, v_s, lf_s, cn_s, xr_s, gr_s = _proj_sample(
            xs, t_new, gmix, w_main, wf_pad, bf_row, gq, gk, n_heads, head_dim, rw)
        lf_s = lf_s[:, :n_heads]
        cache_lft = jnp.swapaxes(cache_logf[l], 1, 2)
        cache_kt = cache_k[l].transpose(0, 2, 3, 1).reshape(n_phys, aw, page)
        cache_vt = cache_v[l].transpose(0, 2, 3, 1).reshape(n_phys, aw, page)
        att, att_s = _attention(qs, augq, kb, augk, vt, batch, seq, q_s, k_s, v_s, cn_s[:, :n_heads],
                                cache_kt, cache_vt, cache_lft, page_table, t_new, n_heads, head_dim)

        xp = _tail(xp, att, rec, p_prompt[l].reshape(batch * seq, -1), tail_w)
        outs[0].append(kt.reshape(batch, n_heads, head_dim, seq).transpose(0, 3, 1, 2))
        outs[1].append(vt32.reshape(batch, n_heads, head_dim, seq).transpose(0, 3, 1, 2))
        outs[2].append(lft.transpose(0, 2, 1))
        outs[3].append(x_tiles.reshape(batch, -1, V7X_SUBLANES, rw)[:, -1, V7X_SUBLANES - (CONV_WIDTH - 1):])
        outs[4].append(h_tiles.reshape(batch, -1, V7X_SUBLANES, rw)[:, -1, -1])

        hist = jnp.concatenate([state_conv[l], jnp.zeros((db, t_new, rw), _F32)], axis=1)
        prevs = tuple(hist[:, CONV_WIDTH - 1 - s:CONV_WIDTH - 1 - s + t_new].reshape(db * t_new, rw)
                      for s in range(1, CONV_WIDTH))
        h0 = jnp.repeat(state_h[l], t_new, axis=0)
        rec_s, h_s = _rglru_sample(xr_s, gr_s, prevs, h0, t_new, rec_w)
        xs = _tail(xs, att_s, rec_s, p_sample[l].reshape(db * t_new, -1), tail_w)
        outs[5].append(k_s.reshape(db, t_new, n_heads, head_dim))
        outs[6].append(v_s.reshape(db, t_new, n_heads, head_dim))
        outs[7].append(lf_s.reshape(db, t_new, n_heads))
        outs[8].append(xr_s.reshape(db, t_new, rw)[:, t_new - (CONV_WIDTH - 1):])
        outs[9].append(h_s.reshape(db, t_new, rw)[:, -1])
    return (xp.reshape(batch, seq, d_model), xs.reshape(db, t_new, d_model),
            *(jnp.stack(o) for o in outs))
```

```python
import functools

import numpy as np
import jax
import jax.numpy as jnp
from jax import lax
from jax.experimental import pallas as pl
from jax.experimental.pallas import tpu as pltpu

RG_C = 8.0
NORM_EPS = 1e-6
LOG2_E = 1.4426950408889634
CONV_WIDTH = 4

V7X_LANES = 128
V7X_SUBLANES = 8
V7X_BF16_SUBLANES = 16
V7X_MXU_DIM = 256
V7X_VMEM_LIMIT_BYTES = 56 * 1024 * 1024

PROJ_ROWS = 512
ATTN_BLOCK = 512
FOX_LOOKAHEAD = 4
FFN_CHUNK = 256
BIAS_LANES = 8

_F32 = jnp.float32
_BF16 = jnp.bfloat16
_NT = (((1,), (1,)), ((), ()))


def _dot(a, b):
    return jnp.dot(a, b, preferred_element_type=_F32)


def _dot_nt(a, b):
    return lax.dot_general(a, b, _NT, preferred_element_type=_F32)


def _rms(x, g):
    return x * lax.rsqrt(jnp.mean(x * x, axis=-1, keepdims=True) + NORM_EPS) * g


def _split3(z):
    hi = z.astype(_BF16).astype(_F32)
    r = z - hi
    mid = r.astype(_BF16).astype(_F32)
    lo = r - mid
    return hi, mid, lo


def _stack3(z, rows):
    hi, mid, lo = _split3(z)
    pad = jnp.zeros((rows - 3 * V7X_SUBLANES, z.shape[1]), _F32)
    return jnp.concatenate([hi, mid, lo, pad], axis=0).astype(_BF16)


def _sum3(r):
    s = V7X_SUBLANES
    return r[0:s] + r[s:2 * s] + r[2 * s:3 * s]


def _head_norm(z, g, gmat):
    zz = (z * z).astype(_BF16)
    half = gmat.shape[0]
    ms = jnp.concatenate([_dot(zz[:, :half], gmat), _dot(zz[:, half:], gmat)], axis=1)
    return z * lax.rsqrt(ms + NORM_EPS) * g


def _const_spec(shape):
    nd = len(shape)
    return pl.BlockSpec(shape, lambda *_: (0,) * nd, pipeline_mode=pl.Buffered(1))


def _params(n_axes):
    return pltpu.CompilerParams(
        dimension_semantics=("arbitrary",) * n_axes,
        vmem_limit_bytes=V7X_VMEM_LIMIT_BYTES,
    )


def _sigmoid(z):
    return 0.5 * jnp.tanh(0.5 * z) + 0.5


def _rglru_coeffs(xr, shifted, rec_refs):
    wconv_ref, bconv_ref, wa_ref, ba_ref, wx_ref, bx_ref, lam_ref = rec_refs
    w = wconv_ref[...]
    xc = bconv_ref[...] + shifted[2] * w[0:1]
    xc = xc + shifted[1] * w[1:2]
    xc = xc + shifted[0] * w[2:3]
    xc = xc + xr * w[3:4]

    xcb = xc.astype(_BF16)
    half = wa_ref.shape[1]

    def gate(w_ref, b_ref):
        z = jnp.concatenate([_dot(xcb[:, :half], w_ref[0]), _dot(xcb[:, half:], w_ref[1])], axis=1)
        return _sigmoid(z + b_ref[...])

    r_gate = gate(wa_ref, ba_ref)
    i_gate = gate(wx_ref, bx_ref)
    log_a = -RG_C * r_gate * jax.nn.softplus(-lam_ref[...])
    a = jnp.exp(log_a)
    om = -jnp.tanh(log_a) * (a * a + 1.0)
    root = jnp.where(om > 0.0, om * lax.rsqrt(om), 0.0)
    return a, root * (i_gate * xc)


def _scan_step(a, b, a_s, b_s):
    return a * a_s, b + a * b_s


def _rglru_prompt_tile(xr, gr, hc, rec_refs, xbuf_ref, abuf_ref, bbuf_ref):
    tm, rw = xr.shape
    sub = V7X_SUBLANES
    xbuf_ref[sub:sub + tm] = xr
    shifted = [xbuf_ref[sub - k:sub - k + tm] for k in range(1, CONV_WIDTH)]
    a, b = _rglru_coeffs(xr, shifted, rec_refs)
    abuf_ref[0:sub] = jnp.ones((sub, rw), _F32)
    bbuf_ref[0:sub] = jnp.zeros((sub, rw), _F32)
    d = 1
    while d < tm:
        if d % sub == 0:
            a_new, b_new = _scan_step(a[d:], b[d:], a[:tm - d], b[:tm - d])
            a = jnp.concatenate([a[:d], a_new], axis=0)
            b = jnp.concatenate([b[:d], b_new], axis=0)
        else:
            abuf_ref[sub:sub + tm] = a
            bbuf_ref[sub:sub + tm] = b
            a, b = _scan_step(a, b, abuf_ref[sub - d:sub - d + tm], bbuf_ref[sub - d:sub - d + tm])
        d *= 2
    h = a * hc + b
    return h * jax.nn.gelu(gr), h


def _proj_kernel(x_ref, gmix_ref, w_ref, wft_ref, bf_ref, gq_ref, gk_ref, gmat_ref, cum_ref,
                 aq_ref, ak_ref, cq_ref, ck_ref, *refs, tiles_per_seq, aw, rw):
    rec_refs = refs[:7]
    (k_out, v_out, lft_out, qs_out, kb_out, vt_out, augq_out, augk_out, rec_out, h_out, xtail_out,
     carry_ref, hc_ref, xbuf_ref, abuf_ref, bbuf_ref) = refs[7:]
    i = pl.program_id(0)
    tm = x_ref.shape[0]
    sub = V7X_SUBLANES
    seq_start = i % tiles_per_seq == 0

    @pl.when(seq_start)
    def _():
        carry_ref[...] = jnp.zeros_like(carry_ref)
        hc_ref[...] = jnp.zeros_like(hc_ref)
        xbuf_ref[0:sub] = jnp.zeros((sub, rw), _F32)

    @pl.when(jnp.logical_not(seq_start))
    def _():
        xbuf_ref[0:sub] = xbuf_ref[tm:tm + sub]

    hn = _rms(x_ref[...], gmix_ref[...]).astype(_BF16)
    w_rec = 3 * aw
    proj_rec = _dot(hn, w_ref[:, w_rec:w_rec + 2 * rw])
    xr = proj_rec[:, 0:rw]
    xtail_out[...] = xr[tm - sub:tm]

    y, h = _rglru_prompt_tile(xr, proj_rec[:, rw:2 * rw], hc_ref[...], rec_refs, xbuf_ref, abuf_ref, bbuf_ref)
    hc_ref[...] = h[tm - 1:tm]
    rec_out[...] = y
    h_out[...] = h[tm - sub:tm]

    proj = _dot(hn, w_ref[:, 0:w_rec])
    gmat = gmat_ref[...]
    qn = _head_norm(proj[:, 0:aw], gq_ref[...], gmat)
    kn = _head_norm(proj[:, aw:2 * aw], gk_ref[...], gmat)
    vt = proj[:, 2 * aw:3 * aw].T
    k_out[...] = kn.T
    v_out[...] = vt
    qs_out[...] = qn.astype(_BF16)
    kb_out[...] = kn.astype(_BF16)
    vt_out[...] = vt.astype(_BF16)

    ft = _dot_nt(wft_ref[...], hn)
    lft = jax.nn.log_sigmoid(ft[0:8] + bf_ref[...])
    lft_out[...] = lft

    p3 = _stack3(lft, V7X_LANES)
    blk = cum_ref.shape[0]
    carry = carry_ref[...]
    cums = []
    for c in range(tm // blk):
        r = _dot(p3[:, c * blk:(c + 1) * blk], cum_ref[...])
        cums.append(_sum3(r[:, :blk]) + carry)
        carry = carry + _sum3(r[:, blk:])
    carry_ref[...] = carry
    cum = jnp.concatenate(cums, axis=1)

    c3 = _stack3(cum * LOG2_E, V7X_LANES)
    qat = _dot(aq_ref[...], c3) + cq_ref[...]
    kat = _dot(ak_ref[...], c3) + ck_ref[...]
    augq_out[...] = qat.T.astype(_BF16)
    augk_out[...] = kat.T.astype(_BF16)


def _bias_placement(n_heads):
    aq = np.zeros((V7X_LANES, V7X_LANES), np.float32)
    ak = np.zeros((V7X_LANES, V7X_LANES), np.float32)
    cq = np.zeros((V7X_LANES, 1), np.float32)
    ck = np.zeros((V7X_LANES, 1), np.float32)
    for h in range(n_heads):
        for j in range(3):
            aq[h * BIAS_LANES + j, j * V7X_SUBLANES + h] = 1.0
            cq[h * BIAS_LANES + 3 + j, 0] = 1.0
            ak[h * BIAS_LANES + 3 + j, j * V7X_SUBLANES + h] = -1.0
            ck[h * BIAS_LANES + j, 0] = 1.0
    return aq, ak, cq, ck


def _group_mean_matrix(head_dim):
    idx = np.arange(V7X_MXU_DIM) // head_dim
    return (idx[:, None] == idx[None, :]).astype(np.float32) / head_dim


def _prefix_matrix():
    idx = np.arange(V7X_MXU_DIM)
    incl = (idx[:, None] <= idx[None, :]).astype(np.float32)
    return np.concatenate([incl, np.ones_like(incl)], axis=1)


def _proj_prompt(x2d, seq, gmix, w_main, wft, bf_col, gq, gk, rec_w, n_heads, head_dim, rw):
    n, d = x2d.shape
    tm = PROJ_ROWS
    aw = n_heads * head_dim
    aq, ak, cq, ck = _bias_placement(n_heads)
    consts = [
        gmix, w_main, wft, bf_col, gq, gk,
        jnp.asarray(_group_mean_matrix(head_dim), _BF16),
        jnp.asarray(_prefix_matrix(), _BF16),
        jnp.asarray(aq, _BF16), jnp.asarray(ak, _BF16), jnp.asarray(cq), jnp.asarray(ck),
        *rec_w,
    ]
    tps = seq // tm
    batch = n // seq
    row = lambda w: pl.BlockSpec((tm, w), lambda i: (i, 0))
    seq_t = lambda r: pl.BlockSpec((None, r, tm), lambda i: (i // tps, 0, i % tps))
    out_shape = (
        jax.ShapeDtypeStruct((batch, aw, seq), _F32),
        jax.ShapeDtypeStruct((batch, aw, seq), _F32),
        jax.ShapeDtypeStruct((batch, V7X_SUBLANES, seq), _F32),
        jax.ShapeDtypeStruct((n, aw), _BF16),
        jax.ShapeDtypeStruct((n, aw), _BF16),
        jax.ShapeDtypeStruct((batch, aw, seq), _BF16),
        jax.ShapeDtypeStruct((n, V7X_LANES), _BF16),
        jax.ShapeDtypeStruct((n, V7X_LANES), _BF16),
        jax.ShapeDtypeStruct((n, rw), _F32),
        jax.ShapeDtypeStruct((n // tm * V7X_SUBLANES, rw), _F32),
        jax.ShapeDtypeStruct((n // tm * V7X_SUBLANES, rw), _F32),
    )
    tail8 = pl.BlockSpec((V7X_SUBLANES, rw), lambda i: (i, 0))
    out_specs = (
        seq_t(aw), seq_t(aw), seq_t(V7X_SUBLANES), row(aw), row(aw), seq_t(aw),
        row(V7X_LANES), row(V7X_LANES), row(rw), tail8, tail8,
    )
    return pl.pallas_call(
        functools.partial(_proj_kernel, tiles_per_seq=tps, aw=aw, rw=rw),
        grid=(n // tm,),
        in_specs=[row(d)] + [_const_spec(c.shape) for c in consts],
        out_specs=out_specs,
        out_shape=out_shape,
        scratch_shapes=[pltpu.VMEM((V7X_SUBLANES, V7X_MXU_DIM), _F32), pltpu.VMEM((1, rw), _F32)]
        + [pltpu.VMEM((V7X_SUBLANES + tm, rw), _F32)] * 3,
        compiler_params=_params(1),
        name="proj_prompt",
    )(x2d, *consts)


class _PagedSide:
    def __init__(self, small, pages, pps, page, uo_ref, o_ref, scratch, t_new, head_dim, n_pages):
        self.q_ref, self.kn_ref, self.vn_ref, self.cnq_ref, self.cnk_ref = small
        self.k_page, self.v_page, self.lf_page = pages
        self.pps, self.page, self.uo_ref, self.o_ref = pps, page, uo_ref, o_ref
        self.qbd_ref, self.m_ref, self.l_ref, self.acc_ref, self.carry_ref = scratch
        self.t_new, self.head_dim, self.n_pages = t_new, head_dim, n_pages
        self.nh = V7X_SUBLANES
        self.aw = self.q_ref.shape[1]

    def _head_mask(self):
        feat = lax.broadcasted_iota(jnp.int32, (self.nh, self.aw), 1)
        head = lax.broadcasted_iota(jnp.int32, (self.nh, self.aw), 0)
        return (feat >= head * self.head_dim) & (feat < (head + 1) * self.head_dim)

    def init(self):
        nh, aw, page, t_new = self.nh, self.aw, self.page, self.t_new
        rows = t_new * nh
        q = self.q_ref[...]
        head_mask = self._head_mask()
        qbd = jnp.concatenate(
            [jnp.where(head_mask, jnp.broadcast_to(q[t:t + 1], (nh, aw)), 0.0) for t in range(t_new)],
            axis=0).astype(_BF16)
        self.qbd_ref[...] = qbd
        pad = jnp.zeros((page - self.kn_ref.shape[0], aw), _F32)
        kn = jnp.concatenate([self.kn_ref[...], pad], axis=0).astype(_BF16)
        vn = jnp.concatenate([self.vn_ref[...], pad], axis=0).astype(_BF16)
        s = _dot_nt(qbd, kn) + self.cnq_ref[...] - self.cnk_ref[...]
        row = lax.broadcasted_iota(jnp.int32, (rows, page), 0)
        t_col = lax.broadcasted_iota(jnp.int32, (rows, page), 1)
        s = jnp.where(t_col * nh <= row, s, -jnp.inf)
        m = jnp.max(s, axis=1, keepdims=True)
        p = jnp.exp(s - m)
        self.m_ref[...] = m
        self.l_ref[...] = jnp.sum(p, axis=1, keepdims=True)
        self.acc_ref[...] = _dot(p.astype(_BF16), vn)
        self.carry_ref[...] = jnp.zeros_like(self.carry_ref)

    def scores(self, step):
        pps, page = self.pps, self.page
        qbd = self.qbd_ref[...]
        carry = self.carry_ref[...]
        scores = []
        for i in range(pps):
            valid = step * pps + i < self.n_pages
            s = _dot(qbd, self.k_page(i).astype(_BF16))
            lf = jnp.where(valid, self.lf_page(i), 0.0)
            r = _dot(_stack3(lf, 4 * V7X_SUBLANES), self.uo_ref[...])
            bias = _sum3(r[:, :page]) + carry
            carry = carry + _sum3(r[:, page:])
            s = s + jnp.concatenate([bias] * self.t_new, axis=0)
            scores.append(jnp.where(valid, s, -jnp.inf))
        self.carry_ref[...] = carry
        s_all = jnp.concatenate(scores, axis=1) + self.cnq_ref[...]
        m_old = self.m_ref[...]
        m_new = jnp.maximum(m_old, jnp.max(s_all, axis=1, keepdims=True))
        alpha = jnp.exp(m_old - m_new)
        p = jnp.exp(s_all - m_new)
        self.l_ref[...] = alpha * self.l_ref[...] + jnp.sum(p, axis=1, keepdims=True)
        self.m_ref[...] = m_new
        return p.astype(_BF16), alpha

    def values(self, p, alpha):
        vt_all = jnp.concatenate([self.v_page(i).astype(_BF16) for i in range(self.pps)], axis=1)
        self.acc_ref[...] = alpha * self.acc_ref[...] + _dot_nt(p, vt_all)

    def finish(self):
        nh = self.nh
        a = self.acc_ref[...] / self.l_ref[...]
        head_mask = self._head_mask()
        for t in range(self.t_new):
            self.o_ref[t:t + 1, :] = jnp.sum(jnp.where(head_mask, a[t * nh:(t + 1) * nh], 0.0),
                                             axis=0, keepdims=True)


def _fox_kernel(qi_tab, ki_tab, pt_ref, q_ref, aq_ref, k_ref, ak_ref, vt_ref, *refs,
                n_heads, head_dim, n_samp, pps, t_new, n_pages):
    n_small = 5
    small = refs[:n_small * n_samp]
    rest = refs[n_small * n_samp:]
    cache_refs = rest[0:3]
    uo_ref, o_ref = rest[3], rest[4]
    so_refs = rest[5:5 + n_samp]
    qf_ref, m_ref, acc_ref = rest[5 + n_samp:8 + n_samp]
    s_scratch = rest[8 + n_samp:8 + 6 * n_samp]
    page_bufs = rest[8 + 6 * n_samp:11 + 6 * n_samp]
    sem = rest[11 + 6 * n_samp]

    batch_i = pl.program_id(0)
    step = pl.program_id(1)
    n_steps = pl.num_programs(1)
    g = batch_i * n_steps + step
    slot = g % 2
    page = page_bufs[0].shape[3]

    def page_copies(b2, s2, dst_slot, lookup):
        copies = []
        for u in range(n_samp):
            for i in range(pps):
                pid = 0
                if lookup:
                    pid = pt_ref[b2 * n_samp + u, jnp.maximum(n_pages - 1 - (s2 * pps + i), 0)]
                for c, (cache, buf) in enumerate(zip(cache_refs, page_bufs)):
                    copies.append(pltpu.make_async_copy(cache.at[pid], buf.at[dst_slot, u * pps + i],
                                                        sem.at[dst_slot, c]))
        return copies

    @pl.when(g == 0)
    def _first_pages():
        for c in page_copies(0, 0, 0, True):
            c.start()

    @pl.when(g + 1 < pl.num_programs(0) * n_steps)
    def _next_pages():
        wrap = step + 1 == n_steps
        for c in page_copies(jnp.where(wrap, batch_i + 1, batch_i), jnp.where(wrap, 0, step + 1),
                             1 - slot, True):
            c.start()

    for c in page_copies(batch_i, step, slot, False):
        c.wait()

    def reader(buf, u):
        return lambda i: buf[slot, u * pps + i]

    paged = [
        _PagedSide(small[u * n_small:(u + 1) * n_small], [reader(buf, u) for buf in page_bufs], pps, page,
                   uo_ref, so_refs[u], s_scratch[u * 5:(u + 1) * 5], t_new, head_dim, n_pages)
        for u in range(n_samp)]

    qi = qi_tab[step]
    ki = ki_tab[step]
    bq = q_ref.shape[0]
    bk = k_ref.shape[0]
    pair = 2 * head_dim

    @pl.when(step == 0)
    def _paged_init():
        for side in paged:
            side.init()

    @pl.when(ki == 0)
    def _init():
        lane = lax.broadcasted_iota(jnp.int32, (bq, V7X_LANES), 1)
        a = aq_ref[...]
        zero = jnp.zeros_like(a)
        for h in range(n_heads):
            q2 = q_ref[:, (h // 2) * pair:(h // 2 + 1) * pair]
            lo = head_dim * (h % 2)
            qh = jnp.where((lane >= lo) & (lane < lo + head_dim), q2, zero)
            ah = jnp.where((lane >= h * BIAS_LANES) & (lane < (h + 1) * BIAS_LANES), a, zero)
            qf_ref[h] = jnp.concatenate([qh, ah], axis=1)
        m_ref[...] = jnp.full_like(m_ref, -jnp.inf)
        acc_ref[...] = jnp.zeros_like(acc_ref)

    def step_body(masked):
        ak = ak_ref[...]
        hk, hq = bk // 2, bq // 2
        ones_blk = (lax.broadcasted_iota(jnp.int32, (V7X_BF16_SUBLANES, bk), 0) == 0).astype(_BF16)
        if masked:
            tri = (lax.broadcasted_iota(jnp.int32, (hk, hq), 0)
                   <= lax.broadcasted_iota(jnp.int32, (hk, hq), 1))

        def scores(h):
            hp = h // 2
            kf = jnp.concatenate([k_ref[:, hp * pair:(hp + 1) * pair], ak], axis=1)
            if not masked:
                return _dot_nt(kf, qf_ref[h])
            return (_dot_nt(kf[:hk], qf_ref[h]),
                    _dot_nt(kf[hk:], qf_ref[h, hq:, :]))

        paged_p = [side.scores(step) for side in paged]
        ahead = [scores(h) for h in range(FOX_LOOKAHEAD)]
        for side, (p_s, alpha_s) in zip(paged, paged_p):
            side.values(p_s, alpha_s)
        for h in range(n_heads):
            s = ahead.pop(0)
            if h + FOX_LOOKAHEAD < n_heads:
                ahead.append(scores(h + FOX_LOOKAHEAD))
            m_old = m_ref[h]
            lhs = jnp.concatenate([vt_ref[h * head_dim:(h + 1) * head_dim, :], ones_blk], axis=0)
            if not masked:
                m_new = jnp.maximum(m_old, jnp.max(s, axis=0, keepdims=True))
                pv = _dot(lhs, jnp.exp2(s - m_new).astype(_BF16))
            else:
                s_a, s_b = s
                s_a = jnp.concatenate([jnp.where(tri, s_a[:, :hq], -jnp.inf), s_a[:, hq:]], axis=1)
                s_b = jnp.where(tri, s_b, -jnp.inf)
                top = jnp.max(s_a, axis=0, keepdims=True)
                top = jnp.concatenate([top[:, :hq], jnp.maximum(top[:, hq:], jnp.max(s_b, axis=0, keepdims=True))],
                                      axis=1)
                m_new = jnp.maximum(m_old, top)
                pv = _dot(lhs[:, :hk], jnp.exp2(s_a - m_new).astype(_BF16))
                pv_b = _dot(lhs[:, hk:], jnp.exp2(s_b - m_new[:, hq:]).astype(_BF16))
                pv = jnp.concatenate([pv[:, :hq], pv[:, hq:] + pv_b], axis=1)
            acc_ref[h] = jnp.exp2(m_old - m_new) * acc_ref[h] + pv
            m_ref[h] = m_new

    @pl.when(ki < qi)
    def _off_diagonal():
        step_body(False)

    @pl.when(ki == qi)
    def _diagonal():
        step_body(True)
        for hp in range(n_heads // 2):
            outs = []
            for h in (2 * hp, 2 * hp + 1):
                a = acc_ref[h]
                outs.append(a[0:head_dim] / a[head_dim:head_dim + 1])
            o_ref[:, hp * pair:(hp + 1) * pair] = jnp.concatenate(outs, axis=0).T

    @pl.when(step == pl.num_programs(1) - 1)
    def _paged_finish():
        for side in paged:
            side.finish()


def _suffix_matrix(page):
    idx = np.arange(page)
    strict = (idx[:, None] > idx[None, :]).astype(np.float32)
    return np.concatenate([strict, np.ones_like(strict)], axis=1)


def _attention(qs, augq, kb, augk, vt, batch, seq, q_s, k_new, v_new, cn, cache_kt, cache_vt, cache_lft,
               page_table, t_new, n_heads, head_dim):
    aw = n_heads * head_dim
    bq = bk = ATTN_BLOCK
    nq = seq // bq
    qs3 = qs.reshape(batch, seq, aw)
    kb3 = kb.reshape(batch, seq, aw)
    aq3 = augq.reshape(batch, seq, V7X_LANES)
    ak3 = augk.reshape(batch, seq, V7X_LANES)
    pairs = [(qi, ki) for qi in range(nq) for ki in range(qi + 1)]
    qi_tab = jnp.asarray([p[0] for p in pairs], jnp.int32)
    ki_tab = jnp.asarray([p[1] for p in pairs], jnp.int32)
    n_steps = len(pairs)

    db, n_pages = page_table.shape
    page = cache_kt.shape[2]
    assert db % batch == 0, "sample batches must split evenly over the prompt batches"
    n_samp = db // batch
    pps = -(-n_pages // n_steps)
    rows = t_new * n_heads
    pad_t = V7X_SUBLANES - t_new
    q3 = q_s.reshape(db, t_new, aw)
    kn = jnp.pad(k_new.reshape(db, t_new, aw), ((0, 0), (0, pad_t), (0, 0)))
    vn = jnp.pad(v_new.reshape(db, t_new, aw), ((0, 0), (0, pad_t), (0, 0)))
    cn3 = cn.reshape(db, t_new, n_heads)
    cnq = cn3.reshape(db, rows, 1)
    cnk = jnp.tile(jnp.swapaxes(cn3, 1, 2), (1, t_new, 1))
    cnk = jnp.pad(cnk, ((0, 0), (0, 0), (0, page - t_new)))
    uo = jnp.asarray(_suffix_matrix(page), _BF16)

    qmap = lambda b, s, qt, kt, pt: (b, qt[s], 0)
    kmap = lambda b, s, qt, kt, pt: (b, kt[s], 0)

    def samp_spec(shape, u):
        return pl.BlockSpec((None,) + shape, lambda b, s, qt, kt, pt: (b * n_samp + u, 0, 0))

    small_specs, small_args = [], []
    for u in range(n_samp):
        small_specs += [samp_spec((t_new, aw), u), samp_spec((V7X_SUBLANES, aw), u),
                        samp_spec((V7X_SUBLANES, aw), u), samp_spec((rows, 1), u), samp_spec((rows, page), u)]
        small_args += [q3, kn, vn, cnq, cnk]
    caches = (cache_kt, cache_vt, cache_lft)

    grid_spec = pltpu.PrefetchScalarGridSpec(
        num_scalar_prefetch=3,
        grid=(batch, n_steps),
        in_specs=[
            pl.BlockSpec((None, bq, aw), qmap),
            pl.BlockSpec((None, bq, V7X_LANES), qmap),
            pl.BlockSpec((None, bk, aw), kmap),
            pl.BlockSpec((None, bk, V7X_LANES), kmap),
            pl.BlockSpec((None, aw, bk), lambda b, s, qt, kt, pt: (b, 0, kt[s])),
        ] + small_specs + [pl.BlockSpec(memory_space=pl.ANY)] * len(caches)
        + [pl.BlockSpec(uo.shape, lambda b, s, qt, kt, pt: (0, 0))],
        out_specs=[pl.BlockSpec((None, bq, aw), qmap)]
        + [pl.BlockSpec((None, t_new, aw), lambda b, s, qt, kt, pt: (b, 0, 0))] * n_samp,
        scratch_shapes=[
            pltpu.VMEM((n_heads, bq, 2 * V7X_LANES), _BF16),
            pltpu.VMEM((n_heads, 1, bq), _F32),
            pltpu.VMEM((n_heads, head_dim + V7X_BF16_SUBLANES, bq), _F32),
        ] + [
            pltpu.VMEM((rows, aw), _BF16),
            pltpu.VMEM((rows, 1), _F32),
            pltpu.VMEM((rows, 1), _F32),
            pltpu.VMEM((rows, aw), _F32),
            pltpu.VMEM((V7X_SUBLANES, page), _F32),
        ] * n_samp + [
            pltpu.VMEM((2, n_samp * pps) + c.shape[1:], c.dtype) for c in caches
        ] + [pltpu.SemaphoreType.DMA((2, len(caches)))],
    )
    outs = pl.pallas_call(
        functools.partial(_fox_kernel, n_heads=n_heads, head_dim=head_dim, n_samp=n_samp, pps=pps,
                          t_new=t_new, n_pages=n_pages),
        grid_spec=grid_spec,
        out_shape=[jax.ShapeDtypeStruct((batch, seq, aw), _F32)]
        + [jax.ShapeDtypeStruct((batch, t_new, aw), _F32)] * n_samp,
        compiler_params=_params(2),
        name="attention",
    )(qi_tab, ki_tab, page_table, qs3, aq3, kb3, ak3, vt, *small_args, *caches, uo)
    att = outs[0].reshape(batch * seq, aw)
    att_s = jnp.stack(outs[1:], axis=1)
    return att, att_s.reshape(db * t_new, aw)


def _rglru_sample_kernel(xr_ref, gr_ref, p1_ref, p2_ref, p3_ref, h0_ref, *refs, seg_len):
    rec_refs = refs[:7]
    y_ref, h_ref = refs[7:]
    tm, rw = xr_ref.shape
    xr = xr_ref[...]
    t = lax.broadcasted_iota(jnp.int32, (tm, rw), 0) & (seg_len - 1)
    prev = (p1_ref, p2_ref, p3_ref)
    shifted = [jnp.where(t >= k, pltpu.roll(xr, k, 0), prev[k - 1][...]) for k in range(1, CONV_WIDTH)]
    a, b = _rglru_coeffs(xr, shifted, rec_refs)
    d = 1
    while d < seg_len:
        a_new, b_new = _scan_step(a, b, pltpu.roll(a, d, 0), pltpu.roll(b, d, 0))
        a = jnp.where(t >= d, a_new, a)
        b = jnp.where(t >= d, b_new, b)
        d *= 2
    h = a * h0_ref[...] + b
    y_ref[...] = h * jax.nn.gelu(gr_ref[...])
    h_ref[...] = h


def _rglru_sample(xr, gr, prevs, h0, seg_len, rec_w):
    n, rw = xr.shape
    assert seg_len & (seg_len - 1) == 0 and n % seg_len == 0
    full = pl.BlockSpec((n, rw), lambda i: (0, 0))
    return pl.pallas_call(
        functools.partial(_rglru_sample_kernel, seg_len=seg_len),
        grid=(1,),
        in_specs=[full] * 6 + [_const_spec(c.shape) for c in rec_w],
        out_specs=(full, full),
        out_shape=(jax.ShapeDtypeStruct((n, rw), _F32), jax.ShapeDtypeStruct((n, rw), _F32)),
        compiler_params=_params(1),
        name="rglru_sample",
    )(xr, gr, *prevs, h0, *rec_w)


def _tail_math(x, att, rec, p, goa_ref, gor_ref, wout_ref, gffn_ref, wi_ref, wo_ref, wple_ref,
               gple_ref, wpg_ref):
    aw = att.shape[1]
    an = _rms(att, goa_ref[...]).astype(_BF16)
    rn = _rms(rec, gor_ref[...]).astype(_BF16)
    x1 = x + _dot(an, wout_ref[0:aw, :]) + _dot(rn, wout_ref[aw:, :])
    hn = _rms(x1, gffn_ref[...]).astype(_BF16)
    hidden = wo_ref.shape[0]
    ffn = None
    for c0 in range(0, hidden, FFN_CHUNK):
        cw = min(FFN_CHUNK, hidden - c0)
        g = _dot(hn, wi_ref[:, c0:c0 + cw])
        u = _dot(hn, wi_ref[:, hidden + c0:hidden + c0 + cw])
        part = _dot((jax.nn.silu(g) * u).astype(_BF16), wo_ref[c0:c0 + cw, :])
        ffn = part if ffn is None else ffn + part
    x2 = x1 + ffn
    gate = jax.nn.sigmoid(_dot(_rms(x2, gple_ref[...]).astype(_BF16), wpg_ref[...]))
    return x2 + _dot(p.astype(_BF16), wple_ref[...]) * gate


def _tail_kernel(x_ref, att_ref, rec_ref, p_ref, *refs):
    o_ref = refs[-1]
    o_ref[...] = _tail_math(x_ref[...], att_ref[...], rec_ref[...], p_ref[...], *refs[:-1])


def _tail(x2d, att, rec, p2d, tail_w):
    n, d = x2d.shape
    tm = min(PROJ_ROWS, n)
    row = lambda w: pl.BlockSpec((tm, w), lambda i: (i, 0))
    return pl.pallas_call(
        _tail_kernel,
        grid=(n // tm,),
        in_specs=[row(d), row(att.shape[1]), row(rec.shape[1]), row(p2d.shape[1])]
        + [_const_spec(c.shape) for c in tail_w],
        out_specs=row(d),
        out_shape=jax.ShapeDtypeStruct((n, d), _F32),
        compiler_params=_params(1),
        name="tail",
    )(x2d, att, rec, p2d, *tail_w)


def _sproj_kernel(x_ref, gmix_ref, w_ref, wf_ref, bf_ref, gq_ref, gk_ref, gmat_ref, seg_ref,
                  q_out, k_out, v_out, lf_out, cn_out, xr_out, gr_out, *, aw, rw):
    hn = _rms(x_ref[...], gmix_ref[...]).astype(_BF16)
    proj = _dot(hn, w_ref[...])
    gmat = gmat_ref[...]
    q_out[...] = _head_norm(proj[:, 0:aw], gq_ref[...], gmat)
    k_out[...] = _head_norm(proj[:, aw:2 * aw], gk_ref[...], gmat)
    v_out[...] = proj[:, 2 * aw:3 * aw]
    xr_out[...] = proj[:, 3 * aw:3 * aw + rw]
    gr_out[...] = proj[:, 3 * aw + rw:3 * aw + 2 * rw]
    lf = jax.nn.log_sigmoid(_dot(hn, wf_ref[...]) + bf_ref[...])
    lf_out[...] = lf
    hi, mid, lo = _split3(lf)
    seg = seg_ref[...]
    cn_out[...] = _dot(seg, hi.astype(_BF16)) + _dot(seg, mid.astype(_BF16)) + _dot(seg, lo.astype(_BF16))


def _proj_sample(x2d, t_new, gmix, w_main, wf_pad, bf_row, gq, gk, n_heads, head_dim, rw):
    n, d = x2d.shape
    aw = n_heads * head_dim
    idx = np.arange(n)
    seg = ((idx[:, None] // t_new == idx[None, :] // t_new) & (idx[None, :] <= idx[:, None])).astype(np.float32)
    ins = [x2d, gmix, w_main, wf_pad, bf_row, gq, gk,
           jnp.asarray(_group_mean_matrix(head_dim), _BF16), jnp.asarray(seg, _BF16)]
    full = lambda s: pl.BlockSpec(s, lambda i: (0,) * len(s))
    widths = (aw, aw, aw, V7X_LANES, V7X_LANES, rw, rw)
    return pl.pallas_call(
        functools.partial(_sproj_kernel, aw=aw, rw=rw),
        grid=(1,),
        in_specs=[full(a.shape) for a in ins],
        out_specs=tuple(full((n, w)) for w in widths),
        out_shape=tuple(jax.ShapeDtypeStruct((n, w), _F32) for w in widths),
        compiler_params=_params(1),
        name="proj_sample",
    )(*ins)


def _block_diag_pair(w):
    nb, dd, _ = w.shape
    per = V7X_MXU_DIM // dd
    tiles = []
    for half in range(nb // per):
        tile = jnp.zeros((V7X_MXU_DIM, V7X_MXU_DIM), w.dtype)
        for j in range(per):
            tile = lax.dynamic_update_slice(tile, w[half * per + j], (j * dd, j * dd))
        tiles.append(tile)
    return jnp.stack(tiles).astype(_BF16)


def kernel(x_prompt, x_sample, p_prompt, p_sample, cache_k, cache_v, cache_logf, state_conv, state_h, page_table, g_mix, w_in, b_f, g_q, g_k, w_conv, b_conv, w_a, b_a, w_x, b_x, lam, g_out_attn, g_out_rec, w_out, g_ffn, w_ffn_in, w_ffn_out, w_ple, g_ple, w_ple_gate):
    batch, seq, d_model = x_prompt.shape
    db, t_new, _ = x_sample.shape
    depth, n_heads, head_dim = g_q.shape
    aw = n_heads * head_dim
    rw = lam.shape[1]
    hidden = w_ffn_out.shape[1]
    n_phys, page = cache_k.shape[1], cache_k.shape[2]
    assert n_heads == V7X_SUBLANES and aw == 2 * V7X_MXU_DIM and rw == 2 * V7X_MXU_DIM
    assert seq % ATTN_BLOCK == 0 and seq % PROJ_ROWS == 0

    xp = x_prompt.reshape(batch * seq, d_model)
    xs = x_sample.reshape(db * t_new, d_model)
    outs = [[] for _ in range(10)]
    row = lambda a: a.reshape(1, -1)
    for l in range(depth):
        w_l = w_in[l]
        w_main = jnp.concatenate([w_l[:, :3 * aw], w_l[:, 3 * aw + n_heads:]], axis=1).astype(_BF16)
        w_f = w_l[:, 3 * aw:3 * aw + n_heads]
        wft = jnp.pad(w_f.T, ((0, V7X_BF16_SUBLANES - n_heads), (0, 0))).astype(_BF16)
        wf_pad = jnp.pad(w_f, ((0, 0), (0, V7X_LANES - n_heads))).astype(_BF16)
        bf_col = b_f[l].reshape(n_heads, 1)
        bf_row = jnp.pad(b_f[l], (0, V7X_LANES - n_heads)).reshape(1, V7X_LANES)
        gmix = row(g_mix[l])
        gq = row(g_q[l]) * (head_dim ** -0.5)
        gk = row(g_k[l])
        rec_w = (w_conv[l], row(b_conv[l]), _block_diag_pair(w_a[l]), row(b_a[l]),
                 _block_diag_pair(w_x[l]), row(b_x[l]), row(lam[l]))
        tail_w = (row(g_out_attn[l]), row(g_out_rec[l]), w_out[l].astype(_BF16), row(g_ffn[l]),
                  w_ffn_in[l].astype(_BF16),
                  w_ffn_out[l].astype(_BF16), w_ple[l].astype(_BF16), row(g_ple[l]),
                  w_ple_gate[l].astype(_BF16))

        kt, vt32, lft, qs, kb, vt, augq, augk, rec, h_tiles, x_tiles = _proj_prompt(
            xp, seq, gmix, w_main, wft, bf_col, gq * LOG2_E, gk, rec_w, n_heads, head_dim, rw)
        q_s, k_s, v_s, lf_s, cn_s, xr_s, gr_s = _proj_sample(
            xs, t_new, gmix, w_main, wf_pad, bf_row, gq, gk, n_heads, head_dim, rw)
        lf_s = lf_s[:, :n_heads]
        cache_lft = jnp.swapaxes(cache_logf[l], 1, 2)
        cache_kt = cache_k[l].transpose(0, 2, 3, 1).reshape(n_phys, aw, page)
        cache_vt = cache_v[l].transpose(0, 2, 3, 1).reshape(n_phys, aw, page)
        att, att_s = _attention(qs, augq, kb, augk, vt, batch, seq, q_s, k_s, v_s, cn_s[:, :n_heads],
                                cache_kt, cache_vt, cache_lft, page_table, t_new, n_heads, head_dim)

        xp = _tail(xp, att, rec, p_prompt[l].reshape(batch * seq, -1), tail_w)
        outs[0].append(kt.reshape(batch, n_heads, head_dim, seq).transpose(0, 3, 1, 2))
        outs[1].append(vt32.reshape(batch, n_heads, head_dim, seq).transpose(0, 3, 1, 2))
        outs[2].append(lft.transpose(0, 2, 1))
        outs[3].append(x_tiles.reshape(batch, -1, V7X_SUBLANES, rw)[:, -1, V7X_SUBLANES - (CONV_WIDTH - 1):])
        outs[4].append(h_tiles.reshape(batch, -1, V7X_SUBLANES, rw)[:, -1, -1])

        hist = jnp.concatenate([state_conv[l], jnp.zeros((db, t_new, rw), _F32)], axis=1)
        prevs = tuple(hist[:, CONV_WIDTH - 1 - s:CONV_WIDTH - 1 - s + t_new].reshape(db * t_new, rw)
                      for s in range(1, CONV_WIDTH))
        h0 = jnp.repeat(state_h[l], t_new, axis=0)
        rec_s, h_s = _rglru_sample(xr_s, gr_s, prevs, h0, t_new, rec_w)
        xs = _tail(xs, att_s, rec_s, p_sample[l].reshape(db * t_new, -1), tail_w)
        outs[5].append(k_s.reshape(db, t_new, n_heads, head_dim))
        outs[6].append(v_s.reshape(db, t_new, n_heads, head_dim))
        outs[7].append(lf_s.reshape(db, t_new, n_heads))
        outs[8].append(xr_s.reshape(db, t_new, rw)[:, t_new - (CONV_WIDTH - 1):])
        outs[9].append(h_s.reshape(db, t_new, rw)[:, -1])
    return (xp.reshape(batch, seq, d_model), xs.reshape(db, t_new, d_model),
            *(jnp.stack(o) for o in outs))
```

```python
import functools

import numpy as np
import jax
import jax.numpy as jnp
from jax import lax
from jax.experimental import pallas as pl
from jax.experimental.pallas import tpu as pltpu

RG_C = 8.0
NORM_EPS = 1e-6
LOG2_E = 1.4426950408889634
CONV_WIDTH = 4

V7X_LANES = 128
V7X_SUBLANES = 8
V7X_BF16_SUBLANES = 16
V7X_MXU_DIM = 256
V7X_VMEM_LIMIT_BYTES = 56 * 1024 * 1024

PROJ_PROMPT_ROWS = 1024
PROJ_ROWS = 512
ATTN_BLOCK = 512
FOX_LOOKAHEAD = 4
FFN_CHUNK = 256
BIAS_LANES = 8

_F32 = jnp.float32
_BF16 = jnp.bfloat16
_NT = (((1,), (1,)), ((), ()))


def _dot(a, b):
    return jnp.dot(a, b, preferred_element_type=_F32)


def _dot_nt(a, b):
    return lax.dot_general(a, b, _NT, preferred_element_type=_F32)


def _rms(x, g):
    return x * lax.rsqrt(jnp.mean(x * x, axis=-1, keepdims=True) + NORM_EPS) * g


def _split3(z):
    hi = z.astype(_BF16).astype(_F32)
    r = z - hi
    mid = r.astype(_BF16).astype(_F32)
    lo = r - mid
    return hi, mid, lo


def _stack3(z, rows):
    hi, mid, lo = _split3(z)
    pad = jnp.zeros((rows - 3 * V7X_SUBLANES, z.shape[1]), _F32)
    return jnp.concatenate([hi, mid, lo, pad], axis=0).astype(_BF16)


def _sum3(r):
    s = V7X_SUBLANES
    return r[0:s] + r[s:2 * s] + r[2 * s:3 * s]


def _head_norm(z, g, gmat):
    zz = (z * z).astype(_BF16)
    half = gmat.shape[0]
    ms = jnp.concatenate([_dot(zz[:, :half], gmat), _dot(zz[:, half:], gmat)], axis=1)
    return z * lax.rsqrt(ms + NORM_EPS) * g


def _const_spec(shape):
    nd = len(shape)
    return pl.BlockSpec(shape, lambda *_: (0,) * nd, pipeline_mode=pl.Buffered(1))


def _params(n_axes):
    return pltpu.CompilerParams(
        dimension_semantics=("arbitrary",) * n_axes,
        vmem_limit_bytes=V7X_VMEM_LIMIT_BYTES,
    )


def _sigmoid(z):
    return 0.5 * jnp.tanh(0.5 * z) + 0.5


def _rglru_coeffs(xr, shifted, rec_refs):
    wconv_ref, bconv_ref, wa_ref, ba_ref, wx_ref, bx_ref, lam_ref = rec_refs
    w = wconv_ref[...]
    xc = bconv_ref[...] + shifted[2] * w[0:1]
    xc = xc + shifted[1] * w[1:2]
    xc = xc + shifted[0] * w[2:3]
    xc = xc + xr * w[3:4]

    xcb = xc.astype(_BF16)
    half = wa_ref.shape[1]

    def gate(w_ref, b_ref):
        z = jnp.concatenate([_dot(xcb[:, :half], w_ref[0]), _dot(xcb[:, half:], w_ref[1])], axis=1)
        return _sigmoid(z + b_ref[...])

    r_gate = gate(wa_ref, ba_ref)
    i_gate = gate(wx_ref, bx_ref)
    log_a = -RG_C * r_gate * jax.nn.softplus(-lam_ref[...])
    a = jnp.exp(log_a)
    om = -jnp.tanh(log_a) * (a * a + 1.0)
    root = jnp.where(om > 0.0, om * lax.rsqrt(om), 0.0)
    return a, root * (i_gate * xc)


def _scan_step(a, b, a_s, b_s):
    return a * a_s, b + a * b_s


def _rglru_prompt_tile(xr, gr, hc, rec_refs, xbuf_ref, abuf_ref, bbuf_ref):
    tm, rw = xr.shape
    sub = V7X_SUBLANES
    xbuf_ref[sub:sub + tm] = xr
    shifted = [xbuf_ref[sub - k:sub - k + tm] for k in range(1, CONV_WIDTH)]
    a, b = _rglru_coeffs(xr, shifted, rec_refs)
    abuf_ref[0:sub] = jnp.ones((sub, rw), _F32)
    bbuf_ref[0:sub] = jnp.zeros((sub, rw), _F32)
    d = 1
    while d < tm:
        if d % sub == 0:
            a_new, b_new = _scan_step(a[d:], b[d:], a[:tm - d], b[:tm - d])
            a = jnp.concatenate([a[:d], a_new], axis=0)
            b = jnp.concatenate([b[:d], b_new], axis=0)
        else:
            abuf_ref[sub:sub + tm] = a
            bbuf_ref[sub:sub + tm] = b
            a, b = _scan_step(a, b, abuf_ref[sub - d:sub - d + tm], bbuf_ref[sub - d:sub - d + tm])
        d *= 2
    h = a * hc + b
    return h * jax.nn.gelu(gr), h


def _proj_kernel(x_ref, gmix_ref, w_ref, wft_ref, bf_ref, gq_ref, gk_ref, gmat_ref, cum_ref,
                 aq_ref, ak_ref, cq_ref, ck_ref, *refs, tiles_per_seq, aw, rw):
    rec_refs = refs[:7]
    (k_out, v_out, lft_out, qs_out, kb_out, vt_out, augq_out, augk_out, rec_out, h_out, xtail_out,
     carry_ref, hc_ref, xbuf_ref, abuf_ref, bbuf_ref) = refs[7:]
    i = pl.program_id(0)
    tm = x_ref.shape[0]
    sub = V7X_SUBLANES
    seq_start = i % tiles_per_seq == 0

    @pl.when(seq_start)
    def _():
        carry_ref[...] = jnp.zeros_like(carry_ref)
        hc_ref[...] = jnp.zeros_like(hc_ref)
        xbuf_ref[0:sub] = jnp.zeros((sub, rw), _F32)

    @pl.when(jnp.logical_not(seq_start))
    def _():
        xbuf_ref[0:sub] = xbuf_ref[tm:tm + sub]

    hn = _rms(x_ref[...], gmix_ref[...]).astype(_BF16)
    w_rec = 3 * aw
    proj_rec = _dot(hn, w_ref[:, w_rec:w_rec + 2 * rw])
    xr = proj_rec[:, 0:rw]
    xtail_out[...] = xr[tm - sub:tm]

    y, h = _rglru_prompt_tile(xr, proj_rec[:, rw:2 * rw], hc_ref[...], rec_refs, xbuf_ref, abuf_ref, bbuf_ref)
    hc_ref[...] = h[tm - 1:tm]
    rec_out[...] = y
    h_out[...] = h[tm - sub:tm]

    proj = _dot(hn, w_ref[:, 0:w_rec])
    gmat = gmat_ref[...]
    qn = _head_norm(proj[:, 0:aw], gq_ref[...], gmat)
    kn = _head_norm(proj[:, aw:2 * aw], gk_ref[...], gmat)
    vt = proj[:, 2 * aw:3 * aw].T
    k_out[...] = kn.T
    v_out[...] = vt
    qs_out[...] = qn.astype(_BF16)
    kb_out[...] = kn.astype(_BF16)
    vt_out[...] = vt.astype(_BF16)

    ft = _dot_nt(wft_ref[...], hn)
    lft = jax.nn.log_sigmoid(ft[0:8] + bf_ref[...])
    lft_out[...] = lft

    p3 = _stack3(lft, V7X_LANES)
    blk = cum_ref.shape[0]
    carry = carry_ref[...]
    cums = []
    for c in range(tm // blk):
        r = _dot(p3[:, c * blk:(c + 1) * blk], cum_ref[...])
        cums.append(_sum3(r[:, :blk]) + carry)
        carry = carry + _sum3(r[:, blk:])
    carry_ref[...] = carry
    cum = jnp.concatenate(cums, axis=1)

    c3 = _stack3(cum * LOG2_E, V7X_LANES)
    qat = _dot(aq_ref[...], c3) + cq_ref[...]
    kat = _dot(ak_ref[...], c3) + ck_ref[...]
    augq_out[...] = qat.T.astype(_BF16)
    augk_out[...] = kat.T.astype(_BF16)


def _bias_placement(n_heads):
    aq = np.zeros((V7X_LANES, V7X_LANES), np.float32)
    ak = np.zeros((V7X_LANES, V7X_LANES), np.float32)
    cq = np.zeros((V7X_LANES, 1), np.float32)
    ck = np.zeros((V7X_LANES, 1), np.float32)
    for h in range(n_heads):
        for j in range(3):
            aq[h * BIAS_LANES + j, j * V7X_SUBLANES + h] = 1.0
            cq[h * BIAS_LANES + 3 + j, 0] = 1.0
            ak[h * BIAS_LANES + 3 + j, j * V7X_SUBLANES + h] = -1.0
            ck[h * BIAS_LANES + j, 0] = 1.0
    return aq, ak, cq, ck


def _group_mean_matrix(head_dim):
    idx = np.arange(V7X_MXU_DIM) // head_dim
    return (idx[:, None] == idx[None, :]).astype(np.float32) / head_dim


def _prefix_matrix():
    idx = np.arange(V7X_MXU_DIM)
    incl = (idx[:, None] <= idx[None, :]).astype(np.float32)
    return np.concatenate([incl, np.ones_like(incl)], axis=1)


def _proj_prompt(x2d, seq, gmix, w_main, wft, bf_col, gq, gk, rec_w, n_heads, head_dim, rw):
    n, d = x2d.shape
    tm = PROJ_PROMPT_ROWS
    aw = n_heads * head_dim
    aq, ak, cq, ck = _bias_placement(n_heads)
    consts = [
        gmix, w_main, wft, bf_col, gq, gk,
        jnp.asarray(_group_mean_matrix(head_dim), _BF16),
        jnp.asarray(_prefix_matrix(), _BF16),
        jnp.asarray(aq, _BF16), jnp.asarray(ak, _BF16), jnp.asarray(cq), jnp.asarray(ck),
        *rec_w,
    ]
    tps = seq // tm
    batch = n // seq
    row = lambda w: pl.BlockSpec((tm, w), lambda i: (i, 0))
    seq_t = lambda r: pl.BlockSpec((None, r, tm), lambda i: (i // tps, 0, i % tps))
    out_shape = (
        jax.ShapeDtypeStruct((batch, aw, seq), _F32),
        jax.ShapeDtypeStruct((batch, aw, seq), _F32),
        jax.ShapeDtypeStruct((batch, V7X_SUBLANES, seq), _F32),
        jax.ShapeDtypeStruct((n, aw), _BF16),
        jax.ShapeDtypeStruct((n, aw), _BF16),
        jax.ShapeDtypeStruct((batch, aw, seq), _BF16),
        jax.ShapeDtypeStruct((n, V7X_LANES), _BF16),
        jax.ShapeDtypeStruct((n, V7X_LANES), _BF16),
        jax.ShapeDtypeStruct((n, rw), _F32),
        jax.ShapeDtypeStruct((n // tm * V7X_SUBLANES, rw), _F32),
        jax.ShapeDtypeStruct((n // tm * V7X_SUBLANES, rw), _F32),
    )
    tail8 = pl.BlockSpec((V7X_SUBLANES, rw), lambda i: (i, 0))
    out_specs = (
        seq_t(aw), seq_t(aw), seq_t(V7X_SUBLANES), row(aw), row(aw), seq_t(aw),
        row(V7X_LANES), row(V7X_LANES), row(rw), tail8, tail8,
    )
    return pl.pallas_call(
        functools.partial(_proj_kernel, tiles_per_seq=tps, aw=aw, rw=rw),
        grid=(n // tm,),
        in_specs=[row(d)] + [_const_spec(c.shape) for c in consts],
        out_specs=out_specs,
        out_shape=out_shape,
        scratch_shapes=[pltpu.VMEM((V7X_SUBLANES, V7X_MXU_DIM), _F32), pltpu.VMEM((1, rw), _F32)]
        + [pltpu.VMEM((V7X_SUBLANES + tm, rw), _F32)] * 3,
        compiler_params=_params(1),
        name="proj_prompt",
    )(x2d, *consts)


class _PagedSide:
    def __init__(self, small, pages, pps, page, uo_ref, o_ref, scratch, t_new, head_dim, n_pages):
        self.q_ref, self.kn_ref, self.vn_ref, self.cnq_ref, self.cnk_ref = small
        self.k_page, self.v_page, self.lf_page = pages
        self.pps, self.page, self.uo_ref, self.o_ref = pps, page, uo_ref, o_ref
        self.qbd_ref, self.m_ref, self.l_ref, self.acc_ref, self.carry_ref = scratch
        self.t_new, self.head_dim, self.n_pages = t_new, head_dim, n_pages
        self.nh = V7X_SUBLANES
        self.aw = self.q_ref.shape[1]

    def _head_mask(self):
        feat = lax.broadcasted_iota(jnp.int32, (self.nh, self.aw), 1)
        head = lax.broadcasted_iota(jnp.int32, (self.nh, self.aw), 0)
        return (feat >= head * self.head_dim) & (feat < (head + 1) * self.head_dim)

    def init(self):
        nh, aw, page, t_new = self.nh, self.aw, self.page, self.t_new
        rows = t_new * nh
        q = self.q_ref[...]
        head_mask = self._head_mask()
        qbd = jnp.concatenate(
            [jnp.where(head_mask, jnp.broadcast_to(q[t:t + 1], (nh, aw)), 0.0) for t in range(t_new)],
            axis=0).astype(_BF16)
        self.qbd_ref[...] = qbd
        pad = jnp.zeros((page - self.kn_ref.shape[0], aw), _F32)
        kn = jnp.concatenate([self.kn_ref[...], pad], axis=0).astype(_BF16)
        vn = jnp.concatenate([self.vn_ref[...], pad], axis=0).astype(_BF16)
        s = _dot_nt(qbd, kn) + self.cnq_ref[...] - self.cnk_ref[...]
        row = lax.broadcasted_iota(jnp.int32, (rows, page), 0)
        t_col = lax.broadcasted_iota(jnp.int32, (rows, page), 1)
        s = jnp.where(t_col * nh <= row, s, -jnp.inf)
        m = jnp.max(s, axis=1, keepdims=True)
        p = jnp.exp(s - m)
        self.m_ref[...] = m
        self.l_ref[...] = jnp.sum(p, axis=1, keepdims=True)
        self.acc_ref[...] = _dot(p.astype(_BF16), vn)
        self.carry_ref[...] = jnp.zeros_like(self.carry_ref)

    def scores(self, step):
        pps, page = self.pps, self.page
        qbd = self.qbd_ref[...]
        carry = self.carry_ref[...]
        scores = []
        for i in range(pps):
            valid = step * pps + i < self.n_pages
            s = _dot(qbd, self.k_page(i).astype(_BF16))
            lf = jnp.where(valid, self.lf_page(i), 0.0)
            r = _dot(_stack3(lf, 4 * V7X_SUBLANES), self.uo_ref[...])
            bias = _sum3(r[:, :page]) + carry
            carry = carry + _sum3(r[:, page:])
            s = s + jnp.concatenate([bias] * self.t_new, axis=0)
            scores.append(jnp.where(valid, s, -jnp.inf))
        self.carry_ref[...] = carry
        s_all = jnp.concatenate(scores, axis=1) + self.cnq_ref[...]
        m_old = self.m_ref[...]
        m_new = jnp.maximum(m_old, jnp.max(s_all, axis=1, keepdims=True))
        alpha = jnp.exp(m_old - m_new)
        p = jnp.exp(s_all - m_new)
        self.l_ref[...] = alpha * self.l_ref[...] + jnp.sum(p, axis=1, keepdims=True)
        self.m_ref[...] = m_new
        return p.astype(_BF16), alpha

    def values(self, p, alpha):
        vt_all = jnp.concatenate([self.v_page(i).astype(_BF16) for i in range(self.pps)], axis=1)
        self.acc_ref[...] = alpha * self.acc_ref[...] + _dot_nt(p, vt_all)

    def finish(self):
        nh = self.nh
        a = self.acc_ref[...] / self.l_ref[...]
        head_mask = self._head_mask()
        for t in range(self.t_new):
            self.o_ref[t:t + 1, :] = jnp.sum(jnp.where(head_mask, a[t * nh:(t + 1) * nh], 0.0),
                                             axis=0, keepdims=True)


def _fox_kernel(qi_tab, ki_tab, pt_ref, q_ref, aq_ref, k_ref, ak_ref, vt_ref, *refs,
                n_heads, head_dim, n_samp, pps, t_new, n_pages):
    n_small = 5
    small = refs[:n_small * n_samp]
    rest = refs[n_small * n_samp:]
    cache_refs = rest[0:3]
    uo_ref, o_ref = rest[3], rest[4]
    so_refs = rest[5:5 + n_samp]
    qf_ref, m_ref, acc_ref = rest[5 + n_samp:8 + n_samp]
    s_scratch = rest[8 + n_samp:8 + 6 * n_samp]
    page_bufs = rest[8 + 6 * n_samp:11 + 6 * n_samp]
    sem = rest[11 + 6 * n_samp]

    batch_i = pl.program_id(0)
    step = pl.program_id(1)
    n_steps = pl.num_programs(1)
    g = batch_i * n_steps + step
    slot = g % 2
    page = page_bufs[0].shape[3]

    def page_copies(b2, s2, dst_slot, lookup):
        copies = []
        for u in range(n_samp):
            for i in range(pps):
                pid = 0
                if lookup:
                    pid = pt_ref[b2 * n_samp + u, jnp.maximum(n_pages - 1 - (s2 * pps + i), 0)]
                for c, (cache, buf) in enumerate(zip(cache_refs, page_bufs)):
                    copies.append(pltpu.make_async_copy(cache.at[pid], buf.at[dst_slot, u * pps + i],
                                                        sem.at[dst_slot, c]))
        return copies

    @pl.when(g == 0)
    def _first_pages():
        for c in page_copies(0, 0, 0, True):
            c.start()

    @pl.when(g + 1 < pl.num_programs(0) * n_steps)
    def _next_pages():
        wrap = step + 1 == n_steps
        for c in page_copies(jnp.where(wrap, batch_i + 1, batch_i), jnp.where(wrap, 0, step + 1),
                             1 - slot, True):
            c.start()

    for c in page_copies(batch_i, step, slot, False):
        c.wait()

    def reader(buf, u):
        return lambda i: buf[slot, u * pps + i]

    paged = [
        _PagedSide(small[u * n_small:(u + 1) * n_small], [reader(buf, u) for buf in page_bufs], pps, page,
                   uo_ref, so_refs[u], s_scratch[u * 5:(u + 1) * 5], t_new, head_dim, n_pages)
        for u in range(n_samp)]

    qi = qi_tab[step]
    ki = ki_tab[step]
    bq = q_ref.shape[0]
    bk = k_ref.shape[0]
    pair = 2 * head_dim

    @pl.when(step == 0)
    def _paged_init():
        for side in paged:
            side.init()

    @pl.when(ki == 0)
    def _init():
        lane = lax.broadcasted_iota(jnp.int32, (bq, V7X_LANES), 1)
        a = aq_ref[...]
        zero = jnp.zeros_like(a)
        for h in range(n_heads):
            q2 = q_ref[:, (h // 2) * pair:(h // 2 + 1) * pair]
            lo = head_dim * (h % 2)
            qh = jnp.where((lane >= lo) & (lane < lo + head_dim), q2, zero)
            ah = jnp.where((lane >= h * BIAS_LANES) & (lane < (h + 1) * BIAS_LANES), a, zero)
            qf_ref[h] = jnp.concatenate([qh, ah], axis=1)
        m_ref[...] = jnp.full_like(m_ref, -jnp.inf)
        acc_ref[...] = jnp.zeros_like(acc_ref)

    def step_body(masked):
        ak = ak_ref[...]
        hk, hq = bk // 2, bq // 2
        ones_blk = (lax.broadcasted_iota(jnp.int32, (V7X_BF16_SUBLANES, bk), 0) == 0).astype(_BF16)
        if masked:
            tri = (lax.broadcasted_iota(jnp.int32, (hk, hq), 0)
                   <= lax.broadcasted_iota(jnp.int32, (hk, hq), 1))

        def scores(h):
            hp = h // 2
            kf = jnp.concatenate([k_ref[:, hp * pair:(hp + 1) * pair], ak], axis=1)
            if not masked:
                return _dot_nt(kf, qf_ref[h])
            return (_dot_nt(kf[:hk], qf_ref[h]),
                    _dot_nt(kf[hk:], qf_ref[h, hq:, :]))

        paged_p = [side.scores(step) for side in paged]
        ahead = [scores(h) for h in range(FOX_LOOKAHEAD)]
        for side, (p_s, alpha_s) in zip(paged, paged_p):
            side.values(p_s, alpha_s)
        for h in range(n_heads):
            s = ahead.pop(0)
            if h + FOX_LOOKAHEAD < n_heads:
                ahead.append(scores(h + FOX_LOOKAHEAD))
            m_old = m_ref[h]
            lhs = jnp.concatenate([vt_ref[h * head_dim:(h + 1) * head_dim, :], ones_blk], axis=0)
            if not masked:
                m_new = jnp.maximum(m_old, jnp.max(s, axis=0, keepdims=True))
                pv = _dot(lhs, jnp.exp2(s - m_new).astype(_BF16))
            else:
                s_a, s_b = s
                s_a = jnp.concatenate([jnp.where(tri, s_a[:, :hq], -jnp.inf), s_a[:, hq:]], axis=1)
                s_b = jnp.where(tri, s_b, -jnp.inf)
                top = jnp.max(s_a, axis=0, keepdims=True)
                top = jnp.concatenate([top[:, :hq], jnp.maximum(top[:, hq:], jnp.max(s_b, axis=0, keepdims=True))],
                                      axis=1)
                m_new = jnp.maximum(m_old, top)
                pv = _dot(lhs[:, :hk], jnp.exp2(s_a - m_new).astype(_BF16))
                pv_b = _dot(lhs[:, hk:], jnp.exp2(s_b - m_new[:, hq:]).astype(_BF16))
                pv = jnp.concatenate([pv[:, :hq], pv[:, hq:] + pv_b], axis=1)
            acc_ref[h] = jnp.exp2(m_old - m_new) * acc_ref[h] + pv
            m_ref[h] = m_new

    @pl.when(ki < qi)
    def _off_diagonal():
        step_body(False)

    @pl.when(ki == qi)
    def _diagonal():
        step_body(True)
        for hp in range(n_heads // 2):
            outs = []
            for h in (2 * hp, 2 * hp + 1):
                a = acc_ref[h]
                outs.append(a[0:head_dim] / a[head_dim:head_dim + 1])
            o_ref[:, hp * pair:(hp + 1) * pair] = jnp.concatenate(outs, axis=0).T

    @pl.when(step == pl.num_programs(1) - 1)
    def _paged_finish():
        for side in paged:
            side.finish()


def _suffix_matrix(page):
    idx = np.arange(page)
    strict = (idx[:, None] > idx[None, :]).astype(np.float32)
    return np.concatenate([strict, np.ones_like(strict)], axis=1)


def _attention(qs, augq, kb, augk, vt, batch, seq, q_s, k_new, v_new, cn, cache_kt, cache_vt, cache_lft,
               page_table, t_new, n_heads, head_dim):
    aw = n_heads * head_dim
    bq = bk = ATTN_BLOCK
    nq = seq // bq
    qs3 = qs.reshape(batch, seq, aw)
    kb3 = kb.reshape(batch, seq, aw)
    aq3 = augq.reshape(batch, seq, V7X_LANES)
    ak3 = augk.reshape(batch, seq, V7X_LANES)
    pairs = [(qi, ki) for qi in range(nq) for ki in range(qi + 1)]
    qi_tab = jnp.asarray([p[0] for p in pairs], jnp.int32)
    ki_tab = jnp.asarray([p[1] for p in pairs], jnp.int32)
    n_steps = len(pairs)

    db, n_pages = page_table.shape
    page = cache_kt.shape[2]
    assert db % batch == 0, "sample batches must split evenly over the prompt batches"
    n_samp = db // batch
    pps = -(-n_pages // n_steps)
    rows = t_new * n_heads
    pad_t = V7X_SUBLANES - t_new
    q3 = q_s.reshape(db, t_new, aw)
    kn = jnp.pad(k_new.reshape(db, t_new, aw), ((0, 0), (0, pad_t), (0, 0)))
    vn = jnp.pad(v_new.reshape(db, t_new, aw), ((0, 0), (0, pad_t), (0, 0)))
    cn3 = cn.reshape(db, t_new, n_heads)
    cnq = cn3.reshape(db, rows, 1)
    cnk = jnp.tile(jnp.swapaxes(cn3, 1, 2), (1, t_new, 1))
    cnk = jnp.pad(cnk, ((0, 0), (0, 0), (0, page - t_new)))
    uo = jnp.asarray(_suffix_matrix(page), _BF16)

    qmap = lambda b, s, qt, kt, pt: (b, qt[s], 0)
    kmap = lambda b, s, qt, kt, pt: (b, kt[s], 0)

    def samp_spec(shape, u):
        return pl.BlockSpec((None,) + shape, lambda b, s, qt, kt, pt: (b * n_samp + u, 0, 0))

    small_specs, small_args = [], []
    for u in range(n_samp):
        small_specs += [samp_spec((t_new, aw), u), samp_spec((V7X_SUBLANES, aw), u),
                        samp_spec((V7X_SUBLANES, aw), u), samp_spec((rows, 1), u), samp_spec((rows, page), u)]
        small_args += [q3, kn, vn, cnq, cnk]
    caches = (cache_kt, cache_vt, cache_lft)

    grid_spec = pltpu.PrefetchScalarGridSpec(
        num_scalar_prefetch=3,
        grid=(batch, n_steps),
        in_specs=[
            pl.BlockSpec((None, bq, aw), qmap),
            pl.BlockSpec((None, bq, V7X_LANES), qmap),
            pl.BlockSpec((None, bk, aw), kmap),
            pl.BlockSpec((None, bk, V7X_LANES), kmap),
            pl.BlockSpec((None, aw, bk), lambda b, s, qt, kt, pt: (b, 0, kt[s])),
        ] + small_specs + [pl.BlockSpec(memory_space=pl.ANY)] * len(caches)
        + [pl.BlockSpec(uo.shape, lambda b, s, qt, kt, pt: (0, 0))],
        out_specs=[pl.BlockSpec((None, bq, aw), qmap)]
        + [pl.BlockSpec((None, t_new, aw), lambda b, s, qt, kt, pt: (b, 0, 0))] * n_samp,
        scratch_shapes=[
            pltpu.VMEM((n_heads, bq, 2 * V7X_LANES), _BF16),
            pltpu.VMEM((n_heads, 1, bq), _F32),
            pltpu.VMEM((n_heads, head_dim + V7X_BF16_SUBLANES, bq), _F32),
        ] + [
            pltpu.VMEM((rows, aw), _BF16),
            pltpu.VMEM((rows, 1), _F32),
            pltpu.VMEM((rows, 1), _F32),
            pltpu.VMEM((rows, aw), _F32),
            pltpu.VMEM((V7X_SUBLANES, page), _F32),
        ] * n_samp + [
            pltpu.VMEM((2, n_samp * pps) + c.shape[1:], c.dtype) for c in caches
        ] + [pltpu.SemaphoreType.DMA((2, len(caches)))],
    )
    outs = pl.pallas_call(
        functools.partial(_fox_kernel, n_heads=n_heads, head_dim=head_dim, n_samp=n_samp, pps=pps,
                          t_new=t_new, n_pages=n_pages),
        grid_spec=grid_spec,
        out_shape=[jax.ShapeDtypeStruct((batch, seq, aw), _F32)]
        + [jax.ShapeDtypeStruct((batch, t_new, aw), _F32)] * n_samp,
        compiler_params=_params(2),
        name="attention",
    )(qi_tab, ki_tab, page_table, qs3, aq3, kb3, ak3, vt, *small_args, *caches, uo)
    att = outs[0].reshape(batch * seq, aw)
    att_s = jnp.stack(outs[1:], axis=1)
    return att, att_s.reshape(db * t_new, aw)


def _rglru_sample_kernel(xr_ref, gr_ref, p1_ref, p2_ref, p3_ref, h0_ref, *refs, seg_len):
    rec_refs = refs[:7]
    y_ref, h_ref = refs[7:]
    tm, rw = xr_ref.shape
    xr = xr_ref[...]
    t = lax.broadcasted_iota(jnp.int32, (tm, rw), 0) & (seg_len - 1)
    prev = (p1_ref, p2_ref, p3_ref)
    shifted = [jnp.where(t >= k, pltpu.roll(xr, k, 0), prev[k - 1][...]) for k in range(1, CONV_WIDTH)]
    a, b = _rglru_coeffs(xr, shifted, rec_refs)
    d = 1
    while d < seg_len:
        a_new, b_new = _scan_step(a, b, pltpu.roll(a, d, 0), pltpu.roll(b, d, 0))
        a = jnp.where(t >= d, a_new, a)
        b = jnp.where(t >= d, b_new, b)
        d *= 2
    h = a * h0_ref[...] + b
    y_ref[...] = h * jax.nn.gelu(gr_ref[...])
    h_ref[...] = h


def _rglru_sample(xr, gr, prevs, h0, seg_len, rec_w):
    n, rw = xr.shape
    assert seg_len & (seg_len - 1) == 0 and n % seg_len == 0
    full = pl.BlockSpec((n, rw), lambda i: (0, 0))
    return pl.pallas_call(
        functools.partial(_rglru_sample_kernel, seg_len=seg_len),
        grid=(1,),
        in_specs=[full] * 6 + [_const_spec(c.shape) for c in rec_w],
        out_specs=(full, full),
        out_shape=(jax.ShapeDtypeStruct((n, rw), _F32), jax.ShapeDtypeStruct((n, rw), _F32)),
        compiler_params=_params(1),
        name="rglru_sample",
    )(xr, gr, *prevs, h0, *rec_w)


def _tail_math(x, att, rec, p, goa_ref, gor_ref, wout_ref, gffn_ref, wi_ref, wo_ref, wple_ref,
               gple_ref, wpg_ref):
    aw = att.shape[1]
    an = _rms(att, goa_ref[...]).astype(_BF16)
    rn = _rms(rec, gor_ref[...]).astype(_BF16)
    x1 = x + _dot(an, wout_ref[0:aw, :]) + _dot(rn, wout_ref[aw:, :])
    hn = _rms(x1, gffn_ref[...]).astype(_BF16)
    hidden = wo_ref.shape[0]
    ffn = None
    for c0 in range(0, hidden, FFN_CHUNK):
        cw = min(FFN_CHUNK, hidden - c0)
        g = _dot(hn, wi_ref[:, c0:c0 + cw])
        u = _dot(hn, wi_ref[:, hidden + c0:hidden + c0 + cw])
        part = _dot((jax.nn.silu(g) * u).astype(_BF16), wo_ref[c0:c0 + cw, :])
        ffn = part if ffn is None else ffn + part
    x2 = x1 + ffn
    gate = jax.nn.sigmoid(_dot(_rms(x2, gple_ref[...]).astype(_BF16), wpg_ref[...]))
    return x2 + _dot(p.astype(_BF16), wple_ref[...]) * gate


def _tail_kernel(x_ref, att_ref, rec_ref, p_ref, *refs):
    o_ref = refs[-1]
    o_ref[...] = _tail_math(x_ref[...], att_ref[...], rec_ref[...], p_ref[...], *refs[:-1])


def _tail(x2d, att, rec, p2d, tail_w):
    n, d = x2d.shape
    tm = min(PROJ_ROWS, n)
    row = lambda w: pl.BlockSpec((tm, w), lambda i: (i, 0))
    return pl.pallas_call(
        _tail_kernel,
        grid=(n // tm,),
        in_specs=[row(d), row(att.shape[1]), row(rec.shape[1]), row(p2d.shape[1])]
        + [_const_spec(c.shape) for c in tail_w],
        out_specs=row(d),
        out_shape=jax.ShapeDtypeStruct((n, d), _F32),
        compiler_params=_params(1),
        name="tail",
    )(x2d, att, rec, p2d, *tail_w)


def _sproj_kernel(x_ref, gmix_ref, w_ref, wf_ref, bf_ref, gq_ref, gk_ref, gmat_ref, seg_ref,
                  q_out, k_out, v_out, lf_out, cn_out, xr_out, gr_out, *, aw, rw):
    hn = _rms(x_ref[...], gmix_ref[...]).astype(_BF16)
    proj = _dot(hn, w_ref[...])
    gmat = gmat_ref[...]
    q_out[...] = _head_norm(proj[:, 0:aw], gq_ref[...], gmat)
    k_out[...] = _head_norm(proj[:, aw:2 * aw], gk_ref[...], gmat)
    v_out[...] = proj[:, 2 * aw:3 * aw]
    xr_out[...] = proj[:, 3 * aw:3 * aw + rw]
    gr_out[...] = proj[:, 3 * aw + rw:3 * aw + 2 * rw]
    lf = jax.nn.log_sigmoid(_dot(hn, wf_ref[...]) + bf_ref[...])
    lf_out[...] = lf
    hi, mid, lo = _split3(lf)
    seg = seg_ref[...]
    cn_out[...] = _dot(seg, hi.astype(_BF16)) + _dot(seg, mid.astype(_BF16)) + _dot(seg, lo.astype(_BF16))


def _proj_sample(x2d, t_new, gmix, w_main, wf_pad, bf_row, gq, gk, n_heads, head_dim, rw):
    n, d = x2d.shape
    aw = n_heads * head_dim
    idx = np.arange(n)
    seg = ((idx[:, None] // t_new == idx[None, :] // t_new) & (idx[None, :] <= idx[:, None])).astype(np.float32)
    ins = [x2d, gmix, w_main, wf_pad, bf_row, gq, gk,
           jnp.asarray(_group_mean_matrix(head_dim), _BF16), jnp.asarray(seg, _BF16)]
    full = lambda s: pl.BlockSpec(s, lambda i: (0,) * len(s))
    widths = (aw, aw, aw, V7X_LANES, V7X_LANES, rw, rw)
    return pl.pallas_call(
        functools.partial(_sproj_kernel, aw=aw, rw=rw),
        grid=(1,),
        in_specs=[full(a.shape) for a in ins],
        out_specs=tuple(full((n, w)) for w in widths),
        out_shape=tuple(jax.ShapeDtypeStruct((n, w), _F32) for w in widths),
        compiler_params=_params(1),
        name="proj_sample",
    )(*ins)


def _block_diag_pair(w):
    nb, dd, _ = w.shape
    per = V7X_MXU_DIM // dd
    tiles = []
    for half in range(nb // per):
        tile = jnp.zeros((V7X_MXU_DIM, V7X_MXU_DIM), w.dtype)
        for j in range(per):
            tile = lax.dynamic_update_slice(tile, w[half * per + j], (j * dd, j * dd))
        tiles.append(tile)
    return jnp.stack(tiles).astype(_BF16)


def kernel(x_prompt, x_sample, p_prompt, p_sample, cache_k, cache_v, cache_logf, state_conv, state_h, page_table, g_mix, w_in, b_f, g_q, g_k, w_conv, b_conv, w_a, b_a, w_x, b_x, lam, g_out_attn, g_out_rec, w_out, g_ffn, w_ffn_in, w_ffn_out, w_ple, g_ple, w_ple_gate):
    batch, seq, d_model = x_prompt.shape
    db, t_new, _ = x_sample.shape
    depth, n_heads, head_dim = g_q.shape
    aw = n_heads * head_dim
    rw = lam.shape[1]
    hidden = w_ffn_out.shape[1]
    n_phys, page = cache_k.shape[1], cache_k.shape[2]
    assert n_heads == V7X_SUBLANES and aw == 2 * V7X_MXU_DIM and rw == 2 * V7X_MXU_DIM
    assert seq % ATTN_BLOCK == 0 and seq % PROJ_ROWS == 0 and seq % PROJ_PROMPT_ROWS == 0

    xp = x_prompt.reshape(batch * seq, d_model)
    xs = x_sample.reshape(db * t_new, d_model)
    outs = [[] for _ in range(10)]
    row = lambda a: a.reshape(1, -1)
    for l in range(depth):
        w_l = w_in[l]
        w_main = jnp.concatenate([w_l[:, :3 * aw], w_l[:, 3 * aw + n_heads:]], axis=1).astype(_BF16)
        w_f = w_l[:, 3 * aw:3 * aw + n_heads]
        wft = jnp.pad(w_f.T, ((0, V7X_BF16_SUBLANES - n_heads), (0, 0))).astype(_BF16)
        wf_pad = jnp.pad(w_f, ((0, 0), (0, V7X_LANES - n_heads))).astype(_BF16)
        bf_col = b_f[l].reshape(n_heads, 1)
        bf_row = jnp.pad(b_f[l], (0, V7X_LANES - n_heads)).reshape(1, V7X_LANES)
        gmix = row(g_mix[l])
        gq = row(g_q[l]) * (head_dim ** -0.5)
        gk = row(g_k[l])
        rec_w = (w_conv[l], row(b_conv[l]), _block_diag_pair(w_a[l]), row(b_a[l]),
                 _block_diag_pair(w_x[l]), row(b_x[l]), row(lam[l]))
        tail_w = (row(g_out_attn[l]), row(g_out_rec[l]), w_out[l].astype(_BF16), row(g_ffn[l]),
                  w_ffn_in[l].astype(_BF16),
                  w_ffn_out[l].astype(_BF16), w_ple[l].astype(_BF16), row(g_ple[l]),
                  w_ple_gate[l].astype(_BF16))

        kt, vt32, lft, qs, kb, vt, augq, augk, rec, h_tiles, x_tiles = _proj_prompt(
            xp, seq, gmix, w_main, wft, bf_col, gq * LOG2_E, gk, rec_w, n_heads, head_dim, rw)
        q_s, k_s, v_s, lf_s, cn_s, xr_s, gr_s = _proj_sample(
            xs, t_new, gmix, w_main, wf_pad, bf_row, gq, gk, n_heads, head_dim, rw)
        lf_s = lf_s[:, :n_heads]
        cache_lft = jnp.swapaxes(cache_logf[l], 1, 2)
        cache_kt = cache_k[l].transpose(0, 2, 3, 1).reshape(n_phys, aw, page)
        cache_vt = cache_v[l].transpose(0, 2, 3, 1).reshape(n_phys, aw, page)
        att, att_s = _attention(qs, augq, kb, augk, vt, batch, seq, q_s, k_s, v_s, cn_s[:, :n_heads],
                                cache_kt, cache_vt, cache_lft, page_table, t_new, n_heads, head_dim)

        xp = _tail(xp, att, rec, p_prompt[l].reshape(batch * seq, -1), tail_w)
        outs[0].append(kt.reshape(batch, n_heads, head_dim, seq).transpose(0, 3, 1, 2))
        outs[1].append(vt32.reshape(batch, n_heads, head_dim, seq).transpose(0, 3, 1, 2))
        outs[2].append(lft.transpose(0, 2, 1))
        outs[3].append(x_tiles.reshape(batch, -1, V7X_SUBLANES, rw)[:, -1, V7X_SUBLANES - (CONV_WIDTH - 1):])
        outs[4].append(h_tiles.reshape(batch, -1, V7X_SUBLANES, rw)[:, -1, -1])

        hist = jnp.concatenate([state_conv[l], jnp.zeros((db, t_new, rw), _F32)], axis=1)
        prevs = tuple(hist[:, CONV_WIDTH - 1 - s:CONV_WIDTH - 1 - s + t_new].reshape(db * t_new, rw)
                      for s in range(1, CONV_WIDTH))
        h0 = jnp.repeat(state_h[l], t_new, axis=0)
        rec_s, h_s = _rglru_sample(xr_s, gr_s, prevs, h0, t_new, rec_w)
        xs = _tail(xs, att_s, rec_s, p_sample[l].reshape(db * t_new, -1), tail_w)
        outs[5].append(k_s.reshape(db, t_new, n_heads, head_dim))
        outs[6].append(v_s.reshape(db, t_new, n_heads, head_dim))
        outs[7].append(lf_s.reshape(db, t_new, n_heads))
        outs[8].append(xr_s.reshape(db, t_new, rw)[:, t_new - (CONV_WIDTH - 1):])
        outs[9].append(h_s.reshape(db, t_new, rw)[:, -1])
    return (xp.reshape(batch, seq, d_model), xs.reshape(db, t_new, d_model),
            *(jnp.stack(o) for o in outs))
```

```python
import functools

import numpy as np
import jax
import jax.numpy as jnp
from jax import lax
from jax.experimental import pallas as pl
from jax.experimental.pallas import tpu as pltpu

RG_C = 8.0
NORM_EPS = 1e-6
LOG2_E = 1.4426950408889634
CONV_WIDTH = 4

V7X_LANES = 128
V7X_SUBLANES = 8
V7X_BF16_SUBLANES = 16
V7X_MXU_DIM = 256
V7X_VMEM_LIMIT_BYTES = 56 * 1024 * 1024

PROJ_PROMPT_ROWS = 1024
PROJ_ROWS = 512
ATTN_BLOCK = 512
FOX_LOOKAHEAD = 4
FFN_CHUNK = 256
BIAS_LANES = 8

_F32 = jnp.float32
_BF16 = jnp.bfloat16
_NT = (((1,), (1,)), ((), ()))


def _dot(a, b):
    return jnp.dot(a, b, preferred_element_type=_F32)


def _dot_nt(a, b):
    return lax.dot_general(a, b, _NT, preferred_element_type=_F32)


def _rms(x, g):
    return x * lax.rsqrt(jnp.mean(x * x, axis=-1, keepdims=True) + NORM_EPS) * g


def _split3(z):
    hi = z.astype(_BF16).astype(_F32)
    r = z - hi
    mid = r.astype(_BF16).astype(_F32)
    lo = r - mid
    return hi, mid, lo


def _stack3(z, rows):
    hi, mid, lo = _split3(z)
    pad = jnp.zeros((rows - 3 * V7X_SUBLANES, z.shape[1]), _F32)
    return jnp.concatenate([hi, mid, lo, pad], axis=0).astype(_BF16)


def _sum3(r):
    s = V7X_SUBLANES
    return r[0:s] + r[s:2 * s] + r[2 * s:3 * s]


def _head_norm(z, g, gmat):
    zz = (z * z).astype(_BF16)
    half = gmat.shape[0]
    ms = jnp.concatenate([_dot(zz[:, :half], gmat), _dot(zz[:, half:], gmat)], axis=1)
    return z * lax.rsqrt(ms + NORM_EPS) * g


def _const_spec(shape):
    nd = len(shape)
    return pl.BlockSpec(shape, lambda *_: (0,) * nd, pipeline_mode=pl.Buffered(1))


def _params(n_axes):
    return pltpu.CompilerParams(
        dimension_semantics=("arbitrary",) * n_axes,
        vmem_limit_bytes=V7X_VMEM_LIMIT_BYTES,
    )


def _sigmoid(z):
    return 0.5 * jnp.tanh(0.5 * z) + 0.5


def _rglru_coeffs(xr, shifted, rec_refs):
    wconv_ref, bconv_ref, wa_ref, ba_ref, wx_ref, bx_ref, lam_ref = rec_refs
    w = wconv_ref[...]
    xc = bconv_ref[...] + shifted[2] * w[0:1]
    xc = xc + shifted[1] * w[1:2]
    xc = xc + shifted[0] * w[2:3]
    xc = xc + xr * w[3:4]

    xcb = xc.astype(_BF16)
    half = wa_ref.shape[1]

    def gate(w_ref, b_ref):
        z = jnp.concatenate([_dot(xcb[:, :half], w_ref[0]), _dot(xcb[:, half:], w_ref[1])], axis=1)
        return _sigmoid(z + b_ref[...])

    r_gate = gate(wa_ref, ba_ref)
    i_gate = gate(wx_ref, bx_ref)
    log_a = -RG_C * r_gate * jax.nn.softplus(-lam_ref[...])
    a = jnp.exp(log_a)
    om = -jnp.tanh(log_a) * (a * a + 1.0)
    root = jnp.where(om > 0.0, om * lax.rsqrt(om), 0.0)
    return a, root * (i_gate * xc)


def _scan_step(a, b, a_s, b_s):
    return a * a_s, b + a * b_s


def _rglru_prompt_tile(xr, gr, hc, rec_refs, xbuf_ref, abuf_ref, bbuf_ref):
    tm, rw = xr.shape
    sub = V7X_SUBLANES
    xbuf_ref[sub:sub + tm] = xr
    shifted = [xbuf_ref[sub - k:sub - k + tm] for k in range(1, CONV_WIDTH)]
    a, b = _rglru_coeffs(xr, shifted, rec_refs)
    abuf_ref[0:sub] = jnp.ones((sub, rw), _F32)
    bbuf_ref[0:sub] = jnp.zeros((sub, rw), _F32)
    d = 1
    while d < tm:
        if d % sub == 0:
            a_new, b_new = _scan_step(a[d:], b[d:], a[:tm - d], b[:tm - d])
            a = jnp.concatenate([a[:d], a_new], axis=0)
            b = jnp.concatenate([b[:d], b_new], axis=0)
        else:
            abuf_ref[sub:sub + tm] = a
            bbuf_ref[sub:sub + tm] = b
            a, b = _scan_step(a, b, abuf_ref[sub - d:sub - d + tm], bbuf_ref[sub - d:sub - d + tm])
        d *= 2
    h = a * hc + b
    return h * jax.nn.gelu(gr), h


def _proj_kernel(x_ref, gmix_ref, w_ref, wft_ref, bf_ref, gq_ref, gk_ref, gmat_ref, cum_ref,
                 aq_ref, ak_ref, cq_ref, ck_ref, *refs, tiles_per_seq, aw, rw):
    rec_refs = refs[:7]
    (k_out, v_out, lft_out, qs_out, kb_out, vt_out, augq_out, augk_out, rec_out, h_out, xtail_out,
     carry_ref, hc_ref, xbuf_ref, abuf_ref, bbuf_ref) = refs[7:]
    i = pl.program_id(0)
    tm = x_ref.shape[0]
    sub = V7X_SUBLANES
    seq_start = i % tiles_per_seq == 0

    @pl.when(seq_start)
    def _():
        carry_ref[...] = jnp.zeros_like(carry_ref)
        hc_ref[...] = jnp.zeros_like(hc_ref)
        xbuf_ref[0:sub] = jnp.zeros((sub, rw), _F32)

    @pl.when(jnp.logical_not(seq_start))
    def _():
        xbuf_ref[0:sub] = xbuf_ref[tm:tm + sub]

    hn = _rms(x_ref[...], gmix_ref[...]).astype(_BF16)
    w_rec = 3 * aw
    proj_rec = _dot(hn, w_ref[:, w_rec:w_rec + 2 * rw])
    xr = proj_rec[:, 0:rw]
    xtail_out[...] = xr[tm - sub:tm]

    y, h = _rglru_prompt_tile(xr, proj_rec[:, rw:2 * rw], hc_ref[...], rec_refs, xbuf_ref, abuf_ref, bbuf_ref)
    hc_ref[...] = h[tm - 1:tm]
    rec_out[...] = y
    h_out[...] = h[tm - sub:tm]

    proj = _dot(hn, w_ref[:, 0:w_rec])
    gmat = gmat_ref[...]
    qn = _head_norm(proj[:, 0:aw], gq_ref[...], gmat)
    kn = _head_norm(proj[:, aw:2 * aw], gk_ref[...], gmat)
    vt = proj[:, 2 * aw:3 * aw].T
    k_out[...] = kn.T
    v_out[...] = vt
    qs_out[...] = qn.astype(_BF16)
    kb_out[...] = kn.astype(_BF16)
    vt_out[...] = vt.astype(_BF16)

    ft = _dot_nt(wft_ref[...], hn)
    lft = jax.nn.log_sigmoid(ft[0:8] + bf_ref[...])
    lft_out[...] = lft

    p3 = _stack3(lft, V7X_LANES)
    blk = cum_ref.shape[0]
    carry = carry_ref[...]
    cums = []
    for c in range(tm // blk):
        r = _dot(p3[:, c * blk:(c + 1) * blk], cum_ref[...])
        cums.append(_sum3(r[:, :blk]) + carry)
        carry = carry + _sum3(r[:, blk:])
    carry_ref[...] = carry
    cum = jnp.concatenate(cums, axis=1)

    c3 = _stack3(cum * LOG2_E, V7X_LANES)
    qat = _dot(aq_ref[...], c3) + cq_ref[...]
    kat = _dot(ak_ref[...], c3) + ck_ref[...]
    augq_out[...] = qat.T.astype(_BF16)
    augk_out[...] = kat.T.astype(_BF16)


def _bias_placement(n_heads):
    aq = np.zeros((V7X_LANES, V7X_LANES), np.float32)
    ak = np.zeros((V7X_LANES, V7X_LANES), np.float32)
    cq = np.zeros((V7X_LANES, 1), np.float32)
    ck = np.zeros((V7X_LANES, 1), np.float32)
    for h in range(n_heads):
        for j in range(3):
            aq[h * BIAS_LANES + j, j * V7X_SUBLANES + h] = 1.0
            cq[h * BIAS_LANES + 3 + j, 0] = 1.0
            ak[h * BIAS_LANES + 3 + j, j * V7X_SUBLANES + h] = -1.0
            ck[h * BIAS_LANES + j, 0] = 1.0
    return aq, ak, cq, ck


def _group_mean_matrix(head_dim):
    idx = np.arange(V7X_MXU_DIM) // head_dim
    return (idx[:, None] == idx[None, :]).astype(np.float32) / head_dim


def _prefix_matrix():
    idx = np.arange(V7X_MXU_DIM)
    incl = (idx[:, None] <= idx[None, :]).astype(np.float32)
    return np.concatenate([incl, np.ones_like(incl)], axis=1)


def _proj_prompt(x2d, seq, gmix, w_main, wft, bf_col, gq, gk, rec_w, n_heads, head_dim, rw):
    n, d = x2d.shape
    tm = PROJ_PROMPT_ROWS
    aw = n_heads * head_dim
    aq, ak, cq, ck = _bias_placement(n_heads)
    consts = [
        gmix, w_main, wft, bf_col, gq, gk,
        jnp.asarray(_group_mean_matrix(head_dim), _BF16),
        jnp.asarray(_prefix_matrix(), _BF16),
        jnp.asarray(aq, _BF16), jnp.asarray(ak, _BF16), jnp.asarray(cq), jnp.asarray(ck),
        *rec_w,
    ]
    tps = seq // tm
    batch = n // seq
    row = lambda w: pl.BlockSpec((tm, w), lambda i: (i, 0))
    seq_t = lambda r: pl.BlockSpec((None, r, tm), lambda i: (i // tps, 0, i % tps))
    out_shape = (
        jax.ShapeDtypeStruct((batch, aw, seq), _F32),
        jax.ShapeDtypeStruct((batch, aw, seq), _F32),
        jax.ShapeDtypeStruct((batch, V7X_SUBLANES, seq), _F32),
        jax.ShapeDtypeStruct((n, aw), _BF16),
        jax.ShapeDtypeStruct((n, aw), _BF16),
        jax.ShapeDtypeStruct((batch, aw, seq), _BF16),
        jax.ShapeDtypeStruct((n, V7X_LANES), _BF16),
        jax.ShapeDtypeStruct((n, V7X_LANES), _BF16),
        jax.ShapeDtypeStruct((n, rw), _F32),
        jax.ShapeDtypeStruct((n // tm * V7X_SUBLANES, rw), _F32),
        jax.ShapeDtypeStruct((n // tm * V7X_SUBLANES, rw), _F32),
    )
    tail8 = pl.BlockSpec((V7X_SUBLANES, rw), lambda i: (i, 0))
    out_specs = (
        seq_t(aw), seq_t(aw), seq_t(V7X_SUBLANES), row(aw), row(aw), seq_t(aw),
        row(V7X_LANES), row(V7X_LANES), row(rw), tail8, tail8,
    )
    return pl.pallas_call(
        functools.partial(_proj_kernel, tiles_per_seq=tps, aw=aw, rw=rw),
        grid=(n // tm,),
        in_specs=[row(d)] + [_const_spec(c.shape) for c in consts],
        out_specs=out_specs,
        out_shape=out_shape,
        scratch_shapes=[pltpu.VMEM((V7X_SUBLANES, V7X_MXU_DIM), _F32), pltpu.VMEM((1, rw), _F32)]
        + [pltpu.VMEM((V7X_SUBLANES + tm, rw), _F32)] * 3,
        compiler_params=_params(1),
        name="proj_prompt",
    )(x2d, *consts)


class _PagedSide:
    def __init__(self, small, pages, pps, page, uo_ref, o_ref, scratch, t_new, head_dim, n_pages):
        self.q_ref, self.kn_ref, self.vn_ref, self.cnq_ref, self.cnk_ref = small
        self.k_page, self.v_page, self.lf_page = pages
        self.pps, self.page, self.uo_ref, self.o_ref = pps, page, uo_ref, o_ref
        self.qbd_ref, self.m_ref, self.l_ref, self.acc_ref, self.carry_ref = scratch
        self.t_new, self.head_dim, self.n_pages = t_new, head_dim, n_pages
        self.nh = V7X_SUBLANES
        self.aw = self.q_ref.shape[1]

    def _head_mask(self):
        feat = lax.broadcasted_iota(jnp.int32, (self.nh, self.aw), 1)
        head = lax.broadcasted_iota(jnp.int32, (self.nh, self.aw), 0)
        return (feat >= head * self.head_dim) & (feat < (head + 1) * self.head_dim)

    def init(self):
        nh, aw, page, t_new = self.nh, self.aw, self.page, self.t_new
        rows = t_new * nh
        q = self.q_ref[...]
        head_mask = self._head_mask()
        qbd = jnp.concatenate(
            [jnp.where(head_mask, jnp.broadcast_to(q[t:t + 1], (nh, aw)), 0.0) for t in range(t_new)],
            axis=0).astype(_BF16)
        self.qbd_ref[...] = qbd
        pad = jnp.zeros((page - self.kn_ref.shape[0], aw), _F32)
        kn = jnp.concatenate([self.kn_ref[...], pad], axis=0).astype(_BF16)
        vn = jnp.concatenate([self.vn_ref[...], pad], axis=0).astype(_BF16)
        s = _dot_nt(qbd, kn) + self.cnq_ref[...] - self.cnk_ref[...]
        row = lax.broadcasted_iota(jnp.int32, (rows, page), 0)
        t_col = lax.broadcasted_iota(jnp.int32, (rows, page), 1)
        s = jnp.where(t_col * nh <= row, s, -jnp.inf)
        m = jnp.max(s, axis=1, keepdims=True)
        p = jnp.exp(s - m)
        self.m_ref[...] = m
        self.l_ref[...] = jnp.sum(p, axis=1, keepdims=True)
        self.acc_ref[...] = _dot(p.astype(_BF16), vn)
        self.carry_ref[...] = jnp.zeros_like(self.carry_ref)

    def scores(self, step):
        pps, page = self.pps, self.page
        qbd = self.qbd_ref[...]
        carry = self.carry_ref[...]
        scores = []
        for i in range(pps):
            valid = step * pps + i < self.n_pages
            s = _dot(qbd, self.k_page(i).astype(_BF16))
            lf = jnp.where(valid, self.lf_page(i), 0.0)
            r = _dot(_stack3(lf, 4 * V7X_SUBLANES), self.uo_ref[...])
            bias = _sum3(r[:, :page]) + carry
            carry = carry + _sum3(r[:, page:])
            s = s + jnp.concatenate([bias] * self.t_new, axis=0)
            scores.append(jnp.where(valid, s, -jnp.inf))
        self.carry_ref[...] = carry
        s_all = jnp.concatenate(scores, axis=1) + self.cnq_ref[...]
        m_old = self.m_ref[...]
        m_new = jnp.maximum(m_old, jnp.max(s_all, axis=1, keepdims=True))
        alpha = jnp.exp(m_old - m_new)
        p = jnp.exp(s_all - m_new)
        self.l_ref[...] = alpha * self.l_ref[...] + jnp.sum(p, axis=1, keepdims=True)
        self.m_ref[...] = m_new
        return p.astype(_BF16), alpha

    def values(self, p, alpha):
        vt_all = jnp.concatenate([self.v_page(i).astype(_BF16) for i in range(self.pps)], axis=1)
        self.acc_ref[...] = alpha * self.acc_ref[...] + _dot_nt(p, vt_all)

    def finish(self):
        nh = self.nh
        a = self.acc_ref[...] / self.l_ref[...]
        head_mask = self._head_mask()
        for t in range(self.t_new):
            self.o_ref[t:t + 1, :] = jnp.sum(jnp.where(head_mask, a[t * nh:(t + 1) * nh], 0.0),
                                             axis=0, keepdims=True)


def _fox_kernel(qi_tab, ki_tab, pt_ref, q_ref, aq_ref, k_ref, ak_ref, vt_ref, *refs,
                n_heads, head_dim, n_samp, pps, t_new, n_pages):
    n_small = 5
    small = refs[:n_small * n_samp]
    rest = refs[n_small * n_samp:]
    cache_refs = rest[0:3]
    uo_ref, o_ref = rest[3], rest[4]
    so_refs = rest[5:5 + n_samp]
    qf_ref, m_ref, acc_ref = rest[5 + n_samp:8 + n_samp]
    s_scratch = rest[8 + n_samp:8 + 6 * n_samp]
    page_bufs = rest[8 + 6 * n_samp:11 + 6 * n_samp]
    sem = rest[11 + 6 * n_samp]

    batch_i = pl.program_id(0)
    step = pl.program_id(1)
    n_steps = pl.num_programs(1)
    g = batch_i * n_steps + step
    slot = g % 2
    page = page_bufs[0].shape[3]

    def page_copies(b2, s2, dst_slot, lookup):
        copies = []
        for u in range(n_samp):
            for i in range(pps):
                pid = 0
                if lookup:
                    pid = pt_ref[b2 * n_samp + u, jnp.maximum(n_pages - 1 - (s2 * pps + i), 0)]
                for c, (cache, buf) in enumerate(zip(cache_refs, page_bufs)):
                    copies.append(pltpu.make_async_copy(cache.at[pid], buf.at[dst_slot, u * pps + i],
                                                        sem.at[dst_slot, c]))
        return copies

    def start_all(copies):
        for n, c in enumerate(copies):
            c.start(priority=(n // len(cache_refs)) % 2)

    @pl.when(g == 0)
    def _first_pages():
        start_all(page_copies(0, 0, 0, True))

    @pl.when(g + 1 < pl.num_programs(0) * n_steps)
    def _next_pages():
        wrap = step + 1 == n_steps
        start_all(page_copies(jnp.where(wrap, batch_i + 1, batch_i), jnp.where(wrap, 0, step + 1),
                              1 - slot, True))

    for c in page_copies(batch_i, step, slot, False):
        c.wait()

    def reader(buf, u):
        return lambda i: buf[slot, u * pps + i]

    paged = [
        _PagedSide(small[u * n_small:(u + 1) * n_small], [reader(buf, u) for buf in page_bufs], pps, page,
                   uo_ref, so_refs[u], s_scratch[u * 5:(u + 1) * 5], t_new, head_dim, n_pages)
        for u in range(n_samp)]

    qi = qi_tab[step]
    ki = ki_tab[step]
    bq = q_ref.shape[0]
    bk = k_ref.shape[0]
    pair = 2 * head_dim

    @pl.when(step == 0)
    def _paged_init():
        for side in paged:
            side.init()

    @pl.when(ki == 0)
    def _init():
        lane = lax.broadcasted_iota(jnp.int32, (bq, V7X_LANES), 1)
        a = aq_ref[...]
        zero = jnp.zeros_like(a)
        for h in range(n_heads):
            q2 = q_ref[:, (h // 2) * pair:(h // 2 + 1) * pair]
            lo = head_dim * (h % 2)
            qh = jnp.where((lane >= lo) & (lane < lo + head_dim), q2, zero)
            ah = jnp.where((lane >= h * BIAS_LANES) & (lane < (h + 1) * BIAS_LANES), a, zero)
            qf_ref[h] = jnp.concatenate([qh, ah], axis=1)
        m_ref[...] = jnp.full_like(m_ref, -jnp.inf)
        acc_ref[...] = jnp.zeros_like(acc_ref)

    def step_body(masked):
        ak = ak_ref[...]
        hk, hq = bk // 2, bq // 2
        ones_blk = (lax.broadcasted_iota(jnp.int32, (V7X_BF16_SUBLANES, bk), 0) == 0).astype(_BF16)
        if masked:
            tri = (lax.broadcasted_iota(jnp.int32, (hk, hq), 0)
                   <= lax.broadcasted_iota(jnp.int32, (hk, hq), 1))

        def scores(h):
            hp = h // 2
            kf = jnp.concatenate([k_ref[:, hp * pair:(hp + 1) * pair], ak], axis=1)
            if not masked:
                return _dot_nt(kf, qf_ref[h])
            return (_dot_nt(kf[:hk], qf_ref[h]),
                    _dot_nt(kf[hk:], qf_ref[h, hq:, :]))

        paged_p = [side.scores(step) for side in paged]
        ahead = [scores(h) for h in range(FOX_LOOKAHEAD)]
        for side, (p_s, alpha_s) in zip(paged, paged_p):
            side.values(p_s, alpha_s)
        for h in range(n_heads):
            s = ahead.pop(0)
            if h + FOX_LOOKAHEAD < n_heads:
                ahead.append(scores(h + FOX_LOOKAHEAD))
            m_old = m_ref[h]
            lhs = jnp.concatenate([vt_ref[h * head_dim:(h + 1) * head_dim, :], ones_blk], axis=0)
            if not masked:
                m_new = jnp.maximum(m_old, jnp.max(s, axis=0, keepdims=True))
                pv = _dot(lhs, jnp.exp2(s - m_new).astype(_BF16))
            else:
                s_a, s_b = s
                s_a = jnp.concatenate([jnp.where(tri, s_a[:, :hq], -jnp.inf), s_a[:, hq:]], axis=1)
                s_b = jnp.where(tri, s_b, -jnp.inf)
                top = jnp.max(s_a, axis=0, keepdims=True)
                top = jnp.concatenate([top[:, :hq], jnp.maximum(top[:, hq:], jnp.max(s_b, axis=0, keepdims=True))],
                                      axis=1)
                m_new = jnp.maximum(m_old, top)
                pv = _dot(lhs[:, :hk], jnp.exp2(s_a - m_new).astype(_BF16))
                pv_b = _dot(lhs[:, hk:], jnp.exp2(s_b - m_new[:, hq:]).astype(_BF16))
                pv = jnp.concatenate([pv[:, :hq], pv[:, hq:] + pv_b], axis=1)
            acc_ref[h] = jnp.exp2(m_old - m_new) * acc_ref[h] + pv
            m_ref[h] = m_new

    @pl.when(ki < qi)
    def _off_diagonal():
        step_body(False)

    @pl.when(ki == qi)
    def _diagonal():
        step_body(True)
        for hp in range(n_heads // 2):
            outs = []
            for h in (2 * hp, 2 * hp + 1):
                a = acc_ref[h]
                outs.append(a[0:head_dim] / a[head_dim:head_dim + 1])
            o_ref[:, hp * pair:(hp + 1) * pair] = jnp.concatenate(outs, axis=0).T

    @pl.when(step == pl.num_programs(1) - 1)
    def _paged_finish():
        for side in paged:
            side.finish()


def _suffix_matrix(page):
    idx = np.arange(page)
    strict = (idx[:, None] > idx[None, :]).astype(np.float32)
    return np.concatenate([strict, np.ones_like(strict)], axis=1)


def _attention(qs, augq, kb, augk, vt, batch, seq, q_s, k_new, v_new, cn, cache_kt, cache_vt, cache_lft,
               page_table, t_new, n_heads, head_dim):
    aw = n_heads * head_dim
    bq = bk = ATTN_BLOCK
    nq = seq // bq
    qs3 = qs.reshape(batch, seq, aw)
    kb3 = kb.reshape(batch, seq, aw)
    aq3 = augq.reshape(batch, seq, V7X_LANES)
    ak3 = augk.reshape(batch, seq, V7X_LANES)
    pairs = [(qi, ki) for qi in range(nq) for ki in range(qi + 1)]
    qi_tab = jnp.asarray([p[0] for p in pairs], jnp.int32)
    ki_tab = jnp.asarray([p[1] for p in pairs], jnp.int32)
    n_steps = len(pairs)

    db, n_pages = page_table.shape
    page = cache_kt.shape[2]
    assert db % batch == 0, "sample batches must split evenly over the prompt batches"
    n_samp = db // batch
    pps = -(-n_pages // n_steps)
    rows = t_new * n_heads
    pad_t = V7X_SUBLANES - t_new
    q3 = q_s.reshape(db, t_new, aw)
    kn = jnp.pad(k_new.reshape(db, t_new, aw), ((0, 0), (0, pad_t), (0, 0)))
    vn = jnp.pad(v_new.reshape(db, t_new, aw), ((0, 0), (0, pad_t), (0, 0)))
    cn3 = cn.reshape(db, t_new, n_heads)
    cnq = cn3.reshape(db, rows, 1)
    cnk = jnp.tile(jnp.swapaxes(cn3, 1, 2), (1, t_new, 1))
    cnk = jnp.pad(cnk, ((0, 0), (0, 0), (0, page - t_new)))
    uo = jnp.asarray(_suffix_matrix(page), _BF16)

    qmap = lambda b, s, qt, kt, pt: (b, qt[s], 0)
    kmap = lambda b, s, qt, kt, pt: (b, kt[s], 0)

    def samp_spec(shape, u):
        return pl.BlockSpec((None,) + shape, lambda b, s, qt, kt, pt: (b * n_samp + u, 0, 0))

    small_specs, small_args = [], []
    for u in range(n_samp):
        small_specs += [samp_spec((t_new, aw), u), samp_spec((V7X_SUBLANES, aw), u),
                        samp_spec((V7X_SUBLANES, aw), u), samp_spec((rows, 1), u), samp_spec((rows, page), u)]
        small_args += [q3, kn, vn, cnq, cnk]
    caches = (cache_kt, cache_vt, cache_lft)

    grid_spec = pltpu.PrefetchScalarGridSpec(
        num_scalar_prefetch=3,
        grid=(batch, n_steps),
        in_specs=[
            pl.BlockSpec((None, bq, aw), qmap),
            pl.BlockSpec((None, bq, V7X_LANES), qmap),
            pl.BlockSpec((None, bk, aw), kmap),
            pl.BlockSpec((None, bk, V7X_LANES), kmap),
            pl.BlockSpec((None, aw, bk), lambda b, s, qt, kt, pt: (b, 0, kt[s])),
        ] + small_specs + [pl.BlockSpec(memory_space=pl.ANY)] * len(caches)
        + [pl.BlockSpec(uo.shape, lambda b, s, qt, kt, pt: (0, 0))],
        out_specs=[pl.BlockSpec((None, bq, aw), qmap)]
        + [pl.BlockSpec((None, t_new, aw), lambda b, s, qt, kt, pt: (b, 0, 0))] * n_samp,
        scratch_shapes=[
            pltpu.VMEM((n_heads, bq, 2 * V7X_LANES), _BF16),
            pltpu.VMEM((n_heads, 1, bq), _F32),
            pltpu.VMEM((n_heads, head_dim + V7X_BF16_SUBLANES, bq), _F32),
        ] + [
            pltpu.VMEM((rows, aw), _BF16),
            pltpu.VMEM((rows, 1), _F32),
            pltpu.VMEM((rows, 1), _F32),
            pltpu.VMEM((rows, aw), _F32),
            pltpu.VMEM((V7X_SUBLANES, page), _F32),
        ] * n_samp + [
            pltpu.VMEM((2, n_samp * pps) + c.shape[1:], c.dtype) for c in caches
        ] + [pltpu.SemaphoreType.DMA((2, len(caches)))],
    )
    outs = pl.pallas_call(
        functools.partial(_fox_kernel, n_heads=n_heads, head_dim=head_dim, n_samp=n_samp, pps=pps,
                          t_new=t_new, n_pages=n_pages),
        grid_spec=grid_spec,
        out_shape=[jax.ShapeDtypeStruct((batch, seq, aw), _F32)]
        + [jax.ShapeDtypeStruct((batch, t_new, aw), _F32)] * n_samp,
        compiler_params=_params(2),
        name="attention",
    )(qi_tab, ki_tab, page_table, qs3, aq3, kb3, ak3, vt, *small_args, *caches, uo)
    att = outs[0].reshape(batch * seq, aw)
    att_s = jnp.stack(outs[1:], axis=1)
    return att, att_s.reshape(db * t_new, aw)


def _rglru_sample_kernel(xr_ref, gr_ref, p1_ref, p2_ref, p3_ref, h0_ref, *refs, seg_len):
    rec_refs = refs[:7]
    y_ref, h_ref = refs[7:]
    tm, rw = xr_ref.shape
    xr = xr_ref[...]
    t = lax.broadcasted_iota(jnp.int32, (tm, rw), 0) & (seg_len - 1)
    prev = (p1_ref, p2_ref, p3_ref)
    shifted = [jnp.where(t >= k, pltpu.roll(xr, k, 0), prev[k - 1][...]) for k in range(1, CONV_WIDTH)]
    a, b = _rglru_coeffs(xr, shifted, rec_refs)
    d = 1
    while d < seg_len:
        a_new, b_new = _scan_step(a, b, pltpu.roll(a, d, 0), pltpu.roll(b, d, 0))
        a = jnp.where(t >= d, a_new, a)
        b = jnp.where(t >= d, b_new, b)
        d *= 2
    h = a * h0_ref[...] + b
    y_ref[...] = h * jax.nn.gelu(gr_ref[...])
    h_ref[...] = h


def _rglru_sample(xr, gr, prevs, h0, seg_len, rec_w):
    n, rw = xr.shape
    assert seg_len & (seg_len - 1) == 0 and n % seg_len == 0
    full = pl.BlockSpec((n, rw), lambda i: (0, 0))
    return pl.pallas_call(
        functools.partial(_rglru_sample_kernel, seg_len=seg_len),
        grid=(1,),
        in_specs=[full] * 6 + [_const_spec(c.shape) for c in rec_w],
        out_specs=(full, full),
        out_shape=(jax.ShapeDtypeStruct((n, rw), _F32), jax.ShapeDtypeStruct((n, rw), _F32)),
        compiler_params=_params(1),
        name="rglru_sample",
    )(xr, gr, *prevs, h0, *rec_w)


def _tail_math(x, att, rec, p, goa_ref, gor_ref, wout_ref, gffn_ref, wi_ref, wo_ref, wple_ref,
               gple_ref, wpg_ref):
    aw = att.shape[1]
    an = _rms(att, goa_ref[...]).astype(_BF16)
    rn = _rms(rec, gor_ref[...]).astype(_BF16)
    x1 = x + _dot(an, wout_ref[0:aw, :]) + _dot(rn, wout_ref[aw:, :])
    hn = _rms(x1, gffn_ref[...]).astype(_BF16)
    hidden = wo_ref.shape[0]
    ffn = None
    for c0 in range(0, hidden, FFN_CHUNK):
        cw = min(FFN_CHUNK, hidden - c0)
        g = _dot(hn, wi_ref[:, c0:c0 + cw])
        u = _dot(hn, wi_ref[:, hidden + c0:hidden + c0 + cw])
        part = _dot((jax.nn.silu(g) * u).astype(_BF16), wo_ref[c0:c0 + cw, :])
        ffn = part if ffn is None else ffn + part
    x2 = x1 + ffn
    gate = jax.nn.sigmoid(_dot(_rms(x2, gple_ref[...]).astype(_BF16), wpg_ref[...]))
    return x2 + _dot(p.astype(_BF16), wple_ref[...]) * gate


def _tail_kernel(x_ref, att_ref, rec_ref, p_ref, *refs):
    o_ref = refs[-1]
    o_ref[...] = _tail_math(x_ref[...], att_ref[...], rec_ref[...], p_ref[...], *refs[:-1])


def _tail(x2d, att, rec, p2d, tail_w):
    n, d = x2d.shape
    tm = min(PROJ_ROWS, n)
    row = lambda w: pl.BlockSpec((tm, w), lambda i: (i, 0))
    return pl.pallas_call(
        _tail_kernel,
        grid=(n // tm,),
        in_specs=[row(d), row(att.shape[1]), row(rec.shape[1]), row(p2d.shape[1])]
        + [_const_spec(c.shape) for c in tail_w],
        out_specs=row(d),
        out_shape=jax.ShapeDtypeStruct((n, d), _F32),
        compiler_params=_params(1),
        name="tail",
    )(x2d, att, rec, p2d, *tail_w)


def _sproj_kernel(x_ref, gmix_ref, w_ref, wf_ref, bf_ref, gq_ref, gk_ref, gmat_ref, seg_ref,
                  q_out, k_out, v_out, lf_out, cn_out, xr_out, gr_out, *, aw, rw):
    hn = _rms(x_ref[...], gmix_ref[...]).astype(_BF16)
    proj = _dot(hn, w_ref[...])
    gmat = gmat_ref[...]
    q_out[...] = _head_norm(proj[:, 0:aw], gq_ref[...], gmat)
    k_out[...] = _head_norm(proj[:, aw:2 * aw], gk_ref[...], gmat)
    v_out[...] = proj[:, 2 * aw:3 * aw]
    xr_out[...] = proj[:, 3 * aw:3 * aw + rw]
    gr_out[...] = proj[:, 3 * aw + rw:3 * aw + 2 * rw]
    lf = jax.nn.log_sigmoid(_dot(hn, wf_ref[...]) + bf_ref[...])
    lf_out[...] = lf
    hi, mid, lo = _split3(lf)
    seg = seg_ref[...]
    cn_out[...] = _dot(seg, hi.astype(_BF16)) + _dot(seg, mid.astype(_BF16)) + _dot(seg, lo.astype(_BF16))


def _proj_sample(x2d, t_new, gmix, w_main, wf_pad, bf_row, gq, gk, n_heads, head_dim, rw):
    n, d = x2d.shape
    aw = n_heads * head_dim
    idx = np.arange(n)
    seg = ((idx[:, None] // t_new == idx[None, :] // t_new) & (idx[None, :] <= idx[:, None])).astype(np.float32)
    ins = [x2d, gmix, w_main, wf_pad, bf_row, gq, gk,
           jnp.asarray(_group_mean_matrix(head_dim), _BF16), jnp.asarray(seg, _BF16)]
    full = lambda s: pl.BlockSpec(s, lambda i: (0,) * len(s))
    widths = (aw, aw, aw, V7X_LANES, V7X_LANES, rw, rw)
    return pl.pallas_call(
        functools.partial(_sproj_kernel, aw=aw, rw=rw),
        grid=(1,),
        in_specs=[full(a.shape) for a in ins],
        out_specs=tuple(full((n, w)) for w in widths),
        out_shape=tuple(jax.ShapeDtypeStruct((n, w), _F32) for w in widths),
        compiler_params=_params(1),
        name="proj_sample",
    )(*ins)


def _block_diag_pair(w):
    nb, dd, _ = w.shape
    per = V7X_MXU_DIM // dd
    tiles = []
    for half in range(nb // per):
        tile = jnp.zeros((V7X_MXU_DIM, V7X_MXU_DIM), w.dtype)
        for j in range(per):
            tile = lax.dynamic_update_slice(tile, w[half * per + j], (j * dd, j * dd))
        tiles.append(tile)
    return jnp.stack(tiles).astype(_BF16)


def kernel(x_prompt, x_sample, p_prompt, p_sample, cache_k, cache_v, cache_logf, state_conv, state_h, page_table, g_mix, w_in, b_f, g_q, g_k, w_conv, b_conv, w_a, b_a, w_x, b_x, lam, g_out_attn, g_out_rec, w_out, g_ffn, w_ffn_in, w_ffn_out, w_ple, g_ple, w_ple_gate):
    batch, seq, d_model = x_prompt.shape
    db, t_new, _ = x_sample.shape
    depth, n_heads, head_dim = g_q.shape
    aw = n_heads * head_dim
    rw = lam.shape[1]
    hidden = w_ffn_out.shape[1]
    n_phys, page = cache_k.shape[1], cache_k.shape[2]
    assert n_heads == V7X_SUBLANES and aw == 2 * V7X_MXU_DIM and rw == 2 * V7X_MXU_DIM
    assert seq % ATTN_BLOCK == 0 and seq % PROJ_ROWS == 0 and seq % PROJ_PROMPT_ROWS == 0

    xp = x_prompt.reshape(batch * seq, d_model)
    xs = x_sample.reshape(db * t_new, d_model)
    outs = [[] for _ in range(10)]
    row = lambda a: a.reshape(1, -1)
    for l in range(depth):
        w_l = w_in[l]
        w_main = jnp.concatenate([w_l[:, :3 * aw], w_l[:, 3 * aw + n_heads:]], axis=1).astype(_BF16)
        w_f = w_l[:, 3 * aw:3 * aw + n_heads]
        wft = jnp.pad(w_f.T, ((0, V7X_BF16_SUBLANES - n_heads), (0, 0))).astype(_BF16)
        wf_pad = jnp.pad(w_f, ((0, 0), (0, V7X_LANES - n_heads))).astype(_BF16)
        bf_col = b_f[l].reshape(n_heads, 1)
        bf_row = jnp.pad(b_f[l], (0, V7X_LANES - n_heads)).reshape(1, V7X_LANES)
        gmix = row(g_mix[l])
        gq = row(g_q[l]) * (head_dim ** -0.5)
        gk = row(g_k[l])
        rec_w = (w_conv[l], row(b_conv[l]), _block_diag_pair(w_a[l]), row(b_a[l]),
                 _block_diag_pair(w_x[l]), row(b_x[l]), row(lam[l]))
        tail_w = (row(g_out_attn[l]), row(g_out_rec[l]), w_out[l].astype(_BF16), row(g_ffn[l]),
                  w_ffn_in[l].astype(_BF16),
                  w_ffn_out[l].astype(_BF16), w_ple[l].astype(_BF16), row(g_ple[l]),
                  w_ple_gate[l].astype(_BF16))

        kt, vt32, lft, qs, kb, vt, augq, augk, rec, h_tiles, x_tiles = _proj_prompt(
            xp, seq, gmix, w_main, wft, bf_col, gq * LOG2_E, gk, rec_w, n_heads, head_dim, rw)
        q_s, k_s, v_s, lf_s, cn_s, xr_s, gr_s = _proj_sample(
            xs, t_new, gmix, w_main, wf_pad, bf_row, gq, gk, n_heads, head_dim, rw)
        lf_s = lf_s[:, :n_heads]
        cache_lft = jnp.swapaxes(cache_logf[l], 1, 2)
        cache_kt = cache_k[l].transpose(0, 2, 3, 1).reshape(n_phys, aw, page)
        cache_vt = cache_v[l].transpose(0, 2, 3, 1).reshape(n_phys, aw, page)
        att, att_s = _attention(qs, augq, kb, augk, vt, batch, seq, q_s, k_s, v_s, cn_s[:, :n_heads],
                                cache_kt, cache_vt, cache_lft, page_table, t_new, n_heads, head_dim)

        xp = _tail(xp, att, rec, p_prompt[l].reshape(batch * seq, -1), tail_w)
        outs[0].append(kt.reshape(batch, n_heads, head_dim, seq).transpose(0, 3, 1, 2))
        outs[1].append(vt32.reshape(batch, n_heads, head_dim, seq).transpose(0, 3, 1, 2))
        outs[2].append(lft.transpose(0, 2, 1))
        outs[3].append(x_tiles.reshape(batch, -1, V7X_SUBLANES, rw)[:, -1, V7X_SUBLANES - (CONV_WIDTH - 1):])
        outs[4].append(h_tiles.reshape(batch, -1, V7X_SUBLANES, rw)[:, -1, -1])

        hist = jnp.concatenate([state_conv[l], jnp.zeros((db, t_new, rw), _F32)], axis=1)
        prevs = tuple(hist[:, CONV_WIDTH - 1 - s:CONV_WIDTH - 1 - s + t_new].reshape(db * t_new, rw)
                      for s in range(1, CONV_WIDTH))
        h0 = jnp.repeat(state_h[l], t_new, axis=0)
        rec_s, h_s = _rglru_sample(xr_s, gr_s, prevs, h0, t_new, rec_w)
        xs = _tail(xs, att_s, rec_s, p_sample[l].reshape(db * t_new, -1), tail_w)
        outs[5].append(k_s.reshape(db, t_new, n_heads, head_dim))
        outs[6].append(v_s.reshape(db, t_new, n_heads, head_dim))
        outs[7].append(lf_s.reshape(db, t_new, n_heads))
        outs[8].append(xr_s.reshape(db, t_new, rw)[:, t_new - (CONV_WIDTH - 1):])
        outs[9].append(h_s.reshape(db, t_new, rw)[:, -1])
    return (xp.reshape(batch, seq, d_model), xs.reshape(db, t_new, d_model),
            *(jnp.stack(o) for o in outs))
```
